```python
import math
import jax, jax.numpy as jnp
from jax import lax
import numpy as np


D_MODEL = 1024
BATCH = 16
SEQ = 4096
DEPTH = 4

CHUNK = 64
N_META = 16
Q_BLOCK = 2 * CHUNK
SSD_CHUNK = 2 * CHUNK
NORM_EPS = 1e-6

ATTN_HEADS = 16
ATTN_HEAD_DIM = 64
D_ATTN = ATTN_HEADS * ATTN_HEAD_DIM
FORGET_BIAS_INIT = 3.0

SSD_HEAD_DIM = 64
D_SSD = D_MODEL
SSD_HEADS = D_SSD // SSD_HEAD_DIM
SSD_GROUPS = 2
SSD_STATE = 128
SSD_CONV = 4
D_XBC = D_SSD + 2 * SSD_GROUPS * SSD_STATE

D_LRU = D_MODEL
LRU_BLOCKS = 16
LRU_BLOCK_DIM = D_LRU // LRU_BLOCKS
LRU_CONV = 4
LRU_C = 8.0

D_FF = 2816

N_BRANCH = 3
IN_SIZES = (D_ATTN, D_ATTN, D_ATTN, ATTN_HEADS, D_SSD, D_XBC, SSD_HEADS, D_LRU, D_LRU, N_BRANCH * D_MODEL)
N_IN = 3 * D_ATTN + ATTN_HEADS + D_SSD + D_XBC + SSD_HEADS + 2 * D_LRU + N_BRANCH * D_MODEL

kernel_name = 'hybrid_fox_ssd_rglru_macaron'


def _in_split_points():
    return [int(v) for v in np.cumsum(IN_SIZES)[:-1]]


def rms_norm(x, g):
    xf = x.astype(jnp.float32)
    y = xf * lax.rsqrt(jnp.mean(xf * xf, axis=-1, keepdims=True) + NORM_EPS)
    return (y * g.astype(jnp.float32)).astype(x.dtype)


def swiglu_ffn(h, w_gate_up, w_down):
    g, u = jnp.split(h @ w_gate_up, 2, axis=-1)
    return (jax.nn.silu(g) * u) @ w_down


def causal_depthwise_conv(x, w, b):
    k_width, c = w.shape
    y = lax.conv_general_dilated(x, w[:, None, :].astype(x.dtype), window_strides=(1,),
                                 padding=[(k_width - 1, 0)],
                                 dimension_numbers=('NWC', 'WIO', 'NWC'),
                                 feature_group_count=c)
    return y + b.astype(y.dtype)


def forgetting_attention(q, k, v, log_f):
    b, L, h, dh = q.shape
    c = jnp.cumsum(log_f, axis=1).transpose(0, 2, 1)
    scale = dh ** -0.5
    outs = []
    for start in range(0, L, Q_BLOCK):
        end = start + Q_BLOCK
        s = jnp.einsum('bqhd,bkhd->bhqk', q[:, start:end], k[:, :end],
                       preferred_element_type=jnp.float32) * scale
        bias = c[:, :, start:end, None] - c[:, :, None, :end]
        visible = jnp.arange(end)[None, :] <= jnp.arange(start, end)[:, None]
        p = jax.nn.softmax(jnp.where(visible, s + bias, -jnp.inf), axis=-1)
        outs.append(jnp.einsum('bhqk,bkhd->bqhd', p.astype(v.dtype), v[:, :end]))
    return jnp.concatenate(outs, axis=1)


def segsum(a):
    t = a.shape[-1]
    a_rep = jnp.broadcast_to(a[..., :, None], a.shape + (t,))
    a_rep = jnp.where(jnp.tril(jnp.ones((t, t), bool), -1), a_rep, 0.0)
    cs = jnp.cumsum(a_rep, axis=-2)
    return jnp.where(jnp.tril(jnp.ones((t, t), bool)), cs, -jnp.inf)


def ssd_chunked_scan(x, dt, a, bm, cm):
    b, L, h, p = x.shape
    g, n = bm.shape[2], bm.shape[3]
    e = h // g
    nc = L // SSD_CHUNK
    q = SSD_CHUNK
    xdt = (x * dt[..., None]).reshape(b, nc, q, g, e, p)
    da = (dt * a).reshape(b, nc, q, g, e).transpose(0, 3, 4, 1, 2)
    bc = bm.reshape(b, nc, q, g, n)
    cc = cm.reshape(b, nc, q, g, n)
    a_cum = jnp.cumsum(da, axis=-1)
    decay_in = jnp.exp(segsum(da))
    cb = jnp.einsum('bclgn,bcsgn->bgcls', cc, bc)
    y_diag = jnp.einsum('bgecls,bcsgep->bclgep', cb[:, :, None] * decay_in, xdt)
    decay_to_end = jnp.exp(a_cum[..., -1:] - a_cum).transpose(0, 3, 4, 1, 2)
    states = jnp.einsum('bclgn,bclgep->bcgepn', bc, xdt * decay_to_end[..., None])
    states = jnp.concatenate([jnp.zeros_like(states[:, :1]), states], axis=1)
    chunk_a = jnp.pad(a_cum[..., -1], [(0, 0), (0, 0), (0, 0), (1, 0)])
    chunk_decay = jnp.exp(segsum(chunk_a))
    prev_states = jnp.einsum('bgezc,bcgepn->bzgepn', chunk_decay, states)[:, :-1]
    decay_from_start = jnp.exp(a_cum).transpose(0, 3, 4, 1, 2)
    y_off = jnp.einsum('bclgn,bcgepn->bclgep', cc, prev_states) * decay_from_start[..., None]
    return (y_diag + y_off).reshape(b, L, h, p)


def mamba2_branch(z, xbc, dt, conv_w, conv_b, dt_bias, a_log, d_skip, norm_w):
    b, L, _ = z.shape
    f32 = jnp.float32
    xbc = jax.nn.silu(causal_depthwise_conv(xbc, conv_w, conv_b)).astype(f32)
    xs, bm, cm = jnp.split(xbc, [D_SSD, D_SSD + SSD_GROUPS * SSD_STATE], axis=-1)
    dt = jax.nn.softplus(dt.astype(f32) + dt_bias.astype(f32))
    a = -jnp.exp(a_log.astype(f32))
    xh = xs.reshape(b, L, SSD_HEADS, SSD_HEAD_DIM)
    y = ssd_chunked_scan(xh, dt, a,
                         bm.reshape(b, L, SSD_GROUPS, SSD_STATE),
                         cm.reshape(b, L, SSD_GROUPS, SSD_STATE))
    y = y + d_skip.astype(f32)[:, None] * xh
    gsz = D_SSD // SSD_GROUPS
    y = y.reshape(b, L, SSD_GROUPS, gsz) * jax.nn.silu(z.astype(f32)).reshape(b, L, SSD_GROUPS, gsz)
    y = y * lax.rsqrt(jnp.mean(y * y, axis=-1, keepdims=True) + NORM_EPS)
    return (y.reshape(b, L, D_SSD) * norm_w.astype(f32)).astype(z.dtype)


def rglru_branch(xr, gate, conv_w, conv_b, w_a, b_a, w_x, b_x, lam):
    b, L, _ = xr.shape
    f32 = jnp.float32
    xc = causal_depthwise_conv(xr, conv_w, conv_b).astype(f32)
    xb = xc.reshape(b, L, LRU_BLOCKS, LRU_BLOCK_DIM)
    r = jax.nn.sigmoid(jnp.einsum('blhi,hij->blhj', xb, w_a.astype(f32)).reshape(b, L, D_LRU) + b_a.astype(f32))
    i = jax.nn.sigmoid(jnp.einsum('blhi,hij->blhj', xb, w_x.astype(f32)).reshape(b, L, D_LRU) + b_x.astype(f32))
    log_a = LRU_C * r * jax.nn.log_sigmoid(lam.astype(f32))
    a = jnp.exp(log_a)
    mult = jnp.sqrt(-jnp.expm1(2.0 * log_a))
    mult = jnp.where(jnp.arange(L)[None, :, None] == 0, 1.0, mult)
    u = mult * (i * xc)

    def combine(left, right):
        a_l, u_l = left
        a_r, u_r = right
        return a_l * a_r, a_r * u_l + u_r

    _, hs = lax.associative_scan(combine, (a, u), axis=1)
    return (hs * jax.nn.gelu(gate.astype(f32))).astype(xr.dtype)


def mixer_block(h, w_in, fox_forget_bias, ssd_conv_w, ssd_conv_b, ssd_dt_bias, ssd_a_log, ssd_d, ssd_norm,
                lru_conv_w, lru_conv_b, lru_w_a, lru_b_a, lru_w_x, lru_b_x, lru_lambda,
                w_branch_attn, w_branch_ssd, w_branch_lru, w_out):
    b, L, _ = h.shape
    proj = h @ w_in
    q, k, v, f_logit, z, xbc, dt, xr, gate_r, merge = jnp.split(proj, _in_split_points(), axis=-1)
    log_f = jax.nn.log_sigmoid(f_logit.astype(jnp.float32) + fox_forget_bias.astype(jnp.float32))
    split_heads = lambda t: t.reshape(b, L, ATTN_HEADS, ATTN_HEAD_DIM)
    y_a = forgetting_attention(split_heads(q), split_heads(k), split_heads(v), log_f).reshape(b, L, D_ATTN)
    y_b = mamba2_branch(z, xbc, dt, ssd_conv_w, ssd_conv_b, ssd_dt_bias, ssd_a_log, ssd_d, ssd_norm)
    y_c = rglru_branch(xr, gate_r, lru_conv_w, lru_conv_b, lru_w_a, lru_b_a, lru_w_x, lru_b_x, lru_lambda)
    g_a, g_b, g_c = jnp.split(jax.nn.sigmoid(merge), N_BRANCH, axis=-1)
    mixed = g_a * (y_a @ w_branch_attn) + g_b * (y_b @ w_branch_ssd) + g_c * (y_c @ w_branch_lru)
    return mixed @ w_out


def setup_inputs(seed: int = 0) -> dict:
    key = jax.random.key(seed)
    keys = iter(jax.random.split(key, 40))

    def normal(shape, scale):
        return jax.random.normal(next(keys), shape, jnp.float32) * scale

    def uniform(shape, lo, hi):
        return jax.random.uniform(next(keys), shape, jnp.float32, lo, hi)

    def gain(shape):
        return 1.0 + normal(shape, 0.02)

    x = normal((BATCH, SEQ, D_MODEL), 1.0)
    meta_tokens = normal((N_META, D_MODEL), 1.0)
    ffn1_norm = gain((DEPTH, D_MODEL))
    ffn1_w_gate_up = normal((DEPTH, D_MODEL, 2 * D_FF), D_MODEL ** -0.5)
    ffn1_w_down = normal((DEPTH, D_FF, D_MODEL), D_FF ** -0.5)
    mix_norm = gain((DEPTH, D_MODEL))
    w_in = normal((DEPTH, D_MODEL, N_IN), D_MODEL ** -0.5)
    fox_forget_bias = FORGET_BIAS_INIT + normal((DEPTH, ATTN_HEADS), 0.5)
    ssd_conv_w = normal((DEPTH, SSD_CONV, D_XBC), SSD_CONV ** -0.5)
    ssd_conv_b = normal((DEPTH, D_XBC), 0.02)
    dt0 = jnp.exp(uniform((DEPTH, SSD_HEADS), math.log(1e-3), math.log(1e-1)))
    ssd_dt_bias = dt0 + jnp.log(-jnp.expm1(-dt0))
    ssd_a_log = jnp.log(uniform((DEPTH, SSD_HEADS), 1.0, 16.0))
    ssd_d = gain((DEPTH, SSD_HEADS))
    ssd_norm = gain((DEPTH, D_SSD))
    lru_conv_w = normal((DEPTH, LRU_CONV, D_LRU), LRU_CONV ** -0.5)
    lru_conv_b = normal((DEPTH, D_LRU), 0.02)
    lru_w_a = normal((DEPTH, LRU_BLOCKS, LRU_BLOCK_DIM, LRU_BLOCK_DIM), LRU_BLOCK_DIM ** -0.5)
    lru_b_a = normal((DEPTH, D_LRU), 0.02)
    lru_w_x = normal((DEPTH, LRU_BLOCKS, LRU_BLOCK_DIM, LRU_BLOCK_DIM), LRU_BLOCK_DIM ** -0.5)
    lru_b_x = normal((DEPTH, D_LRU), 0.02)
    a_pow_c = uniform((DEPTH, D_LRU), 0.9, 0.999)
    a0 = a_pow_c ** (1.0 / LRU_C)
    lru_lambda = jnp.log(a0) - jnp.log1p(-a0)
    w_branch_attn = normal((DEPTH, D_ATTN, D_MODEL), D_ATTN ** -0.5)
    w_branch_ssd = normal((DEPTH, D_SSD, D_MODEL), D_SSD ** -0.5)
    w_branch_lru = normal((DEPTH, D_LRU, D_MODEL), D_LRU ** -0.5)
    w_out = normal((DEPTH, D_MODEL, D_MODEL), D_MODEL ** -0.5)
    ffn2_norm = gain((DEPTH, D_MODEL))
    ffn2_w_gate_up = normal((DEPTH, D_MODEL, 2 * D_FF), D_MODEL ** -0.5)
    ffn2_w_down = normal((DEPTH, D_FF, D_MODEL), D_FF ** -0.5)
    final_norm = gain((D_MODEL,))
    return {'x': x, 'meta_tokens': meta_tokens,
            'ffn1_norm': ffn1_norm, 'ffn1_w_gate_up': ffn1_w_gate_up, 'ffn1_w_down': ffn1_w_down,
            'mix_norm': mix_norm, 'w_in': w_in, 'fox_forget_bias': fox_forget_bias,
            'ssd_conv_w': ssd_conv_w, 'ssd_conv_b': ssd_conv_b, 'ssd_dt_bias': ssd_dt_bias,
            'ssd_a_log': ssd_a_log, 'ssd_d': ssd_d, 'ssd_norm': ssd_norm,
            'lru_conv_w': lru_conv_w, 'lru_conv_b': lru_conv_b, 'lru_w_a': lru_w_a, 'lru_b_a': lru_b_a,
            'lru_w_x': lru_w_x, 'lru_b_x': lru_b_x, 'lru_lambda': lru_lambda,
            'w_branch_attn': w_branch_attn, 'w_branch_ssd': w_branch_ssd, 'w_branch_lru': w_branch_lru,
            'w_out': w_out,
            'ffn2_norm': ffn2_norm, 'ffn2_w_gate_up': ffn2_w_gate_up, 'ffn2_w_down': ffn2_w_down,
            'final_norm': final_norm}


def reference(x, meta_tokens, ffn1_norm, ffn1_w_gate_up, ffn1_w_down, mix_norm, w_in, fox_forget_bias,
              ssd_conv_w, ssd_conv_b, ssd_dt_bias, ssd_a_log, ssd_d, ssd_norm,
              lru_conv_w, lru_conv_b, lru_w_a, lru_b_a, lru_w_x, lru_b_x, lru_lambda,
              w_branch_attn, w_branch_ssd, w_branch_lru, w_out,
              ffn2_norm, ffn2_w_gate_up, ffn2_w_down, final_norm):
    b, s, d = x.shape
    length = N_META + s
    padded = -(-length // Q_BLOCK) * Q_BLOCK
    meta = jnp.broadcast_to(meta_tokens.astype(x.dtype)[None], (b, N_META, d))
    h = jnp.concatenate([meta, x, jnp.zeros((b, padded - length, d), x.dtype)], axis=1)
    for l in range(DEPTH):
        h = h + 0.5 * swiglu_ffn(rms_norm(h, ffn1_norm[l]), ffn1_w_gate_up[l], ffn1_w_down[l])
        h = h + mixer_block(rms_norm(h, mix_norm[l]), w_in[l], fox_forget_bias[l],
                            ssd_conv_w[l], ssd_conv_b[l], ssd_dt_bias[l], ssd_a_log[l], ssd_d[l], ssd_norm[l],
                            lru_conv_w[l], lru_conv_b[l], lru_w_a[l], lru_b_a[l], lru_w_x[l], lru_b_x[l],
                            lru_lambda[l], w_branch_attn[l], w_branch_ssd[l], w_branch_lru[l], w_out[l])
        h = h + 0.5 * swiglu_ffn(rms_norm(h, ffn2_norm[l]), ffn2_w_gate_up[l], ffn2_w_down[l])
    return rms_norm(h, final_norm)[:, N_META:N_META + s]
```

```python
import functools

import jax
import jax.numpy as jnp
from jax import lax
from jax.experimental import pallas as pl
from jax.experimental.pallas import tpu as pltpu

F32 = jnp.float32
BF16 = jnp.bfloat16
HIGHEST = lax.Precision.HIGHEST

D_MODEL = 1024
N_META = 16
Q_BLOCK = 128
SSD_CHUNK = 128
NORM_EPS = 1e-6
ATTN_HEADS = 16
ATTN_HEAD_DIM = 64
D_ATTN = ATTN_HEADS * ATTN_HEAD_DIM
SSD_HEAD_DIM = 64
D_SSD = D_MODEL
SSD_HEADS = D_SSD // SSD_HEAD_DIM
SSD_GROUPS = 2
SSD_STATE = 128
SSD_CONV = 4
D_XBC = D_SSD + 2 * SSD_GROUPS * SSD_STATE
D_LRU = D_MODEL
LRU_BLOCKS = 16
LRU_BLOCK_DIM = D_LRU // LRU_BLOCKS
LRU_CONV = 4
LRU_C = 8.0
D_FF = 2816
N_BRANCH = 3

LANES = 128
SUBLANES = 8
VMEM_LIMIT_BYTES = 56 * 1024 * 1024

FFN_ROWS = 512
FFN_CHUNK = 256
PROJ_ROWS = 256
PROJ_CHUNK = 512
ATTN_TQ = 384
MERGE_ROWS = 512
SCAN_ROWS = 128
FDT_COLS = LANES
DT_LANE0 = ATTN_HEADS
NEG_BIG = -1e30


def _cparams(sem):
    return pltpu.CompilerParams(dimension_semantics=sem, vmem_limit_bytes=VMEM_LIMIT_BYTES)


def _const_spec(shape):
    nd = len(shape)
    return pl.BlockSpec(shape, lambda *_: (0,) * nd, pipeline_mode=pl.Buffered(1))


def _rms(x, g):
    ms = jnp.mean(x * x, axis=-1, keepdims=True)
    return (x * lax.rsqrt(ms + NORM_EPS)) * g


def _dot(a, b):
    return jnp.dot(a, b, preferred_element_type=F32)


def _dot_nt(a, b):
    return lax.dot_general(a, b, (((1,), (1,)), ((), ())), preferred_element_type=F32)


def _dot_exact(a, b):
    return lax.dot_general(a, b, (((1,), (0,)), ((), ())), precision=HIGHEST,
                           preferred_element_type=F32)


def _log_sigmoid(x):
    return -(jnp.maximum(-x, 0.0) + jnp.log1p(jnp.exp(-jnp.abs(x))))


def _softplus(x):
    return jnp.maximum(x, 0.0) + jnp.log1p(jnp.exp(-jnp.abs(x)))


def _sigmoid(x):
    return 1.0 / (1.0 + jnp.exp(-x))


def _ffn_body(x_ref, g_ref, wg_ref, wu_ref, wd_ref, fg_ref, o_ref, a_scr, *, final_norm):
    x = x_ref[...]
    hn = _rms(x, g_ref[...]).astype(BF16)
    for c0 in range(0, D_FF, FFN_CHUNK):
        gate = _dot(hn, wg_ref[:, c0:c0 + FFN_CHUNK])
        up = _dot(hn, wu_ref[:, c0:c0 + FFN_CHUNK])
        a_scr[:, c0:c0 + FFN_CHUNK] = ((gate * _sigmoid(gate)) * up).astype(BF16)
    y = x + 0.5 * _dot(a_scr[...], wd_ref[...])
    if final_norm:
        y = _rms(y, fg_ref[...])
    o_ref[...] = y


def _ffn(x, g, wg, wu, wd, fg, final_norm):
    t = x.shape[0]
    tm = FFN_ROWS
    return pl.pallas_call(
        functools.partial(_ffn_body, final_norm=final_norm),
        out_shape=jax.ShapeDtypeStruct((t, D_MODEL), F32),
        grid=(t // tm,),
        in_specs=[pl.BlockSpec((tm, D_MODEL), lambda i: (i, 0)),
                  _const_spec((1, D_MODEL)),
                  _const_spec((D_MODEL, D_FF)),
                  _const_spec((D_MODEL, D_FF)),
                  _const_spec((D_FF, D_MODEL)),
                  _const_spec((1, D_MODEL))],
        out_specs=pl.BlockSpec((tm, D_MODEL), lambda i: (i, 0)),
        scratch_shapes=[pltpu.VMEM((tm, D_FF), BF16)],
        compiler_params=_cparams(("parallel",)),
        name="ffn",
    )(x, g, wg, wu, wd, fg)


_PROJ_GROUPS = (
    ("q", D_ATTN, ATTN_HEAD_DIM ** -0.5, BF16),
    ("k", D_ATTN, 1.0, BF16),
    ("v", D_ATTN, 1.0, BF16),
    ("z", D_SSD, 1.0, F32),
    ("xbc", D_XBC, 1.0, F32),
    ("xr", D_LRU, 1.0, F32),
    ("gate", D_LRU, 1.0, F32),
    ("merge", N_BRANCH * D_MODEL, 1.0, F32),
    ("fdt", FDT_COLS, 1.0, F32),
)
N_PROJ = sum(w for _, w, _, _ in _PROJ_GROUPS)


def _inproj_body(x_ref, g_ref, w_ref, *out_refs):
    hn = _rms(x_ref[...], g_ref[...]).astype(BF16)
    off = 0
    for (_, width, scale, dtype), o_ref in zip(_PROJ_GROUPS, out_refs):
        ck = min(PROJ_CHUNK, width)
        for c0 in range(0, width, ck):
            acc = _dot(hn, w_ref[:, off + c0:off + c0 + ck])
            if scale != 1.0:
                acc = acc * scale
            o_ref[:, c0:c0 + ck] = acc.astype(dtype)
        off += width


def _inproj(x, g, w):
    t = x.shape[0]
    tm = PROJ_ROWS
    return pl.pallas_call(
        _inproj_body,
        out_shape=[jax.ShapeDtypeStruct((t, width), dtype) for _, width, _, dtype in _PROJ_GROUPS],
        grid=(t // tm,),
        in_specs=[pl.BlockSpec((tm, D_MODEL), lambda i: (i, 0)),
                  _const_spec((1, D_MODEL)),
                  _const_spec((D_MODEL, N_PROJ))],
        out_specs=[pl.BlockSpec((tm, width), lambda i: (i, 0)) for _, width, _, _ in _PROJ_GROUPS],
        compiler_params=_cparams(("parallel",)),
        name="inproj",
    )(x, g, w)


def _foxc_body(fdt_ref, fb_ref, ccol_ref, ct_ref, *, n_chunks):
    row = lax.broadcasted_iota(jnp.int32, (LANES, LANES), 0)
    col = lax.broadcasted_iota(jnp.int32, (LANES, LANES), 1)
    tri = (row >= col).astype(F32)
    head_lane = col < ATTN_HEADS
    carry = jnp.zeros((1, LANES), F32)
    for j in range(n_chunks):
        r0 = j * LANES
        f = fdt_ref[0, r0:r0 + LANES, :]
        lf = jnp.where(head_lane, _log_sigmoid(f + fb_ref[...]), 0.0)
        c = _dot_exact(tri, lf) + carry
        ccol_ref[0, r0:r0 + LANES, :] = c
        ct_ref[0, :, r0:r0 + LANES] = c.T[:ATTN_HEADS, :]
        carry = c[LANES - 1:LANES, :]


def _foxc(fdt, fb, b, lp):
    n_chunks = lp // LANES
    return pl.pallas_call(
        functools.partial(_foxc_body, n_chunks=n_chunks),
        out_shape=[jax.ShapeDtypeStruct((b, lp, LANES), F32),
                   jax.ShapeDtypeStruct((b, ATTN_HEADS, lp), F32)],
        grid=(b,),
        in_specs=[pl.BlockSpec((1, lp, LANES), lambda i: (i, 0, 0)),
                  pl.BlockSpec((1, LANES), lambda i: (0, 0))],
        out_specs=[pl.BlockSpec((1, lp, LANES), lambda i: (i, 0, 0)),
                   pl.BlockSpec((1, ATTN_HEADS, lp), lambda i: (i, 0, 0))],
        compiler_params=_cparams(("parallel",)),
        name="fox_cumsum",
    )(fdt.reshape(b, lp, LANES), fb)


def _attn_body(q_ref, k_ref, v_ref, cc_ref, ct_ref, o_ref, *, tq):
    iq = pl.program_id(2)
    q2 = q_ref[0]
    lane = lax.broadcasted_iota(jnp.int32, (tq, LANES), 1)
    first = lane < ATTN_HEAD_DIM
    zero = jnp.zeros_like(q2)
    qh = (jnp.where(first, q2, zero), jnp.where(first, zero, q2))
    cc = cc_ref[0, 0]
    ccol = (cc[:, 0:1], cc[:, 1:2])
    rowi = lax.broadcasted_iota(jnp.int32, (tq, tq), 0)
    coli = lax.broadcasted_iota(jnp.int32, (tq, tq), 1)
    visible = rowi >= coli

    def step(j, carry, masked):
        ks = pl.multiple_of(j * tq, tq)
        k2 = k_ref[0, pl.ds(ks, tq), :]
        v2 = v_ref[0, pl.ds(ks, tq), :]
        out = []
        for h in range(2):
            m, l, acc = carry[h]
            s = _dot_nt(qh[h], k2)
            crow = ct_ref[0, 0, h:h + 1, pl.ds(ks, tq)]
            s = s + (ccol[h] - crow)
            if masked:
                s = jnp.where(visible, s, NEG_BIG)
            m_new = jnp.maximum(m, jnp.max(s, axis=-1, keepdims=True))
            alpha = jnp.exp(m - m_new)
            p = jnp.exp(s - m_new)
            l = alpha * l + jnp.sum(p, axis=-1, keepdims=True)
            acc = alpha * acc + _dot(p.astype(BF16), v2)
            out.append((m_new, l, acc))
        return tuple(out)

    init = tuple((jnp.full((tq, 1), NEG_BIG, F32), jnp.zeros((tq, 1), F32),
                  jnp.zeros((tq, LANES), F32)) for _ in range(2))
    carry = lax.fori_loop(0, iq, lambda j, c: step(j, c, False), init)
    (_, l0, acc0), (_, l1, acc1) = step(iq, carry, True)
    o_ref[0] = jnp.where(first, acc0 / l0, acc1 / l1).astype(o_ref.dtype)


def _attention(q, k, v, ccol, ct, b, lp):
    tq = ATTN_TQ
    hp = ATTN_HEADS // 2
    return pl.pallas_call(
        functools.partial(_attn_body, tq=tq),
        out_shape=jax.ShapeDtypeStruct((b, lp, D_ATTN), BF16),
        grid=(b, hp, lp // tq),
        in_specs=[pl.BlockSpec((1, tq, LANES), lambda bi, hi, qi: (bi, qi, hi)),
                  pl.BlockSpec((1, lp, LANES), lambda bi, hi, qi: (bi, 0, hi)),
                  pl.BlockSpec((1, lp, LANES), lambda bi, hi, qi: (bi, 0, hi)),
                  pl.BlockSpec((1, 1, tq, 2), lambda bi, hi, qi: (bi, hi, qi, 0)),
                  pl.BlockSpec((1, 1, 2, lp), lambda bi, hi, qi: (bi, hi, 0, 0))],
        out_specs=pl.BlockSpec((1, tq, LANES), lambda bi, hi, qi: (bi, qi, hi)),
        compiler_params=_cparams(("parallel", "parallel", "arbitrary")),
        name="fox_attention",
    )(q.reshape(b, lp, D_ATTN), k.reshape(b, lp, D_ATTN), v.reshape(b, lp, D_ATTN), ccol, ct)


def _causal_conv(x, xp_scr, w_ref, b_ref, first_tile, rows, taps):
    @pl.when(first_tile)
    def _():
        xp_scr[0:SUBLANES, :] = jnp.zeros((SUBLANES, x.shape[1]), F32)

    xp_scr[SUBLANES:SUBLANES + rows, :] = x
    y = b_ref[...] + w_ref[taps - 1:taps, :] * x
    for kk in range(taps - 1):
        r0 = SUBLANES - (taps - 1) + kk
        y = y + w_ref[kk:kk + 1, :] * xp_scr[r0:r0 + rows, :]
    xp_scr[0:SUBLANES, :] = x[rows - SUBLANES:rows, :]
    return y


def _ssd_body(xbc_ref, z_ref, fdt_ref, cw_ref, cb_ref, dtb_ref, alog_ref, dfull_ref, nw_ref,
              exp_ref, o_ref, xp_scr, st_scr):
    q = SSD_CHUNK
    first_tile = pl.program_id(1) == 0

    @pl.when(first_tile)
    def _():
        st_scr[...] = jnp.zeros(st_scr.shape, F32)

    y = _causal_conv(xbc_ref[0], xp_scr, cw_ref, cb_ref, first_tile, q, SSD_CONV)
    xc = y * _sigmoid(y)
    gs = D_SSD // SSD_GROUPS
    heads_per_group = SSD_HEADS // SSD_GROUPS

    lane = lax.broadcasted_iota(jnp.int32, (q, LANES), 1)
    dt_lane = (lane >= DT_LANE0) & (lane < DT_LANE0 + SSD_HEADS)
    dt = jnp.where(dt_lane, _softplus(fdt_ref[0] + dtb_ref[...]), 0.0)
    a = -jnp.exp(alog_ref[...])
    da = dt * a
    row = lax.broadcasted_iota(jnp.int32, (q, q), 0)
    col = lax.broadcasted_iota(jnp.int32, (q, q), 1)
    lower = row >= col
    a_cum = _dot_exact(lower.astype(F32), da)
    a_cum_t = a_cum.T
    expand = exp_ref[...]
    dt_full = _dot_exact(dt, expand)
    a_cum_full = _dot_exact(a_cum, expand)
    a_last_full = a_cum_full[q - 1:q, :]
    decay_to_end = jnp.exp(a_last_full - a_cum_full)
    decay_from_start = jnp.exp(a_cum_full)
    chunk_decay = jnp.exp(a_last_full)

    xs = xc[:, :D_SSD]
    xdt = xs * dt_full
    xdt_b = xdt.astype(BF16)
    xde_b = (xdt * decay_to_end).astype(BF16)
    half = lax.broadcasted_iota(jnp.int32, (q, LANES), 1) < SSD_HEAD_DIM

    for g in range(SSD_GROUPS):
        bm = xc[:, D_SSD + g * SSD_STATE:D_SSD + (g + 1) * SSD_STATE]
        cm = xc[:, D_SSD + SSD_GROUPS * SSD_STATE + g * SSD_STATE:
                D_SSD + SSD_GROUPS * SSD_STATE + (g + 1) * SSD_STATE]
        bm_b = bm.astype(BF16)
        cm_b = cm.astype(BF16)
        cb = _dot_nt(cm_b, bm_b)
        y_pairs = []
        for pair in range(heads_per_group // 2):
            c0 = g * gs + pair * LANES
            xp = xdt_b[:, c0:c0 + LANES]
            ys = []
            for e in range(2):
                hl = DT_LANE0 + g * heads_per_group + 2 * pair + e
                seg = a_cum[:, hl:hl + 1] - a_cum_t[hl:hl + 1, :]
                dec = jnp.exp(jnp.where(lower, seg, -jnp.inf))
                ys.append(_dot((cb * dec).astype(BF16), xp))
            y_pairs.append(jnp.where(half, ys[0], ys[1]))
        y_diag = jnp.concatenate(y_pairs, axis=1)
        sl = slice(g * gs, (g + 1) * gs)
        prev = st_scr[g]
        y_off = _dot(cm_b, prev.astype(BF16)) * decay_from_start[:, sl]
        st_scr[g] = prev * chunk_decay[:, sl] + _dot(bm.T.astype(BF16), xde_b[:, sl])
        yg = y_diag + y_off + dfull_ref[:, sl] * xs[:, sl]
        zg = z_ref[0, :, sl]
        yg = yg * (zg * _sigmoid(zg))
        yg = yg * lax.rsqrt(jnp.mean(yg * yg, axis=-1, keepdims=True) + NORM_EPS)
        o_ref[0, :, sl] = (yg * nw_ref[:, sl]).astype(o_ref.dtype)


def _ssd(xbc, z, fdt, cw, cb, dtb, alog, dfull, nw, expand, b, lp):
    q = SSD_CHUNK
    return pl.pallas_call(
        _ssd_body,
        out_shape=jax.ShapeDtypeStruct((b, lp, D_SSD), BF16),
        grid=(b, lp // q),
        in_specs=[pl.BlockSpec((1, q, D_XBC), lambda bi, ci: (bi, ci, 0)),
                  pl.BlockSpec((1, q, D_SSD), lambda bi, ci: (bi, ci, 0)),
                  pl.BlockSpec((1, q, FDT_COLS), lambda bi, ci: (bi, ci, 0)),
                  _const_spec((SSD_CONV, D_XBC)),
                  _const_spec((1, D_XBC)),
                  _const_spec((1, LANES)),
                  _const_spec((1, LANES)),
                  _const_spec((1, D_SSD)),
                  _const_spec((1, D_SSD)),
                  _const_spec((LANES, D_SSD))],
        out_specs=pl.BlockSpec((1, q, D_SSD), lambda bi, ci: (bi, ci, 0)),
        scratch_shapes=[pltpu.VMEM((SUBLANES + q, D_XBC), F32),
                        pltpu.VMEM((SSD_GROUPS, SSD_STATE, D_SSD // SSD_GROUPS), F32)],
        compiler_params=_cparams(("parallel", "arbitrary")),
        name="ssd",
    )(xbc.reshape(b, lp, D_XBC), z.reshape(b, lp, D_SSD), fdt.reshape(b, lp, FDT_COLS),
      cw, cb, dtb, alog, dfull, nw, expand)


def _lru_body(xr_ref, gate_ref, cw_ref, cb_ref, w2_ref, ba_ref, bx_ref, lam_ref, o_ref,
              xp_scr, h_scr):
    rows = SCAN_ROWS
    first_tile = pl.program_id(1) == 0

    @pl.when(first_tile)
    def _():
        h_scr[...] = jnp.zeros(h_scr.shape, F32)

    xc = _causal_conv(xr_ref[0], xp_scr, cw_ref, cb_ref, first_tile, rows, LRU_CONV)
    xc_b = xc.astype(BF16)
    pre = [_dot(xc_b[:, j * LANES:(j + 1) * LANES], w2_ref[j]) for j in range(D_LRU // LANES)]
    pre_a = jnp.concatenate([p[:, :LANES] for p in pre], axis=1)
    pre_x = jnp.concatenate([p[:, LANES:] for p in pre], axis=1)
    r = _sigmoid(pre_a + ba_ref[...])
    i = _sigmoid(pre_x + bx_ref[...])
    log_a = LRU_C * r * _log_sigmoid(lam_ref[...])
    a = jnp.exp(log_a)
    mult = jnp.sqrt(-jnp.tanh(log_a) * (a * a + 1.0))
    rowi = lax.broadcasted_iota(jnp.int32, (rows, D_LRU), 0)
    mult = jnp.where(first_tile & (rowi == 0), 1.0, mult)
    u = mult * (i * xc)

    d = 1
    while d < rows:
        keep = rowi >= d
        a_s = jnp.where(keep, pltpu.roll(a, d, 0), 1.0)
        u_s = jnp.where(keep, pltpu.roll(u, d, 0), 0.0)
        u = a * u_s + u
        a = a * a_s
        d *= 2
    h = a * h_scr[0:1, :] + u
    h_scr[0:1, :] = h[rows - 1:rows, :]
    o_ref[0] = (h * jax.nn.gelu(gate_ref[0])).astype(o_ref.dtype)


def _lru(xr, gate, cw, cb, w2, ba, bx, lam, b, lp):
    rows = SCAN_ROWS
    return pl.pallas_call(
        _lru_body,
        out_shape=jax.ShapeDtypeStruct((b, lp, D_LRU), BF16),
        grid=(b, lp // rows),
        in_specs=[pl.BlockSpec((1, rows, D_LRU), lambda bi, ti: (bi, ti, 0)),
                  pl.BlockSpec((1, rows, D_LRU), lambda bi, ti: (bi, ti, 0)),
                  _const_spec((LRU_CONV, D_LRU)),
                  _const_spec((1, D_LRU)),
                  _const_spec((D_LRU // LANES, LANES, 2 * LANES)),
                  _const_spec((1, D_LRU)),
                  _const_spec((1, D_LRU)),
                  _const_spec((1, D_LRU))],
        out_specs=pl.BlockSpec((1, rows, D_LRU), lambda bi, ti: (bi, ti, 0)),
        scratch_shapes=[pltpu.VMEM((SUBLANES + rows, D_LRU), F32),
                        pltpu.VMEM((SUBLANES, D_LRU), F32)],
        compiler_params=_cparams(("parallel", "arbitrary")),
        name="rglru",
    )(xr.reshape(b, lp, D_LRU), gate.reshape(b, lp, D_LRU), cw, cb, w2, ba, bx, lam)


def _merge_body(h_ref, ya_ref, yb_ref, yc_ref, m_ref, wa_ref, wb_ref, wc_ref, wo_ref, o_ref):
    mixed = _sigmoid(m_ref[:, 0:D_MODEL]) * _dot(ya_ref[...], wa_ref[...])
    mixed = mixed + _sigmoid(m_ref[:, D_MODEL:2 * D_MODEL]) * _dot(yb_ref[...], wb_ref[...])
    mixed = mixed + _sigmoid(m_ref[:, 2 * D_MODEL:3 * D_MODEL]) * _dot(yc_ref[...], wc_ref[...])
    o_ref[...] = h_ref[...] + _dot(mixed.astype(BF16), wo_ref[...])


def _merge(h, ya, yb, yc, m, wa, wb, wc, wo):
    t = h.shape[0]
    tm = MERGE_ROWS
    row_spec = lambda width: pl.BlockSpec((tm, width), lambda i: (i, 0))
    return pl.pallas_call(
        _merge_body,
        out_shape=jax.ShapeDtypeStruct((t, D_MODEL), F32),
        grid=(t // tm,),
        in_specs=[row_spec(D_MODEL), row_spec(D_ATTN), row_spec(D_SSD), row_spec(D_LRU),
                  row_spec(N_BRANCH * D_MODEL),
                  _const_spec((D_ATTN, D_MODEL)), _const_spec((D_SSD, D_MODEL)),
                  _const_spec((D_LRU, D_MODEL)), _const_spec((D_MODEL, D_MODEL))],
        out_specs=row_spec(D_MODEL),
        compiler_params=_cparams(("parallel",)),
        name="merge_out",
    )(h, ya, yb, yc, m, wa, wb, wc, wo)


def _prep_w_in(w_in):
    sizes = (D_ATTN, D_ATTN, D_ATTN, ATTN_HEADS, D_SSD, D_XBC, SSD_HEADS, D_LRU, D_LRU, N_BRANCH * D_MODEL)
    offs = [0]
    for s in sizes:
        offs.append(offs[-1] + s)
    part = lambda i: w_in[:, offs[i]:offs[i + 1]]
    q, k, v, f, z, xbc, dt, xr, gate, merge = (part(i) for i in range(10))
    pad = jnp.zeros((D_MODEL, FDT_COLS - ATTN_HEADS - SSD_HEADS), w_in.dtype)
    return jnp.concatenate([q, k, v, z, xbc, xr, gate, merge, f, dt, pad], axis=1).astype(BF16)


def _pad_lanes(vec, lane0):
    out = jnp.zeros((1, LANES), F32)
    return out.at[0, lane0:lane0 + vec.shape[0]].set(vec.astype(F32))


def _lru_gate_weights(w_a, w_x):
    def blockdiag_pairs(w):
        w = w.reshape(LRU_BLOCKS // 2, 2, LRU_BLOCK_DIM, LRU_BLOCK_DIM)
        zero = jnp.zeros_like(w[:, 0])
        top = jnp.concatenate([w[:, 0], zero], axis=2)
        bot = jnp.concatenate([zero, w[:, 1]], axis=2)
        return jnp.concatenate([top, bot], axis=1)
    return jnp.concatenate([blockdiag_pairs(w_a), blockdiag_pairs(w_x)], axis=2).astype(BF16)


def _head_expand():
    rows = jnp.arange(LANES)[:, None]
    cols = jnp.arange(D_SSD)[None, :]
    return (rows == DT_LANE0 + cols // SSD_HEAD_DIM).astype(F32)


def kernel(x, meta_tokens, ffn1_norm, ffn1_w_gate_up, ffn1_w_down, mix_norm, w_in, fox_forget_bias,
           ssd_conv_w, ssd_conv_b, ssd_dt_bias, ssd_a_log, ssd_d, ssd_norm,
           lru_conv_w, lru_conv_b, lru_w_a, lru_b_a, lru_w_x, lru_b_x, lru_lambda,
           w_branch_attn, w_branch_ssd, w_branch_lru, w_out,
           ffn2_norm, ffn2_w_gate_up, ffn2_w_down, final_norm):
    b, s, d = x.shape
    depth = w_in.shape[0]
    length = N_META + s
    lp = -(-length // Q_BLOCK) * Q_BLOCK
    t = b * lp
    assert d == D_MODEL and lp % ATTN_TQ == 0 and t % FFN_ROWS == 0

    meta = jnp.broadcast_to(meta_tokens.astype(x.dtype)[None], (b, N_META, d))
    h = jnp.concatenate([meta, x, jnp.zeros((b, lp - length, d), x.dtype)], axis=1).reshape(t, d)

    row = lambda vec: vec.astype(F32).reshape(1, -1)
    expand = _head_expand()
    fg = row(final_norm)
    for l in range(depth):
        h = _ffn(h, row(ffn1_norm[l]), ffn1_w_gate_up[l, :, :D_FF].astype(BF16),
                 ffn1_w_gate_up[l, :, D_FF:].astype(BF16), ffn1_w_down[l].astype(BF16), fg, False)

        q, k, v, z, xbc, xr, gate, merge, fdt = _inproj(h, row(mix_norm[l]), _prep_w_in(w_in[l]))
        ccol, ct = _foxc(fdt, _pad_lanes(fox_forget_bias[l], 0), b, lp)
        hp = ATTN_HEADS // 2
        ccol2 = ccol[:, :, :ATTN_HEADS].reshape(b, lp, hp, 2).transpose(0, 2, 1, 3)
        y_a = _attention(q, k, v, ccol2, ct.reshape(b, hp, 2, lp), b, lp).reshape(t, D_ATTN)
        y_b = _ssd(xbc, z, fdt, ssd_conv_w[l].astype(F32), row(ssd_conv_b[l]),
                   _pad_lanes(ssd_dt_bias[l], DT_LANE0), _pad_lanes(ssd_a_log[l], DT_LANE0),
                   row(jnp.repeat(ssd_d[l], SSD_HEAD_DIM)), row(ssd_norm[l]), expand, b, lp).reshape(t, D_SSD)
        y_c = _lru(xr, gate, lru_conv_w[l].astype(F32), row(lru_conv_b[l]),
                   _lru_gate_weights(lru_w_a[l], lru_w_x[l]), row(lru_b_a[l]), row(lru_b_x[l]),
                   row(lru_lambda[l]), b, lp).reshape(t, D_LRU)
        h = _merge(h, y_a, y_b, y_c, merge, w_branch_attn[l].astype(BF16), w_branch_ssd[l].astype(BF16),
                   w_branch_lru[l].astype(BF16), w_out[l].astype(BF16))

        h = _ffn(h, row(ffn2_norm[l]), ffn2_w_gate_up[l, :, :D_FF].astype(BF16),
                 ffn2_w_gate_up[l, :, D_FF:].astype(BF16), ffn2_w_down[l].astype(BF16), fg,
                 l == depth - 1)
    return h.reshape(b, lp, d)[:, N_META:N_META + s]
```

```python
import functools

import jax
import jax.numpy as jnp
from jax import lax
from jax.experimental import pallas as pl
from jax.experimental.pallas import tpu as pltpu

F32 = jnp.float32
BF16 = jnp.bfloat16
HIGHEST = lax.Precision.HIGHEST

D_MODEL = 1024
N_META = 16
SSD_CHUNK = 128
NORM_EPS = 1e-6
ATTN_HEADS = 16
ATTN_HEAD_DIM = 64
D_ATTN = ATTN_HEADS * ATTN_HEAD_DIM
SSD_HEAD_DIM = 64
D_SSD = D_MODEL
SSD_HEADS = D_SSD // SSD_HEAD_DIM
SSD_GROUPS = 2
SSD_STATE = 128
SSD_CONV = 4
D_XBC = D_SSD + 2 * SSD_GROUPS * SSD_STATE
D_LRU = D_MODEL
LRU_BLOCKS = 16
LRU_BLOCK_DIM = D_LRU // LRU_BLOCKS
LRU_CONV = 4
LRU_C = 8.0
D_FF = 2816
N_BRANCH = 3

LANES = 128
SUBLANES = 8
VMEM_LIMIT_BYTES = 56 * 1024 * 1024

SEQ_ALIGN = 256
FFN_ROWS = 512
FFN_CHUNK = 256
PROJ_ROWS = 256
PROJ_CHUNK = 512
ATTN_TQ = 256
ATTN_HPS = 8
MERGE_ROWS = 512
SCAN_ROWS = 128
FDT_COLS = LANES
DT_LANE0 = ATTN_HEADS
NEG_BIG = -1e30
LOG2E = 1.4426950408889634
Q_SCALE = ATTN_HEAD_DIM ** -0.5 * LOG2E
AUX_PARTS = 3


def _cparams(sem):
    return pltpu.CompilerParams(dimension_semantics=sem, vmem_limit_bytes=VMEM_LIMIT_BYTES)


def _const_spec(shape):
    nd = len(shape)
    return pl.BlockSpec(shape, lambda *_: (0,) * nd, pipeline_mode=pl.Buffered(1))


def _rms(x, g):
    ms = jnp.mean(x * x, axis=-1, keepdims=True)
    return (x * lax.rsqrt(ms + NORM_EPS)) * g


def _dot(a, b):
    return jnp.dot(a, b, preferred_element_type=F32)


def _dot_nt(a, b):
    return lax.dot_general(a, b, (((1,), (1,)), ((), ())), preferred_element_type=F32)


def _dot_exact(a, b):
    return lax.dot_general(a, b, (((1,), (0,)), ((), ())), precision=HIGHEST,
                           preferred_element_type=F32)


def _log_sigmoid(x):
    return -(jnp.maximum(-x, 0.0) + jnp.log1p(jnp.exp(-jnp.abs(x))))


def _softplus(x):
    return jnp.maximum(x, 0.0) + jnp.log1p(jnp.exp(-jnp.abs(x)))


def _sigmoid(x):
    return 1.0 / (1.0 + jnp.exp(-x))


def _ffn_body(x_ref, g_ref, wg_ref, wu_ref, wd_ref, fg_ref, o_ref, a_scr, *, final_norm):
    x = x_ref[...]
    hn = _rms(x, g_ref[...]).astype(BF16)
    for c0 in range(0, D_FF, FFN_CHUNK):
        gate = _dot(hn, wg_ref[:, c0:c0 + FFN_CHUNK])
        up = _dot(hn, wu_ref[:, c0:c0 + FFN_CHUNK])
        a_scr[:, c0:c0 + FFN_CHUNK] = ((gate * _sigmoid(gate)) * up).astype(BF16)
    y = x + 0.5 * _dot(a_scr[...], wd_ref[...])
    if final_norm:
        y = _rms(y, fg_ref[...])
    o_ref[...] = y


def _ffn(x, g, wg, wu, wd, fg, final_norm):
    t = x.shape[0]
    tm = FFN_ROWS
    return pl.pallas_call(
        functools.partial(_ffn_body, final_norm=final_norm),
        out_shape=jax.ShapeDtypeStruct((t, D_MODEL), F32),
        grid=(t // tm,),
        in_specs=[pl.BlockSpec((tm, D_MODEL), lambda i: (i, 0)),
                  _const_spec((1, D_MODEL)),
                  _const_spec((D_MODEL, D_FF)),
                  _const_spec((D_MODEL, D_FF)),
                  _const_spec((D_FF, D_MODEL)),
                  _const_spec((1, D_MODEL))],
        out_specs=pl.BlockSpec((tm, D_MODEL), lambda i: (i, 0)),
        scratch_shapes=[pltpu.VMEM((tm, D_FF), BF16)],
        compiler_params=_cparams(("parallel",)),
        name="ffn",
    )(x, g, wg, wu, wd, fg)


_PLAIN_GROUPS = (("z", D_SSD), ("xbc", D_XBC), ("xr", D_LRU), ("gate", D_LRU), ("merge", N_BRANCH * D_MODEL))
OFF_Q, OFF_K, OFF_V = 0, D_ATTN, 2 * D_ATTN
OFF_PLAIN = 3 * D_ATTN
OFF_FDT = OFF_PLAIN + sum(w for _, w in _PLAIN_GROUPS)
N_PROJ = OFF_FDT + FDT_COLS
D_AUG = ATTN_HEADS * LANES


def _inproj_body(x_ref, g_ref, w_ref, fb_ref, pa_ref, auxc_ref, qa_ref, ka_ref, va_ref,
                 z_ref, xbc_ref, xr_ref, gate_ref, merge_ref, fdt_ref, carry_scr):
    tm = x_ref.shape[1]

    @pl.when(pl.program_id(1) == 0)
    def _():
        carry_scr[...] = jnp.zeros(carry_scr.shape, F32)

    hn = _rms(x_ref[0], g_ref[...]).astype(BF16)

    def mm(c0, width):
        return _dot(hn, w_ref[:, c0:c0 + width])

    fdt = mm(OFF_FDT, FDT_COLS)
    fdt_ref[0] = fdt
    lane = lax.broadcasted_iota(jnp.int32, (tm, LANES), 1)
    lf = jnp.where(lane < ATTN_HEADS, _log_sigmoid(fdt + fb_ref[...]), 0.0)
    row = lax.broadcasted_iota(jnp.int32, (tm, tm), 0)
    col = lax.broadcasted_iota(jnp.int32, (tm, tm), 1)
    c = _dot_exact((row >= col).astype(F32), lf) + carry_scr[0:1, :]
    carry_scr[0:1, :] = c[tm - 1:tm, :]
    cs = c * LOG2E
    hi = cs.astype(BF16).astype(F32)
    r1 = cs - hi
    mid = r1.astype(BF16).astype(F32)
    lo = (r1 - mid).astype(BF16).astype(F32)
    cparts = (hi + pltpu.roll(mid, ATTN_HEADS, 1) + pltpu.roll(lo, 2 * ATTN_HEADS, 1)).astype(BF16)

    first = lane < ATTN_HEAD_DIM
    data_lane = (first, jnp.logical_not(first))
    for c0 in range(0, D_ATTN, PROJ_CHUNK):
        qv = mm(OFF_Q + c0, PROJ_CHUNK) * Q_SCALE
        kv = mm(OFF_K + c0, PROJ_CHUNK)
        vv = mm(OFF_V + c0, PROJ_CHUNK)
        for pr in range(PROJ_CHUNK // LANES):
            pair = c0 // LANES + pr
            g2 = _dot(cparts, pa_ref[:, pair * 2 * LANES:(pair + 1) * 2 * LANES])
            sl = slice(pr * LANES, (pr + 1) * LANES)
            for e in range(2):
                gs = g2[:, e * LANES:(e + 1) * LANES]
                q_aux = gs * auxc_ref[4 * e + 0:4 * e + 1, :] + auxc_ref[4 * e + 1:4 * e + 2, :]
                k_aux = gs * auxc_ref[4 * e + 2:4 * e + 3, :] + auxc_ref[4 * e + 3:4 * e + 4, :]
                hs = slice((2 * pair + e) * LANES, (2 * pair + e + 1) * LANES)
                qa_ref[0, :, hs] = jnp.where(data_lane[e], qv[:, sl], q_aux).astype(qa_ref.dtype)
                ka_ref[0, :, hs] = jnp.where(data_lane[e], kv[:, sl], k_aux).astype(ka_ref.dtype)
                va_ref[0, :, hs] = jnp.where(data_lane[e], vv[:, sl], 1.0).astype(va_ref.dtype)

    off = OFF_PLAIN
    for (_, width), o_ref in zip(_PLAIN_GROUPS, (z_ref, xbc_ref, xr_ref, gate_ref, merge_ref)):
        for c0 in range(0, width, PROJ_CHUNK):
            o_ref[0, :, c0:c0 + PROJ_CHUNK] = mm(off + c0, PROJ_CHUNK).astype(o_ref.dtype)
        off += width


def _inproj(x, g, w, fb, pa, auxc, b, lp):
    tm = PROJ_ROWS
    widths = [D_AUG, D_AUG, D_AUG] + [w_ for _, w_ in _PLAIN_GROUPS] + [FDT_COLS]
    dtypes = [BF16] * 8 + [F32]
    row_spec = lambda width: pl.BlockSpec((1, tm, width), lambda bi, ti: (bi, ti, 0))
    return pl.pallas_call(
        _inproj_body,
        out_shape=[jax.ShapeDtypeStruct((b, lp, width), dt) for width, dt in zip(widths, dtypes)],
        grid=(b, lp // tm),
        in_specs=[row_spec(D_MODEL),
                  _const_spec((1, D_MODEL)),
                  _const_spec((D_MODEL, N_PROJ)),
                  _const_spec((1, LANES)),
                  _const_spec((LANES, D_AUG)),
                  _const_spec((SUBLANES, LANES))],
        out_specs=[row_spec(width) for width in widths],
        scratch_shapes=[pltpu.VMEM((SUBLANES, LANES), F32)],
        compiler_params=_cparams(("parallel", "arbitrary")),
        name="inproj",
    )(x.reshape(b, lp, D_MODEL), g, w, fb, pa, auxc)


def _aux_constants():
    rows = jnp.arange(LANES)[:, None]
    cols = jnp.arange(D_AUG)[None, :]
    head = cols // LANES
    base = jnp.where(head % 2 == 0, ATTN_HEAD_DIM, 0)
    pos = cols % LANES - base
    part = rows // ATTN_HEADS
    src_head = rows % ATTN_HEADS
    valid = (part < AUX_PARTS) & (src_head == head)
    plus = valid & (pos == part)
    minus = valid & (pos == part + AUX_PARTS)
    pa = plus.astype(F32) - minus.astype(F32)
    lane = jnp.arange(LANES)
    rows_out = []
    for e in range(2):
        b0 = ATTN_HEAD_DIM if e == 0 else 0
        lo = ((lane >= b0) & (lane < b0 + AUX_PARTS)).astype(F32)
        hi = ((lane >= b0 + AUX_PARTS) & (lane < b0 + 2 * AUX_PARTS)).astype(F32)
        rows_out += [lo, hi, hi, lo]
    return pa.astype(BF16), jnp.stack(rows_out).astype(F32)


def _attn_body(qa_ref, ka_ref, va_ref, mb_ref, o_ref, m_scr, acc_scr, s_scr, *, tq, hps):
    iq = pl.program_id(2)
    m_scr[...] = jnp.full(m_scr.shape, NEG_BIG, F32)
    acc_scr[...] = jnp.zeros(acc_scr.shape, F32)
    nck = tq // LANES

    def qk(j, h):
        ks = pl.multiple_of(j * tq, tq)
        qa = qa_ref[0, :, h * LANES:(h + 1) * LANES]
        ka = ka_ref[0, pl.ds(ks, tq), h * LANES:(h + 1) * LANES]
        return _dot_nt(qa, ka)

    def softmax_pv(j, h, s):
        ks = pl.multiple_of(j * tq, tq)
        v2 = va_ref[0, pl.ds(ks, tq), h * LANES:(h + 1) * LANES]
        sc = [s[:, c * LANES:(c + 1) * LANES] for c in range(nck)]
        mx = sc[0]
        for c in range(1, nck):
            mx = jnp.maximum(mx, sc[c])
        m_prev = m_scr[h]
        m_new = jnp.maximum(m_prev, jnp.max(mx, axis=1, keepdims=True))
        alpha = jnp.exp2(m_prev - m_new)
        pc = [jnp.exp2(c_ - m_new) for c_ in sc]
        pv = _dot(jnp.concatenate(pc, axis=1).astype(va_ref.dtype), v2)
        acc_scr[h] = alpha * acc_scr[h] + pv
        m_scr[h] = m_new

    for h in range(hps):
        s_scr[h] = qk(0, h)

    def loop_body(j, carry):
        for h in range(hps):
            s_next = qk(j + 1, h)
            softmax_pv(j, h, s_scr[h])
            s_scr[h] = s_next
        return carry

    lax.fori_loop(0, iq, loop_body, 0)
    for h in range(hps):
        softmax_pv(iq, h, s_scr[h] + mb_ref[...])

    first = lax.broadcasted_iota(jnp.int32, (tq, LANES), 1) < ATTN_HEAD_DIM
    for hp in range(hps // 2):
        acc_e = acc_scr[2 * hp]
        acc_o = acc_scr[2 * hp + 1]
        num = jnp.where(first, acc_e, acc_o)
        den = pltpu.roll(jnp.where(first, acc_o, acc_e), ATTN_HEAD_DIM, 1)
        o_ref[0, :, hp * LANES:(hp + 1) * LANES] = (num / den).astype(o_ref.dtype)


def _attention(qa, ka, va, b, lp):
    tq, hps = ATTN_TQ, ATTN_HPS
    causal_bias = jnp.where(jnp.arange(tq)[:, None] >= jnp.arange(tq)[None, :], 0.0, NEG_BIG).astype(F32)
    kv_spec = pl.BlockSpec((1, lp, hps * LANES), lambda bi, hi, qi: (bi, 0, hi))
    return pl.pallas_call(
        functools.partial(_attn_body, tq=tq, hps=hps),
        out_shape=jax.ShapeDtypeStruct((b, lp, D_ATTN), BF16),
        grid=(b, ATTN_HEADS // hps, lp // tq),
        in_specs=[pl.BlockSpec((1, tq, hps * LANES), lambda bi, hi, qi: (bi, qi, hi)),
                  kv_spec, kv_spec,
                  pl.BlockSpec((tq, tq), lambda bi, hi, qi: (0, 0))],
        out_specs=pl.BlockSpec((1, tq, hps * ATTN_HEAD_DIM), lambda bi, hi, qi: (bi, qi, hi)),
        scratch_shapes=[pltpu.VMEM((hps, tq, LANES), F32),
                        pltpu.VMEM((hps, tq, LANES), F32),
                        pltpu.VMEM((hps, tq, tq), F32)],
        compiler_params=_cparams(("parallel", "parallel", "arbitrary")),
        name="fox_attention",
    )(qa, ka, va, causal_bias)


def _causal_conv(x, xp_scr, w_ref, b_ref, first_tile, rows, taps):
    @pl.when(first_tile)
    def _():
        xp_scr[0:SUBLANES, :] = jnp.zeros((SUBLANES, x.shape[1]), F32)

    xp_scr[SUBLANES:SUBLANES + rows, :] = x
    y = b_ref[...] + w_ref[taps - 1:taps, :] * x
    for kk in range(taps - 1):
        r0 = SUBLANES - (taps - 1) + kk
        y = y + w_ref[kk:kk + 1, :] * xp_scr[r0:r0 + rows, :]
    xp_scr[0:SUBLANES, :] = x[rows - SUBLANES:rows, :]
    return y


def _ssd_body(xbc_ref, z_ref, fdt_ref, cw_ref, cb_ref, dtb_ref, alog_ref, dfull_ref, nw_ref,
              exp_ref, o_ref, xp_scr, st_scr):
    q = SSD_CHUNK
    first_tile = pl.program_id(1) == 0

    @pl.when(first_tile)
    def _():
        st_scr[...] = jnp.zeros(st_scr.shape, F32)

    y = _causal_conv(xbc_ref[0].astype(F32), xp_scr, cw_ref, cb_ref, first_tile, q, SSD_CONV)
    xc = y * _sigmoid(y)
    gs = D_SSD // SSD_GROUPS
    heads_per_group = SSD_HEADS // SSD_GROUPS

    lane = lax.broadcasted_iota(jnp.int32, (q, LANES), 1)
    dt_lane = (lane >= DT_LANE0) & (lane < DT_LANE0 + SSD_HEADS)
    dt = jnp.where(dt_lane, _softplus(fdt_ref[0] + dtb_ref[...]), 0.0)
    a = -jnp.exp(alog_ref[...])
    da = dt * a
    row = lax.broadcasted_iota(jnp.int32, (q, q), 0)
    col = lax.broadcasted_iota(jnp.int32, (q, q), 1)
    lower = row >= col
    a_cum = _dot_exact(lower.astype(F32), da)
    a_cum_t = a_cum.T
    expand = exp_ref[...]
    dt_full = _dot_exact(dt, expand)
    a_cum_full = _dot_exact(a_cum, expand)
    a_last_full = a_cum_full[q - 1:q, :]
    decay_to_end = jnp.exp(a_last_full - a_cum_full)
    decay_from_start = jnp.exp(a_cum_full)
    chunk_decay = jnp.exp(a_last_full)

    xs = xc[:, :D_SSD]
    xdt = xs * dt_full
    xdt_b = xdt.astype(BF16)
    xde_b = (xdt * decay_to_end).astype(BF16)
    half = lax.broadcasted_iota(jnp.int32, (q, LANES), 1) < SSD_HEAD_DIM

    for g in range(SSD_GROUPS):
        bm = xc[:, D_SSD + g * SSD_STATE:D_SSD + (g + 1) * SSD_STATE]
        cm = xc[:, D_SSD + SSD_GROUPS * SSD_STATE + g * SSD_STATE:
                D_SSD + SSD_GROUPS * SSD_STATE + (g + 1) * SSD_STATE]
        bm_b = bm.astype(BF16)
        cm_b = cm.astype(BF16)
        cb = _dot_nt(cm_b, bm_b)
        y_pairs = []
        for pair in range(heads_per_group // 2):
            c0 = g * gs + pair * LANES
            xp = xdt_b[:, c0:c0 + LANES]
            ys = []
            for e in range(2):
                hl = DT_LANE0 + g * heads_per_group + 2 * pair + e
                seg = a_cum[:, hl:hl + 1] - a_cum_t[hl:hl + 1, :]
                dec = jnp.exp(jnp.where(lower, seg, -jnp.inf))
                ys.append(_dot((cb * dec).astype(BF16), xp))
            y_pairs.append(jnp.where(half, ys[0], ys[1]))
        y_diag = jnp.concatenate(y_pairs, axis=1)
        sl = slice(g * gs, (g + 1) * gs)
        prev = st_scr[g]
        y_off = _dot(cm_b, prev.astype(BF16)) * decay_from_start[:, sl]
        st_scr[g] = prev * chunk_decay[:, sl] + _dot(bm.T.astype(BF16), xde_b[:, sl])
        yg = y_diag + y_off + dfull_ref[:, sl] * xs[:, sl]
        zg = z_ref[0, :, sl].astype(F32)
        yg = yg * (zg * _sigmoid(zg))
        yg = yg * lax.rsqrt(jnp.mean(yg * yg, axis=-1, keepdims=True) + NORM_EPS)
        o_ref[0, :, sl] = (yg * nw_ref[:, sl]).astype(o_ref.dtype)


def _ssd(xbc, z, fdt, cw, cb, dtb, alog, dfull, nw, expand, b, lp):
    q = SSD_CHUNK
    return pl.pallas_call(
        _ssd_body,
        out_shape=jax.ShapeDtypeStruct((b, lp, D_SSD), BF16),
        grid=(b, lp // q),
        in_specs=[pl.BlockSpec((1, q, D_XBC), lambda bi, ci: (bi, ci, 0)),
                  pl.BlockSpec((1, q, D_SSD), lambda bi, ci: (bi, ci, 0)),
                  pl.BlockSpec((1, q, FDT_COLS), lambda bi, ci: (bi, ci, 0)),
                  _const_spec((SSD_CONV, D_XBC)),
                  _const_spec((1, D_XBC)),
                  _const_spec((1, LANES)),
                  _const_spec((1, LANES)),
                  _const_spec((1, D_SSD)),
                  _const_spec((1, D_SSD)),
                  _const_spec((LANES, D_SSD))],
        out_specs=pl.BlockSpec((1, q, D_SSD), lambda bi, ci: (bi, ci, 0)),
        scratch_shapes=[pltpu.VMEM((SUBLANES + q, D_XBC), F32),
                        pltpu.VMEM((SSD_GROUPS, SSD_STATE, D_SSD // SSD_GROUPS), F32)],
        compiler_params=_cparams(("parallel", "arbitrary")),
        name="ssd",
    )(xbc, z, fdt, cw, cb, dtb, alog, dfull, nw, expand)


def _lru_body(xr_ref, gate_ref, cw_ref, cb_ref, w2_ref, ba_ref, bx_ref, lam_ref, o_ref,
              xp_scr, h_scr):
    rows = SCAN_ROWS
    first_tile = pl.program_id(1) == 0

    @pl.when(first_tile)
    def _():
        h_scr[...] = jnp.zeros(h_scr.shape, F32)

    xc = _causal_conv(xr_ref[0].astype(F32), xp_scr, cw_ref, cb_ref, first_tile, rows, LRU_CONV)
    xc_b = xc.astype(BF16)
    pre = [_dot(xc_b[:, j * LANES:(j + 1) * LANES], w2_ref[j]) for j in range(D_LRU // LANES)]
    pre_a = jnp.concatenate([p[:, :LANES] for p in pre], axis=1)
    pre_x = jnp.concatenate([p[:, LANES:] for p in pre], axis=1)
    r = _sigmoid(pre_a + ba_ref[...])
    i = _sigmoid(pre_x + bx_ref[...])
    log_a = LRU_C * r * _log_sigmoid(lam_ref[...])
    a = jnp.exp(log_a)
    mult = jnp.sqrt(-jnp.tanh(log_a) * (a * a + 1.0))
    rowi = lax.broadcasted_iota(jnp.int32, (rows, D_LRU), 0)
    mult = jnp.where(first_tile & (rowi == 0), 1.0, mult)
    u = mult * (i * xc)

    d = 1
    while d < rows:
        keep = rowi >= d
        a_s = jnp.where(keep, pltpu.roll(a, d, 0), 1.0)
        u_s = jnp.where(keep, pltpu.roll(u, d, 0), 0.0)
        u = a * u_s + u
        a = a * a_s
        d *= 2
    h = a * h_scr[0:1, :] + u
    h_scr[0:1, :] = h[rows - 1:rows, :]
    o_ref[0] = (h * jax.nn.gelu(gate_ref[0].astype(F32))).astype(o_ref.dtype)


def _lru(xr, gate, cw, cb, w2, ba, bx, lam, b, lp):
    rows = SCAN_ROWS
    return pl.pallas_call(
        _lru_body,
        out_shape=jax.ShapeDtypeStruct((b, lp, D_LRU), BF16),
        grid=(b, lp // rows),
        in_specs=[pl.BlockSpec((1, rows, D_LRU), lambda bi, ti: (bi, ti, 0)),
                  pl.BlockSpec((1, rows, D_LRU), lambda bi, ti: (bi, ti, 0)),
                  _const_spec((LRU_CONV, D_LRU)),
                  _const_spec((1, D_LRU)),
                  _const_spec((D_LRU // LANES, LANES, 2 * LANES)),
                  _const_spec((1, D_LRU)),
                  _const_spec((1, D_LRU)),
                  _const_spec((1, D_LRU))],
        out_specs=pl.BlockSpec((1, rows, D_LRU), lambda bi, ti: (bi, ti, 0)),
        scratch_shapes=[pltpu.VMEM((SUBLANES + rows, D_LRU), F32),
                        pltpu.VMEM((SUBLANES, D_LRU), F32)],
        compiler_params=_cparams(("parallel", "arbitrary")),
        name="rglru",
    )(xr, gate, cw, cb, w2, ba, bx, lam)


def _merge_body(h_ref, ya_ref, yb_ref, yc_ref, m_ref, wa_ref, wb_ref, wc_ref, wo_ref, o_ref):
    gate = lambda i: _sigmoid(m_ref[:, i * D_MODEL:(i + 1) * D_MODEL].astype(F32))
    mixed = gate(0) * _dot(ya_ref[...], wa_ref[...])
    mixed = mixed + gate(1) * _dot(yb_ref[...], wb_ref[...])
    mixed = mixed + gate(2) * _dot(yc_ref[...], wc_ref[...])
    o_ref[...] = h_ref[...] + _dot(mixed.astype(BF16), wo_ref[...])


def _merge(h, ya, yb, yc, m, wa, wb, wc, wo):
    t = h.shape[0]
    tm = MERGE_ROWS
    row_spec = lambda width: pl.BlockSpec((tm, width), lambda i: (i, 0))
    return pl.pallas_call(
        _merge_body,
        out_shape=jax.ShapeDtypeStruct((t, D_MODEL), F32),
        grid=(t // tm,),
        in_specs=[row_spec(D_MODEL), row_spec(D_ATTN), row_spec(D_SSD), row_spec(D_LRU),
                  row_spec(N_BRANCH * D_MODEL),
                  _const_spec((D_ATTN, D_MODEL)), _const_spec((D_SSD, D_MODEL)),
                  _const_spec((D_LRU, D_MODEL)), _const_spec((D_MODEL, D_MODEL))],
        out_specs=row_spec(D_MODEL),
        compiler_params=_cparams(("parallel",)),
        name="merge_out",
    )(h, ya, yb, yc, m, wa, wb, wc, wo)


def _prep_w_in(w_in):
    sizes = (D_ATTN, D_ATTN, D_ATTN, ATTN_HEADS, D_SSD, D_XBC, SSD_HEADS, D_LRU, D_LRU, N_BRANCH * D_MODEL)
    offs = [0]
    for s in sizes:
        offs.append(offs[-1] + s)
    part = lambda i: w_in[:, offs[i]:offs[i + 1]]
    q, k, v, f, z, xbc, dt, xr, gate, merge = (part(i) for i in range(10))
    pad = jnp.zeros((D_MODEL, FDT_COLS - ATTN_HEADS - SSD_HEADS), w_in.dtype)
    return jnp.concatenate([q, k, v, z, xbc, xr, gate, merge, f, dt, pad], axis=1).astype(BF16)


def _pad_lanes(vec, lane0):
    out = jnp.zeros((1, LANES), F32)
    return out.at[0, lane0:lane0 + vec.shape[0]].set(vec.astype(F32))


def _lru_gate_weights(w_a, w_x):
    def blockdiag_pairs(w):
        w = w.reshape(LRU_BLOCKS // 2, 2, LRU_BLOCK_DIM, LRU_BLOCK_DIM)
        zero = jnp.zeros_like(w[:, 0])
        top = jnp.concatenate([w[:, 0], zero], axis=2)
        bot = jnp.concatenate([zero, w[:, 1]], axis=2)
        return jnp.concatenate([top, bot], axis=1)
    return jnp.concatenate([blockdiag_pairs(w_a), blockdiag_pairs(w_x)], axis=2).astype(BF16)


def _head_expand():
    rows = jnp.arange(LANES)[:, None]
    cols = jnp.arange(D_SSD)[None, :]
    return (rows == DT_LANE0 + cols // SSD_HEAD_DIM).astype(F32)


def kernel(x, meta_tokens, ffn1_norm, ffn1_w_gate_up, ffn1_w_down, mix_norm, w_in, fox_forget_bias,
           ssd_conv_w, ssd_conv_b, ssd_dt_bias, ssd_a_log, ssd_d, ssd_norm,
           lru_conv_w, lru_conv_b, lru_w_a, lru_b_a, lru_w_x, lru_b_x, lru_lambda,
           w_branch_attn, w_branch_ssd, w_branch_lru, w_out,
           ffn2_norm, ffn2_w_gate_up, ffn2_w_down, final_norm):
    b, s, d = x.shape
    depth = w_in.shape[0]
    length = N_META + s
    lp = -(-length // SEQ_ALIGN) * SEQ_ALIGN
    t = b * lp
    assert d == D_MODEL and t % FFN_ROWS == 0 and t % MERGE_ROWS == 0

    meta = jnp.broadcast_to(meta_tokens.astype(x.dtype)[None], (b, N_META, d))
    h = jnp.concatenate([meta, x, jnp.zeros((b, lp - length, d), x.dtype)], axis=1).reshape(t, d)

    row = lambda vec: vec.astype(F32).reshape(1, -1)
    expand = _head_expand()
    pa, auxc = _aux_constants()
    fg = row(final_norm)
    for l in range(depth):
        h = _ffn(h, row(ffn1_norm[l]), ffn1_w_gate_up[l, :, :D_FF].astype(BF16),
                 ffn1_w_gate_up[l, :, D_FF:].astype(BF16), ffn1_w_down[l].astype(BF16), fg, False)

        qa, ka, va, z, xbc, xr, gate, merge, fdt = _inproj(
            h, row(mix_norm[l]), _prep_w_in(w_in[l]), _pad_lanes(fox_forget_bias[l], 0), pa, auxc, b, lp)
        y_a = _attention(qa, ka, va, b, lp).reshape(t, D_ATTN)
        y_b = _ssd(xbc, z, fdt, ssd_conv_w[l].astype(F32), row(ssd_conv_b[l]),
                   _pad_lanes(ssd_dt_bias[l], DT_LANE0), _pad_lanes(ssd_a_log[l], DT_LANE0),
                   row(jnp.repeat(ssd_d[l], SSD_HEAD_DIM)), row(ssd_norm[l]), expand, b, lp).reshape(t, D_SSD)
        y_c = _lru(xr, gate, lru_conv_w[l].astype(F32), row(lru_conv_b[l]),
                   _lru_gate_weights(lru_w_a[l], lru_w_x[l]), row(lru_b_a[l]), row(lru_b_x[l]),
                   row(lru_lambda[l]), b, lp).reshape(t, D_LRU)
        h = _merge(h, y_a, y_b, y_c, merge.reshape(t, N_BRANCH * D_MODEL), w_branch_attn[l].astype(BF16),
                   w_branch_ssd[l].astype(BF16), w_branch_lru[l].astype(BF16), w_out[l].astype(BF16))

        h = _ffn(h, row(ffn2_norm[l]), ffn2_w_gate_up[l, :, :D_FF].astype(BF16),
                 ffn2_w_gate_up[l, :, D_FF:].astype(BF16), ffn2_w_down[l].astype(BF16), fg,
                 l == depth - 1)
    return h.reshape(b, lp, d)[:, N_META:N_META + s]
```

```python
import functools

import jax
import jax.numpy as jnp
from jax import lax
from jax.experimental import pallas as pl
from jax.experimental.pallas import tpu as pltpu

F32 = jnp.float32
BF16 = jnp.bfloat16
HIGHEST = lax.Precision.HIGHEST

D_MODEL = 1024
N_META = 16
SSD_CHUNK = 128
NORM_EPS = 1e-6
ATTN_HEADS = 16
ATTN_HEAD_DIM = 64
D_ATTN = ATTN_HEADS * ATTN_HEAD_DIM
SSD_HEAD_DIM = 64
D_SSD = D_MODEL
SSD_HEADS = D_SSD // SSD_HEAD_DIM
SSD_GROUPS = 2
SSD_STATE = 128
SSD_CONV = 4
D_XBC = D_SSD + 2 * SSD_GROUPS * SSD_STATE
D_LRU = D_MODEL
LRU_BLOCKS = 16
LRU_BLOCK_DIM = D_LRU // LRU_BLOCKS
LRU_CONV = 4
LRU_C = 8.0
D_FF = 2816
N_BRANCH = 3

LANES = 128
SUBLANES = 8
VMEM_LIMIT_BYTES = 56 * 1024 * 1024

SEQ_ALIGN = 256
FFN_ROWS = 512
FFN_CHUNK = 256
PROJ_ROWS = 256
PROJ_CHUNK = 512
ATTN_TQ = 256
ATTN_HPS = 8
MERGE_ROWS = 512
SCAN_ROWS = 128
FDT_COLS = LANES
DT_LANE0 = ATTN_HEADS
NEG_BIG = -1e30
LOG2E = 1.4426950408889634
Q_SCALE = ATTN_HEAD_DIM ** -0.5 * LOG2E
AUX_PARTS = 3


def _cparams(sem):
    return pltpu.CompilerParams(dimension_semantics=sem, vmem_limit_bytes=VMEM_LIMIT_BYTES)


def _const_spec(shape):
    nd = len(shape)
    return pl.BlockSpec(shape, lambda *_: (0,) * nd, pipeline_mode=pl.Buffered(1))


def _rms(x, g):
    ms = jnp.mean(x * x, axis=-1, keepdims=True)
    return (x * lax.rsqrt(ms + NORM_EPS)) * g


def _dot(a, b):
    return jnp.dot(a, b, preferred_element_type=F32)


def _dot_nt(a, b):
    return lax.dot_general(a, b, (((1,), (1,)), ((), ())), preferred_element_type=F32)


def _dot_exact(a, b):
    return lax.dot_general(a, b, (((1,), (0,)), ((), ())), precision=HIGHEST,
                           preferred_element_type=F32)


def _log_sigmoid(x):
    return -(jnp.maximum(-x, 0.0) + jnp.log1p(jnp.exp(-jnp.abs(x))))


def _softplus(x):
    return jnp.maximum(x, 0.0) + jnp.log1p(jnp.exp(-jnp.abs(x)))


def _sigmoid(x):
    return 1.0 / (1.0 + jnp.exp(-x))


def _ffn_body(x_ref, g_ref, wg_ref, wu_ref, wd_ref, fg_ref, o_ref, a_scr, *, final_norm):
    x = x_ref[...]
    hn = _rms(x, g_ref[...]).astype(BF16)
    for c0 in range(0, D_FF, FFN_CHUNK):
        gate = _dot(hn, wg_ref[:, c0:c0 + FFN_CHUNK])
        up = _dot(hn, wu_ref[:, c0:c0 + FFN_CHUNK])
        a_scr[:, c0:c0 + FFN_CHUNK] = ((gate * _sigmoid(gate)) * up).astype(BF16)
    y = x + 0.5 * _dot(a_scr[...], wd_ref[...])
    if final_norm:
        y = _rms(y, fg_ref[...])
    o_ref[...] = y


def _ffn(x, g, wg, wu, wd, fg, final_norm):
    t = x.shape[0]
    tm = FFN_ROWS
    return pl.pallas_call(
        functools.partial(_ffn_body, final_norm=final_norm),
        out_shape=jax.ShapeDtypeStruct((t, D_MODEL), F32),
        grid=(t // tm,),
        in_specs=[pl.BlockSpec((tm, D_MODEL), lambda i: (i, 0)),
                  _const_spec((1, D_MODEL)),
                  _const_spec((D_MODEL, D_FF)),
                  _const_spec((D_MODEL, D_FF)),
                  _const_spec((D_FF, D_MODEL)),
                  _const_spec((1, D_MODEL))],
        out_specs=pl.BlockSpec((tm, D_MODEL), lambda i: (i, 0)),
        scratch_shapes=[pltpu.VMEM((tm, D_FF), BF16)],
        compiler_params=_cparams(("parallel",)),
        name="ffn",
    )(x, g, wg, wu, wd, fg)


_PLAIN_GROUPS = (("z", D_SSD), ("xbc", D_XBC), ("xr", D_LRU), ("gate", D_LRU), ("merge", N_BRANCH * D_MODEL))
OFF_K = 0
OFF_PLAIN = D_ATTN
OFF_FDT = OFF_PLAIN + sum(w for _, w in _PLAIN_GROUPS)
N_PROJ = OFF_FDT + FDT_COLS
D_AUG = ATTN_HEADS * LANES
ONE_LANE = LANES - 1


def _inproj_body(x_ref, g_ref, w_ref, wt_ref, fb_ref, pk_ref, pqt_ref, qt_ref, ka_ref, vt_ref,
                 z_ref, xbc_ref, xr_ref, gate_ref, merge_ref, fdt_ref, carry_scr):
    tm = x_ref.shape[1]

    @pl.when(pl.program_id(1) == 0)
    def _():
        carry_scr[...] = jnp.zeros(carry_scr.shape, F32)

    hn = _rms(x_ref[0], g_ref[...]).astype(BF16)

    def mm(c0, width):
        return _dot(hn, w_ref[:, c0:c0 + width])

    fdt = mm(OFF_FDT, FDT_COLS)
    fdt_ref[0] = fdt
    lane = lax.broadcasted_iota(jnp.int32, (tm, LANES), 1)
    lf = jnp.where(lane < ATTN_HEADS, _log_sigmoid(fdt + fb_ref[...]), 0.0)
    row = lax.broadcasted_iota(jnp.int32, (tm, tm), 0)
    col = lax.broadcasted_iota(jnp.int32, (tm, tm), 1)
    c = _dot_exact((row >= col).astype(F32), lf) + carry_scr[0:1, :]
    carry_scr[0:1, :] = c[tm - 1:tm, :]
    cs = c * LOG2E
    hi = cs.astype(BF16).astype(F32)
    r1 = cs - hi
    mid = r1.astype(BF16).astype(F32)
    lo = (r1 - mid).astype(BF16).astype(F32)
    cparts = (hi + pltpu.roll(mid, ATTN_HEADS, 1) + pltpu.roll(lo, 2 * ATTN_HEADS, 1)
              + jnp.where(lane == ONE_LANE, 1.0, 0.0))

    cparts_b = cparts.astype(BF16)
    first = lane < ATTN_HEAD_DIM
    for c0 in range(0, D_ATTN, PROJ_CHUNK):
        kv = mm(OFF_K + c0, PROJ_CHUNK)
        for pr in range(PROJ_CHUNK // LANES):
            pair = c0 // LANES + pr
            k_aux = _dot(cparts_b, pk_ref[:, pair * 2 * LANES:(pair + 1) * 2 * LANES])
            k_data = kv[:, pr * LANES:(pr + 1) * LANES]
            ka_ref[0, :, 2 * pair * LANES:(2 * pair + 1) * LANES] = jnp.where(
                first, k_data, k_aux[:, :LANES]).astype(ka_ref.dtype)
            ka_ref[0, :, (2 * pair + 1) * LANES:(2 * pair + 2) * LANES] = jnp.where(
                first, k_aux[:, LANES:], k_data).astype(ka_ref.dtype)

    hd = ATTN_HEAD_DIM
    q_t = _dot_nt(wt_ref[0:D_ATTN, :], hn) * Q_SCALE
    v_t = _dot_nt(wt_ref[D_ATTN:2 * D_ATTN, :], hn)
    aux_t = _dot(pqt_ref[...], cparts.T.astype(BF16))
    ones = jnp.ones((hd, tm), vt_ref.dtype)
    for h in range(ATTN_HEADS):
        data0 = h * LANES + (0 if h % 2 == 0 else hd)
        aux0 = h * LANES + (hd if h % 2 == 0 else 0)
        qt_ref[0, data0:data0 + hd, :] = q_t[h * hd:(h + 1) * hd, :].astype(qt_ref.dtype)
        qt_ref[0, aux0:aux0 + hd, :] = aux_t[h * hd:(h + 1) * hd, :].astype(qt_ref.dtype)
        vt_ref[0, h * LANES:h * LANES + hd, :] = v_t[h * hd:(h + 1) * hd, :].astype(vt_ref.dtype)
        vt_ref[0, h * LANES + hd:(h + 1) * LANES, :] = ones

    off = OFF_PLAIN
    for (_, width), o_ref in zip(_PLAIN_GROUPS, (z_ref, xbc_ref, xr_ref, gate_ref, merge_ref)):
        for c0 in range(0, width, PROJ_CHUNK):
            o_ref[0, :, c0:c0 + PROJ_CHUNK] = mm(off + c0, PROJ_CHUNK).astype(o_ref.dtype)
        off += width


def _inproj(x, g, w, wt, fb, pk, pqt, b, lp):
    tm = PROJ_ROWS
    row_spec = lambda width: pl.BlockSpec((1, tm, width), lambda bi, ti: (bi, ti, 0))
    col_spec = pl.BlockSpec((1, D_AUG, tm), lambda bi, ti: (bi, 0, ti))
    plain = [w_ for _, w_ in _PLAIN_GROUPS]
    out_shape = ([jax.ShapeDtypeStruct((b, D_AUG, lp), BF16), jax.ShapeDtypeStruct((b, lp, D_AUG), BF16),
                  jax.ShapeDtypeStruct((b, D_AUG, lp), BF16)]
                 + [jax.ShapeDtypeStruct((b, lp, w_), BF16) for w_ in plain]
                 + [jax.ShapeDtypeStruct((b, lp, FDT_COLS), F32)])
    return pl.pallas_call(
        _inproj_body,
        out_shape=out_shape,
        grid=(b, lp // tm),
        in_specs=[row_spec(D_MODEL),
                  _const_spec((1, D_MODEL)),
                  _const_spec((D_MODEL, N_PROJ)),
                  _const_spec((2 * D_ATTN, D_MODEL)),
                  _const_spec((1, LANES)),
                  _const_spec((LANES, D_AUG)),
                  _const_spec((ATTN_HEADS * ATTN_HEAD_DIM, LANES))],
        out_specs=[col_spec, row_spec(D_AUG), col_spec] + [row_spec(w_) for w_ in plain] + [row_spec(FDT_COLS)],
        scratch_shapes=[pltpu.VMEM((SUBLANES, LANES), F32)],
        compiler_params=_cparams(("parallel", "arbitrary")),
        name="inproj",
    )(x.reshape(b, lp, D_MODEL), g, w, wt, fb, pk, pqt)


def _aux_constants():
    hd = ATTN_HEAD_DIM
    src = jnp.arange(LANES)[:, None]
    col = jnp.arange(D_AUG)[None, :]
    head = col // LANES
    pos = col % LANES - jnp.where(head % 2 == 0, hd, 0)
    part = src // ATTN_HEADS
    minus = (part < AUX_PARTS) & (src % ATTN_HEADS == head) & (pos == part + AUX_PARTS)
    plus = (src == ONE_LANE) & (pos >= 0) & (pos < AUX_PARTS)
    pk = plus.astype(F32) - minus.astype(F32)
    r = jnp.arange(ATTN_HEADS * hd)[:, None]
    src = jnp.arange(LANES)[None, :]
    part = src // ATTN_HEADS
    cpart = (part < AUX_PARTS) & (src % ATTN_HEADS == r // hd) & (r % hd == part)
    one = (src == ONE_LANE) & (r % hd >= AUX_PARTS) & (r % hd < 2 * AUX_PARTS)
    pqt = cpart.astype(F32) + one.astype(F32)
    return pk.astype(BF16), pqt.astype(BF16)


def _attn_body(qt_ref, ka_ref, vt_ref, mb_ref, o_ref, m_scr, acc_scr, s_scr, *, tq, hps):
    iq = pl.program_id(2)
    m_scr[...] = jnp.full(m_scr.shape, NEG_BIG, F32)
    acc_scr[...] = jnp.zeros(acc_scr.shape, F32)
    hd = ATTN_HEAD_DIM

    def scores(j, h):
        ks = pl.multiple_of(j * tq, tq)
        ka = ka_ref[0, pl.ds(ks, tq), h * LANES:(h + 1) * LANES]
        return _dot(ka, qt_ref[0, h * LANES:(h + 1) * LANES, :])

    def softmax_pv(j, h, st):
        ks = pl.multiple_of(j * tq, tq)
        vt = vt_ref[0, h * LANES:(h + 1) * LANES, pl.ds(ks, tq)]
        m_prev = m_scr[h]
        m_new = jnp.maximum(m_prev, jnp.max(st, axis=0, keepdims=True))
        alpha = jnp.exp2(m_prev - m_new)
        pt = jnp.exp2((st - m_new).astype(vt.dtype))
        acc_scr[h] = alpha * acc_scr[h] + _dot(vt, pt)
        m_scr[h] = m_new

    for h in range(hps):
        s_scr[h] = scores(0, h)

    def loop_body(j, carry):
        for h in range(hps):
            s_next = scores(j + 1, h)
            softmax_pv(j, h, s_scr[h])
            s_scr[h] = s_next
        return carry

    lax.fori_loop(0, iq, loop_body, 0)
    for h in range(hps):
        softmax_pv(iq, h, s_scr[h] + mb_ref[...])

    for hp in range(hps // 2):
        outs = []
        for e in range(2):
            acc = acc_scr[2 * hp + e]
            outs.append(acc[0:hd, :] / acc[hd:hd + 1, :])
        o_ref[0, :, hp * LANES:(hp + 1) * LANES] = jnp.concatenate(outs, axis=0).T.astype(o_ref.dtype)


def _attention(qt, ka, vt, b, lp):
    tq, hps = ATTN_TQ, ATTN_HPS
    key_i = jnp.arange(tq)[:, None]
    query_i = jnp.arange(tq)[None, :]
    causal_bias = jnp.where(key_i <= query_i, 0.0, NEG_BIG).astype(F32)
    return pl.pallas_call(
        functools.partial(_attn_body, tq=tq, hps=hps),
        out_shape=jax.ShapeDtypeStruct((b, lp, D_ATTN), BF16),
        grid=(b, ATTN_HEADS // hps, lp // tq),
        in_specs=[pl.BlockSpec((1, hps * LANES, tq), lambda bi, hi, qi: (bi, hi, qi)),
                  pl.BlockSpec((1, lp, hps * LANES), lambda bi, hi, qi: (bi, 0, hi)),
                  pl.BlockSpec((1, hps * LANES, lp), lambda bi, hi, qi: (bi, hi, 0)),
                  pl.BlockSpec((tq, tq), lambda bi, hi, qi: (0, 0))],
        out_specs=pl.BlockSpec((1, tq, hps * ATTN_HEAD_DIM), lambda bi, hi, qi: (bi, qi, hi)),
        scratch_shapes=[pltpu.VMEM((hps, 1, tq), F32),
                        pltpu.VMEM((hps, LANES, tq), F32),
                        pltpu.VMEM((hps, tq, tq), F32)],
        compiler_params=_cparams(("parallel", "parallel", "arbitrary")),
        name="fox_attention",
    )(qt, ka, vt, causal_bias)


def _causal_conv(x, xp_scr, w_ref, b_ref, first_tile, rows, taps):
    @pl.when(first_tile)
    def _():
        xp_scr[0:SUBLANES, :] = jnp.zeros((SUBLANES, x.shape[1]), F32)

    xp_scr[SUBLANES:SUBLANES + rows, :] = x
    y = b_ref[...] + w_ref[taps - 1:taps, :] * x
    for kk in range(taps - 1):
        r0 = SUBLANES - (taps - 1) + kk
        y = y + w_ref[kk:kk + 1, :] * xp_scr[r0:r0 + rows, :]
    xp_scr[0:SUBLANES, :] = x[rows - SUBLANES:rows, :]
    return y


def _ssd_body(xbc_ref, z_ref, fdt_ref, cw_ref, cb_ref, dtb_ref, alog_ref, dfull_ref, nw_ref,
              exp_ref, o_ref, xp_scr, st_scr):
    q = SSD_CHUNK
    first_tile = pl.program_id(1) == 0

    @pl.when(first_tile)
    def _():
        st_scr[...] = jnp.zeros(st_scr.shape, F32)

    y = _causal_conv(xbc_ref[0].astype(F32), xp_scr, cw_ref, cb_ref, first_tile, q, SSD_CONV)
    xc = y * _sigmoid(y)
    gs = D_SSD // SSD_GROUPS
    heads_per_group = SSD_HEADS // SSD_GROUPS

    lane = lax.broadcasted_iota(jnp.int32, (q, LANES), 1)
    dt_lane = (lane >= DT_LANE0) & (lane < DT_LANE0 + SSD_HEADS)
    dt = jnp.where(dt_lane, _softplus(fdt_ref[0] + dtb_ref[...]), 0.0)
    a = -jnp.exp(alog_ref[...])
    da = dt * a
    row = lax.broadcasted_iota(jnp.int32, (q, q), 0)
    col = lax.broadcasted_iota(jnp.int32, (q, q), 1)
    lower = row >= col
    a_cum = _dot_exact(lower.astype(F32), da)
    a_cum_t = a_cum.T
    expand = exp_ref[...]
    dt_full = _dot_exact(dt, expand)
    a_cum_full = _dot_exact(a_cum, expand)
    a_last_full = a_cum_full[q - 1:q, :]
    decay_to_end = jnp.exp(a_last_full - a_cum_full)
    decay_from_start = jnp.exp(a_cum_full)
    chunk_decay = jnp.exp(a_last_full)

    xs = xc[:, :D_SSD]
    xdt = xs * dt_full
    xdt_b = xdt.astype(BF16)
    xde_b = (xdt * decay_to_end).astype(BF16)
    half = lax.broadcasted_iota(jnp.int32, (q, LANES), 1) < SSD_HEAD_DIM

    for g in range(SSD_GROUPS):
        bm = xc[:, D_SSD + g * SSD_STATE:D_SSD + (g + 1) * SSD_STATE]
        cm = xc[:, D_SSD + SSD_GROUPS * SSD_STATE + g * SSD_STATE:
                D_SSD + SSD_GROUPS * SSD_STATE + (g + 1) * SSD_STATE]
        bm_b = bm.astype(BF16)
        cm_b = cm.astype(BF16)
        cb = _dot_nt(cm_b, bm_b)
        y_pairs = []
        for pair in range(heads_per_group // 2):
            c0 = g * gs + pair * LANES
            xp = xdt_b[:, c0:c0 + LANES]
            ys = []
            for e in range(2):
                hl = DT_LANE0 + g * heads_per_group + 2 * pair + e
                seg = a_cum[:, hl:hl + 1] - a_cum_t[hl:hl + 1, :]
                dec = jnp.exp(jnp.where(lower, seg, -jnp.inf))
                ys.append(_dot((cb * dec).astype(BF16), xp))
            y_pairs.append(jnp.where(half, ys[0], ys[1]))
        y_diag = jnp.concatenate(y_pairs, axis=1)
        sl = slice(g * gs, (g + 1) * gs)
        prev = st_scr[g]
        y_off = _dot(cm_b, prev.astype(BF16)) * decay_from_start[:, sl]
        st_scr[g] = prev * chunk_decay[:, sl] + _dot(bm.T.astype(BF16), xde_b[:, sl])
        yg = y_diag + y_off + dfull_ref[:, sl] * xs[:, sl]
        zg = z_ref[0, :, sl].astype(F32)
        yg = yg * (zg * _sigmoid(zg))
        yg = yg * lax.rsqrt(jnp.mean(yg * yg, axis=-1, keepdims=True) + NORM_EPS)
        o_ref[0, :, sl] = (yg * nw_ref[:, sl]).astype(o_ref.dtype)


def _ssd(xbc, z, fdt, cw, cb, dtb, alog, dfull, nw, expand, b, lp):
    q = SSD_CHUNK
    return pl.pallas_call(
        _ssd_body,
        out_shape=jax.ShapeDtypeStruct((b, lp, D_SSD), BF16),
        grid=(b, lp // q),
        in_specs=[pl.BlockSpec((1, q, D_XBC), lambda bi, ci: (bi, ci, 0)),
                  pl.BlockSpec((1, q, D_SSD), lambda bi, ci: (bi, ci, 0)),
                  pl.BlockSpec((1, q, FDT_COLS), lambda bi, ci: (bi, ci, 0)),
                  _const_spec((SSD_CONV, D_XBC)),
                  _const_spec((1, D_XBC)),
                  _const_spec((1, LANES)),
                  _const_spec((1, LANES)),
                  _const_spec((1, D_SSD)),
                  _const_spec((1, D_SSD)),
                  _const_spec((LANES, D_SSD))],
        out_specs=pl.BlockSpec((1, q, D_SSD), lambda bi, ci: (bi, ci, 0)),
        scratch_shapes=[pltpu.VMEM((SUBLANES + q, D_XBC), F32),
                        pltpu.VMEM((SSD_GROUPS, SSD_STATE, D_SSD // SSD_GROUPS), F32)],
        compiler_params=_cparams(("parallel", "arbitrary")),
        name="ssd",
    )(xbc, z, fdt, cw, cb, dtb, alog, dfull, nw, expand)


def _lru_body(xr_ref, gate_ref, cw_ref, cb_ref, w2_ref, ba_ref, bx_ref, lam_ref, o_ref,
              xp_scr, h_scr):
    rows = SCAN_ROWS
    first_tile = pl.program_id(1) == 0

    @pl.when(first_tile)
    def _():
        h_scr[...] = jnp.zeros(h_scr.shape, F32)

    xc = _causal_conv(xr_ref[0].astype(F32), xp_scr, cw_ref, cb_ref, first_tile, rows, LRU_CONV)
    xc_b = xc.astype(BF16)
    pre = [_dot(xc_b[:, j * LANES:(j + 1) * LANES], w2_ref[j]) for j in range(D_LRU // LANES)]
    pre_a = jnp.concatenate([p[:, :LANES] for p in pre], axis=1)
    pre_x = jnp.concatenate([p[:, LANES:] for p in pre], axis=1)
    r = _sigmoid(pre_a + ba_ref[...])
    i = _sigmoid(pre_x + bx_ref[...])
    log_a = LRU_C * r * _log_sigmoid(lam_ref[...])
    a = jnp.exp(log_a)
    mult = jnp.sqrt(-jnp.tanh(log_a) * (a * a + 1.0))
    rowi = lax.broadcasted_iota(jnp.int32, (rows, D_LRU), 0)
    mult = jnp.where(first_tile & (rowi == 0), 1.0, mult)
    u = mult * (i * xc)

    groups = rows // SUBLANES
    a3 = a.reshape(groups, SUBLANES, D_LRU)
    u3 = u.reshape(groups, SUBLANES, D_LRU)
    sub = lax.broadcasted_iota(jnp.int32, (groups, SUBLANES, D_LRU), 1)
    d = 1
    while d < SUBLANES:
        keep = sub >= d
        a_s = jnp.where(keep, pltpu.roll(a3, d, 1), 1.0)
        u_s = jnp.where(keep, pltpu.roll(u3, d, 1), 0.0)
        u3 = a3 * u_s + u3
        a3 = a3 * a_s
        d *= 2
    h_prev = h_scr[0:1, :]
    hs = []
    for r in range(groups):
        h_r = a3[r] * h_prev + u3[r]
        hs.append(h_r)
        h_prev = h_r[SUBLANES - 1:SUBLANES, :]
    h = jnp.concatenate(hs, axis=0)
    h_scr[0:1, :] = h[rows - 1:rows, :]
    o_ref[0] = (h * jax.nn.gelu(gate_ref[0].astype(F32))).astype(o_ref.dtype)


def _lru(xr, gate, cw, cb, w2, ba, bx, lam, b, lp):
    rows = SCAN_ROWS
    return pl.pallas_call(
        _lru_body,
        out_shape=jax.ShapeDtypeStruct((b, lp, D_LRU), BF16),
        grid=(b, lp // rows),
        in_specs=[pl.BlockSpec((1, rows, D_LRU), lambda bi, ti: (bi, ti, 0)),
                  pl.BlockSpec((1, rows, D_LRU), lambda bi, ti: (bi, ti, 0)),
                  _const_spec((LRU_CONV, D_LRU)),
                  _const_spec((1, D_LRU)),
                  _const_spec((D_LRU // LANES, LANES, 2 * LANES)),
                  _const_spec((1, D_LRU)),
                  _const_spec((1, D_LRU)),
                  _const_spec((1, D_LRU))],
        out_specs=pl.BlockSpec((1, rows, D_LRU), lambda bi, ti: (bi, ti, 0)),
        scratch_shapes=[pltpu.VMEM((SUBLANES + rows, D_LRU), F32),
                        pltpu.VMEM((SUBLANES, D_LRU), F32)],
        compiler_params=_cparams(("parallel", "arbitrary")),
        name="rglru",
    )(xr, gate, cw, cb, w2, ba, bx, lam)


def _merge_body(h_ref, ya_ref, yb_ref, yc_ref, m_ref, wa_ref, wb_ref, wc_ref, wo_ref, o_ref):
    gate = lambda i: _sigmoid(m_ref[:, i * D_MODEL:(i + 1) * D_MODEL].astype(F32))
    mixed = gate(0) * _dot(ya_ref[...], wa_ref[...])
    mixed = mixed + gate(1) * _dot(yb_ref[...], wb_ref[...])
    mixed = mixed + gate(2) * _dot(yc_ref[...], wc_ref[...])
    o_ref[...] = h_ref[...] + _dot(mixed.astype(BF16), wo_ref[...])


def _merge(h, ya, yb, yc, m, wa, wb, wc, wo):
    t = h.shape[0]
    tm = MERGE_ROWS
    row_spec = lambda width: pl.BlockSpec((tm, width), lambda i: (i, 0))
    return pl.pallas_call(
        _merge_body,
        out_shape=jax.ShapeDtypeStruct((t, D_MODEL), F32),
        grid=(t // tm,),
        in_specs=[row_spec(D_MODEL), row_spec(D_ATTN), row_spec(D_SSD), row_spec(D_LRU),
                  row_spec(N_BRANCH * D_MODEL),
                  _const_spec((D_ATTN, D_MODEL)), _const_spec((D_SSD, D_MODEL)),
                  _const_spec((D_LRU, D_MODEL)), _const_spec((D_MODEL, D_MODEL))],
        out_specs=row_spec(D_MODEL),
        compiler_params=_cparams(("parallel",)),
        name="merge_out",
    )(h, ya, yb, yc, m, wa, wb, wc, wo)


def _prep_w_in(w_in):
    sizes = (D_ATTN, D_ATTN, D_ATTN, ATTN_HEADS, D_SSD, D_XBC, SSD_HEADS, D_LRU, D_LRU, N_BRANCH * D_MODEL)
    offs = [0]
    for s in sizes:
        offs.append(offs[-1] + s)
    part = lambda i: w_in[:, offs[i]:offs[i + 1]]
    q, k, v, f, z, xbc, dt, xr, gate, merge = (part(i) for i in range(10))
    pad = jnp.zeros((D_MODEL, FDT_COLS - ATTN_HEADS - SSD_HEADS), w_in.dtype)
    w = jnp.concatenate([k, z, xbc, xr, gate, merge, f, dt, pad], axis=1).astype(BF16)
    wt = jnp.concatenate([q.T, v.T], axis=0).astype(BF16)
    return w, wt


def _pad_lanes(vec, lane0):
    out = jnp.zeros((1, LANES), F32)
    return out.at[0, lane0:lane0 + vec.shape[0]].set(vec.astype(F32))


def _lru_gate_weights(w_a, w_x):
    def blockdiag_pairs(w):
        w = w.reshape(LRU_BLOCKS // 2, 2, LRU_BLOCK_DIM, LRU_BLOCK_DIM)
        zero = jnp.zeros_like(w[:, 0])
        top = jnp.concatenate([w[:, 0], zero], axis=2)
        bot = jnp.concatenate([zero, w[:, 1]], axis=2)
        return jnp.concatenate([top, bot], axis=1)
    return jnp.concatenate([blockdiag_pairs(w_a), blockdiag_pairs(w_x)], axis=2).astype(BF16)


def _head_expand():
    rows = jnp.arange(LANES)[:, None]
    cols = jnp.arange(D_SSD)[None, :]
    return (rows == DT_LANE0 + cols // SSD_HEAD_DIM).astype(F32)


def kernel(x, meta_tokens, ffn1_norm, ffn1_w_gate_up, ffn1_w_down, mix_norm, w_in, fox_forget_bias,
           ssd_conv_w, ssd_conv_b, ssd_dt_bias, ssd_a_log, ssd_d, ssd_norm,
           lru_conv_w, lru_conv_b, lru_w_a, lru_b_a, lru_w_x, lru_b_x, lru_lambda,
           w_branch_attn, w_branch_ssd, w_branch_lru, w_out,
           ffn2_norm, ffn2_w_gate_up, ffn2_w_down, final_norm):
    b, s, d = x.shape
    depth = w_in.shape[0]
    length = N_META + s
    lp = -(-length // SEQ_ALIGN) * SEQ_ALIGN
    t = b * lp
    assert d == D_MODEL and t % FFN_ROWS == 0 and t % MERGE_ROWS == 0

    meta = jnp.broadcast_to(meta_tokens.astype(x.dtype)[None], (b, N_META, d))
    h = jnp.concatenate([meta, x, jnp.zeros((b, lp - length, d), x.dtype)], axis=1).reshape(t, d)

    row = lambda vec: vec.astype(F32).reshape(1, -1)
    expand = _head_expand()
    pk, pqt = _aux_constants()
    fg = row(final_norm)
    for l in range(depth):
        h = _ffn(h, row(ffn1_norm[l]), ffn1_w_gate_up[l, :, :D_FF].astype(BF16),
                 ffn1_w_gate_up[l, :, D_FF:].astype(BF16), ffn1_w_down[l].astype(BF16), fg, False)

        w_std, w_t = _prep_w_in(w_in[l])
        qt, ka, vt, z, xbc, xr, gate, merge, fdt = _inproj(
            h, row(mix_norm[l]), w_std, w_t, _pad_lanes(fox_forget_bias[l], 0), pk, pqt, b, lp)
        y_a = _attention(qt, ka, vt, b, lp).reshape(t, D_ATTN)
        y_b = _ssd(xbc, z, fdt, ssd_conv_w[l].astype(F32), row(ssd_conv_b[l]),
                   _pad_lanes(ssd_dt_bias[l], DT_LANE0), _pad_lanes(ssd_a_log[l], DT_LANE0),
                   row(jnp.repeat(ssd_d[l], SSD_HEAD_DIM)), row(ssd_norm[l]), expand, b, lp).reshape(t, D_SSD)
        y_c = _lru(xr, gate, lru_conv_w[l].astype(F32), row(lru_conv_b[l]),
                   _lru_gate_weights(lru_w_a[l], lru_w_x[l]), row(lru_b_a[l]), row(lru_b_x[l]),
                   row(lru_lambda[l]), b, lp).reshape(t, D_LRU)
        h = _merge(h, y_a, y_b, y_c, merge.reshape(t, N_BRANCH * D_MODEL), w_branch_attn[l].astype(BF16),
                   w_branch_ssd[l].astype(BF16), w_branch_lru[l].astype(BF16), w_out[l].astype(BF16))

        h = _ffn(h, row(ffn2_norm[l]), ffn2_w_gate_up[l, :, :D_FF].astype(BF16),
                 ffn2_w_gate_up[l, :, D_FF:].astype(BF16), ffn2_w_down[l].astype(BF16), fg,
                 l == depth - 1)
    return h.reshape(b, lp, d)[:, N_META:N_META + s]
```

```python
import functools

import jax
import jax.numpy as jnp
from jax import lax
from jax.experimental import pallas as pl
from jax.experimental.pallas import tpu as pltpu

F32 = jnp.float32
BF16 = jnp.bfloat16

D_MODEL = 1024
N_META = 16
SSD_CHUNK = 128
NORM_EPS = 1e-6
ATTN_HEADS = 16
ATTN_HEAD_DIM = 64
D_ATTN = ATTN_HEADS * ATTN_HEAD_DIM
SSD_HEAD_DIM = 64
D_SSD = D_MODEL
SSD_HEADS = D_SSD // SSD_HEAD_DIM
SSD_GROUPS = 2
SSD_STATE = 128
SSD_CONV = 4
D_XBC = D_SSD + 2 * SSD_GROUPS * SSD_STATE
D_LRU = D_MODEL
LRU_BLOCKS = 16
LRU_BLOCK_DIM = D_LRU // LRU_BLOCKS
LRU_CONV = 4
LRU_C = 8.0
D_FF = 2816
N_BRANCH = 3

LANES = 128
SUBLANES = 8
VMEM_LIMIT_BYTES = 56 * 1024 * 1024

SEQ_ALIGN = 256
FFN_ROWS = 512
FFN_CHUNK = 256
PROJ_ROWS = 256
PROJ_CHUNK = 512
ATTN_TQ = 256
ATTN_HPS = 16
MERGE_ROWS = 512
SCAN_ROWS = 128
FDT_COLS = LANES
DT_LANE0 = ATTN_HEADS
NEG_BIG = -1e30
LOG2E = 1.4426950408889634
Q_SCALE = ATTN_HEAD_DIM ** -0.5 * LOG2E
AUX_PARTS = 3


def _cparams(sem):
    return pltpu.CompilerParams(dimension_semantics=sem, vmem_limit_bytes=VMEM_LIMIT_BYTES)


def _const_spec(shape):
    nd = len(shape)
    return pl.BlockSpec(shape, lambda *_: (0,) * nd, pipeline_mode=pl.Buffered(1))


def _rms(x, g):
    ms = jnp.mean(x * x, axis=-1, keepdims=True)
    return (x * lax.rsqrt(ms + NORM_EPS)) * g


def _dot(a, b):
    return jnp.dot(a, b, preferred_element_type=F32)


def _dot_nt(a, b):
    return lax.dot_general(a, b, (((1,), (1,)), ((), ())), preferred_element_type=F32)


def _split3(x):
    hi = x.astype(BF16)
    r1 = x - hi.astype(F32)
    mid = r1.astype(BF16)
    lo = (r1 - mid.astype(F32)).astype(BF16)
    return hi, mid, lo


def _dot_01_lhs(sel, x):
    hi, mid, lo = _split3(x)
    return _dot(sel, hi) + _dot(sel, mid) + _dot(sel, lo)


def _dot_01_rhs(x, sel):
    hi, mid, lo = _split3(x)
    return _dot(hi, sel) + _dot(mid, sel) + _dot(lo, sel)


def _log_sigmoid(x):
    return -(jnp.maximum(-x, 0.0) + jnp.log1p(jnp.exp(-jnp.abs(x))))


def _softplus(x):
    return jnp.maximum(x, 0.0) + jnp.log1p(jnp.exp(-jnp.abs(x)))


def _sigmoid(x):
    return 1.0 / (1.0 + jnp.exp(-x))


def _ffn_body(x_ref, g_ref, wg_ref, wu_ref, wd_ref, fg_ref, o_ref, a_scr, *, final_norm):
    x = x_ref[...]
    hn = _rms(x, g_ref[...]).astype(BF16)
    for c0 in range(0, D_FF, FFN_CHUNK):
        gate = _dot(hn, wg_ref[:, c0:c0 + FFN_CHUNK])
        up = _dot(hn, wu_ref[:, c0:c0 + FFN_CHUNK])
        a_scr[:, c0:c0 + FFN_CHUNK] = ((gate * _sigmoid(gate)) * up).astype(BF16)
    y = x + 0.5 * _dot(a_scr[...], wd_ref[...])
    if final_norm:
        y = _rms(y, fg_ref[...])
    o_ref[...] = y


def _ffn(x, g, wg, wu, wd, fg, final_norm):
    t = x.shape[0]
    tm = FFN_ROWS
    return pl.pallas_call(
        functools.partial(_ffn_body, final_norm=final_norm),
        out_shape=jax.ShapeDtypeStruct((t, D_MODEL), F32),
        grid=(t // tm,),
        in_specs=[pl.BlockSpec((tm, D_MODEL), lambda i: (i, 0)),
                  _const_spec((1, D_MODEL)),
                  _const_spec((D_MODEL, D_FF)),
                  _const_spec((D_MODEL, D_FF)),
                  _const_spec((D_FF, D_MODEL)),
                  _const_spec((1, D_MODEL))],
        out_specs=pl.BlockSpec((tm, D_MODEL), lambda i: (i, 0)),
        scratch_shapes=[pltpu.VMEM((tm, D_FF), BF16)],
        compiler_params=_cparams(("parallel",)),
        name="ffn",
    )(x, g, wg, wu, wd, fg)


_PLAIN_GROUPS = (("z", D_SSD), ("xbc", D_XBC), ("xr", D_LRU), ("gate", D_LRU), ("merge", N_BRANCH * D_MODEL))
OFF_K = 0
OFF_PLAIN = D_ATTN
OFF_FDT = OFF_PLAIN + sum(w for _, w in _PLAIN_GROUPS)
N_PROJ = OFF_FDT + FDT_COLS
D_AUG = ATTN_HEADS * LANES
ONE_LANE = LANES - 1


def _inproj_body(x_ref, g_ref, w_ref, wt_ref, fb_ref, pk_ref, pqt_ref, qt_ref, ka_ref, vt_ref,
                 z_ref, xbc_ref, xr_ref, gate_ref, merge_ref, fdt_ref, carry_scr):
    tm = x_ref.shape[1]

    @pl.when(pl.program_id(1) == 0)
    def _():
        carry_scr[...] = jnp.zeros(carry_scr.shape, F32)

    hn = _rms(x_ref[0], g_ref[...]).astype(BF16)

    def mm(c0, width):
        return _dot(hn, w_ref[:, c0:c0 + width])

    fdt = mm(OFF_FDT, FDT_COLS)
    fdt_ref[0] = fdt
    lane = lax.broadcasted_iota(jnp.int32, (tm, LANES), 1)
    lf = jnp.where(lane < ATTN_HEADS, _log_sigmoid(fdt + fb_ref[...]), 0.0)
    row = lax.broadcasted_iota(jnp.int32, (tm, tm), 0)
    col = lax.broadcasted_iota(jnp.int32, (tm, tm), 1)
    c = _dot_01_lhs((row >= col).astype(BF16), lf) + carry_scr[0:1, :]
    carry_scr[0:1, :] = c[tm - 1:tm, :]
    hi, mid, lo = (part.astype(F32) for part in _split3(c * LOG2E))
    cparts = (hi + pltpu.roll(mid, ATTN_HEADS, 1) + pltpu.roll(lo, 2 * ATTN_HEADS, 1)
              + jnp.where(lane == ONE_LANE, 1.0, 0.0))

    cparts_b = cparts.astype(BF16)
    first = lane < ATTN_HEAD_DIM
    for c0 in range(0, D_ATTN, PROJ_CHUNK):
        kv = mm(OFF_K + c0, PROJ_CHUNK)
        for pr in range(PROJ_CHUNK // LANES):
            pair = c0 // LANES + pr
            k_aux = _dot(cparts_b, pk_ref[:, pair * 2 * LANES:(pair + 1) * 2 * LANES])
            k_data = kv[:, pr * LANES:(pr + 1) * LANES]
            ka_ref[0, :, 2 * pair * LANES:(2 * pair + 1) * LANES] = jnp.where(
                first, k_data, k_aux[:, :LANES]).astype(ka_ref.dtype)
            ka_ref[0, :, (2 * pair + 1) * LANES:(2 * pair + 2) * LANES] = jnp.where(
                first, k_aux[:, LANES:], k_data).astype(ka_ref.dtype)

    hd = ATTN_HEAD_DIM
    q_t = _dot_nt(wt_ref[0:D_ATTN, :], hn) * Q_SCALE
    v_t = _dot_nt(wt_ref[D_ATTN:2 * D_ATTN, :], hn)
    aux_t = _dot(pqt_ref[...], cparts.T.astype(BF16))
    ones = jnp.ones((hd, tm), vt_ref.dtype)
    for h in range(ATTN_HEADS):
        data0 = h * LANES + (0 if h % 2 == 0 else hd)
        aux0 = h * LANES + (hd if h % 2 == 0 else 0)
        qt_ref[0, data0:data0 + hd, :] = q_t[h * hd:(h + 1) * hd, :].astype(qt_ref.dtype)
        qt_ref[0, aux0:aux0 + hd, :] = aux_t[h * hd:(h + 1) * hd, :].astype(qt_ref.dtype)
        vt_ref[0, h * LANES:h * LANES + hd, :] = v_t[h * hd:(h + 1) * hd, :].astype(vt_ref.dtype)
        vt_ref[0, h * LANES + hd:(h + 1) * LANES, :] = ones

    off = OFF_PLAIN
    for (_, width), o_ref in zip(_PLAIN_GROUPS, (z_ref, xbc_ref, xr_ref, gate_ref, merge_ref)):
        for c0 in range(0, width, PROJ_CHUNK):
            o_ref[0, :, c0:c0 + PROJ_CHUNK] = mm(off + c0, PROJ_CHUNK).astype(o_ref.dtype)
        off += width


def _inproj(x, g, w, wt, fb, pk, pqt, b, lp):
    tm = PROJ_ROWS
    row_spec = lambda width: pl.BlockSpec((1, tm, width), lambda bi, ti: (bi, ti, 0))
    col_spec = pl.BlockSpec((1, D_AUG, tm), lambda bi, ti: (bi, 0, ti))
    plain = [w_ for _, w_ in _PLAIN_GROUPS]
    out_shape = ([jax.ShapeDtypeStruct((b, D_AUG, lp), BF16), jax.ShapeDtypeStruct((b, lp, D_AUG), BF16),
                  jax.ShapeDtypeStruct((b, D_AUG, lp), BF16)]
                 + [jax.ShapeDtypeStruct((b, lp, w_), BF16) for w_ in plain]
                 + [jax.ShapeDtypeStruct((b, lp, FDT_COLS), F32)])
    return pl.pallas_call(
        _inproj_body,
        out_shape=out_shape,
        grid=(b, lp // tm),
        in_specs=[row_spec(D_MODEL),
                  _const_spec((1, D_MODEL)),
                  _const_spec((D_MODEL, N_PROJ)),
                  _const_spec((2 * D_ATTN, D_MODEL)),
                  _const_spec((1, LANES)),
                  _const_spec((LANES, D_AUG)),
                  _const_spec((ATTN_HEADS * ATTN_HEAD_DIM, LANES))],
        out_specs=[col_spec, row_spec(D_AUG), col_spec] + [row_spec(w_) for w_ in plain] + [row_spec(FDT_COLS)],
        scratch_shapes=[pltpu.VMEM((SUBLANES, LANES), F32)],
        compiler_params=_cparams(("parallel", "arbitrary")),
        name="inproj",
    )(x.reshape(b, lp, D_MODEL), g, w, wt, fb, pk, pqt)


def _aux_constants():
    hd = ATTN_HEAD_DIM
    src = jnp.arange(LANES)[:, None]
    col = jnp.arange(D_AUG)[None, :]
    head = col // LANES
    pos = col % LANES - jnp.where(head % 2 == 0, hd, 0)
    part = src // ATTN_HEADS
    minus = (part < AUX_PARTS) & (src % ATTN_HEADS == head) & (pos == part + AUX_PARTS)
    plus = (src == ONE_LANE) & (pos >= 0) & (pos < AUX_PARTS)
    pk = plus.astype(F32) - minus.astype(F32)
    r = jnp.arange(ATTN_HEADS * hd)[:, None]
    src = jnp.arange(LANES)[None, :]
    part = src // ATTN_HEADS
    cpart = (part < AUX_PARTS) & (src % ATTN_HEADS == r // hd) & (r % hd == part)
    one = (src == ONE_LANE) & (r % hd >= AUX_PARTS) & (r % hd < 2 * AUX_PARTS)
    pqt = cpart.astype(F32) + one.astype(F32)
    return pk.astype(BF16), pqt.astype(BF16)


def _attn_body(qt_ref, qn_ref, ka_ref, vt_ref, mb_ref, o_ref, m_scr, acc_scr, s_scr, *, tq, hps):
    iq = pl.program_id(2)
    m_scr[...] = jnp.full(m_scr.shape, NEG_BIG, F32)
    acc_scr[...] = jnp.zeros(acc_scr.shape, F32)
    hd = ATTN_HEAD_DIM

    def scores(j, h):
        ks = pl.multiple_of(j * tq, tq)
        ka = ka_ref[0, pl.ds(ks, tq), h * LANES:(h + 1) * LANES]
        return _dot(ka, qt_ref[0, h * LANES:(h + 1) * LANES, :])

    def softmax_pv(j, h, st):
        ks = pl.multiple_of(j * tq, tq)
        vt = vt_ref[0, h * LANES:(h + 1) * LANES, pl.ds(ks, tq)]
        m_prev = m_scr[h]
        m_new = jnp.maximum(m_prev, jnp.max(st, axis=0, keepdims=True))
        alpha = jnp.exp2(m_prev - m_new)
        pt = jnp.exp2((st - m_new).astype(vt.dtype))
        acc_scr[h] = alpha * acc_scr[h] + _dot(vt, pt)
        m_scr[h] = m_new

    @pl.when(iq == 0)
    def _():
        for h in range(hps):
            s_scr[h] = scores(0, h)

    def loop_body(j, carry):
        for h in range(hps):
            s_next = scores(j + 1, h)
            softmax_pv(j, h, s_scr[h])
            s_scr[h] = s_next
        return carry

    lax.fori_loop(0, iq, loop_body, 0)
    for h in range(hps):
        s_next = _dot(ka_ref[0, 0:tq, h * LANES:(h + 1) * LANES], qn_ref[0, h * LANES:(h + 1) * LANES, :])
        softmax_pv(iq, h, s_scr[h] + mb_ref[...])
        s_scr[h] = s_next

    for hp in range(hps // 2):
        outs = []
        for e in range(2):
            acc = acc_scr[2 * hp + e]
            outs.append(acc[0:hd, :] / acc[hd:hd + 1, :])
        o_ref[0, :, hp * LANES:(hp + 1) * LANES] = jnp.concatenate(outs, axis=0).T.astype(o_ref.dtype)


def _attention(qt, ka, vt, b, lp):
    tq, hps = ATTN_TQ, ATTN_HPS
    key_i = jnp.arange(tq)[:, None]
    query_i = jnp.arange(tq)[None, :]
    causal_bias = jnp.where(key_i <= query_i, 0.0, NEG_BIG).astype(F32)
    last_tile = lp // tq - 1
    return pl.pallas_call(
        functools.partial(_attn_body, tq=tq, hps=hps),
        out_shape=jax.ShapeDtypeStruct((b, lp, D_ATTN), BF16),
        grid=(b, ATTN_HEADS // hps, lp // tq),
        in_specs=[pl.BlockSpec((1, hps * LANES, tq), lambda bi, hi, qi: (bi, hi, qi)),
                  pl.BlockSpec((1, hps * LANES, tq), lambda bi, hi, qi: (bi, hi, jnp.minimum(qi + 1, last_tile))),
                  pl.BlockSpec((1, lp, hps * LANES), lambda bi, hi, qi: (bi, 0, hi), pipeline_mode=pl.Buffered(1)),
                  pl.BlockSpec((1, hps * LANES, lp), lambda bi, hi, qi: (bi, hi, 0), pipeline_mode=pl.Buffered(1)),
                  pl.BlockSpec((tq, tq), lambda bi, hi, qi: (0, 0))],
        out_specs=pl.BlockSpec((1, tq, hps * ATTN_HEAD_DIM), lambda bi, hi, qi: (bi, qi, hi)),
        scratch_shapes=[pltpu.VMEM((hps, 1, tq), F32),
                        pltpu.VMEM((hps, LANES, tq), F32),
                        pltpu.VMEM((hps, tq, tq), F32)],
        compiler_params=_cparams(("parallel", "parallel", "arbitrary")),
        name="fox_attention",
    )(qt, qt, ka, vt, causal_bias)


def _causal_conv(x, xp_scr, w_ref, b_ref, first_tile, rows, taps):
    @pl.when(first_tile)
    def _():
        xp_scr[0:SUBLANES, :] = jnp.zeros((SUBLANES, x.shape[1]), F32)

    xp_scr[SUBLANES:SUBLANES + rows, :] = x
    y = b_ref[...] + w_ref[taps - 1:taps, :] * x
    for kk in range(taps - 1):
        r0 = SUBLANES - (taps - 1) + kk
        y = y + w_ref[kk:kk + 1, :] * xp_scr[r0:r0 + rows, :]
    xp_scr[0:SUBLANES, :] = x[rows - SUBLANES:rows, :]
    return y


def _ssd_body(xbc_ref, z_ref, fdt_ref, cw_ref, cb_ref, dtb_ref, alog_ref, dfull_ref, nw_ref,
              exp_ref, o_ref, xp_scr, st_scr):
    q = SSD_CHUNK
    first_tile = pl.program_id(1) == 0

    @pl.when(first_tile)
    def _():
        st_scr[...] = jnp.zeros(st_scr.shape, F32)

    y = _causal_conv(xbc_ref[0].astype(F32), xp_scr, cw_ref, cb_ref, first_tile, q, SSD_CONV)
    xc = y * _sigmoid(y)
    gs = D_SSD // SSD_GROUPS
    heads_per_group = SSD_HEADS // SSD_GROUPS

    lane = lax.broadcasted_iota(jnp.int32, (q, LANES), 1)
    dt_lane = (lane >= DT_LANE0) & (lane < DT_LANE0 + SSD_HEADS)
    dt = jnp.where(dt_lane, _softplus(fdt_ref[0] + dtb_ref[...]), 0.0)
    a = -jnp.exp(alog_ref[...])
    da = dt * a
    row = lax.broadcasted_iota(jnp.int32, (q, q), 0)
    col = lax.broadcasted_iota(jnp.int32, (q, q), 1)
    lower = row >= col
    a_cum = _dot_01_lhs(lower.astype(BF16), da)
    a_cum_t = a_cum.T
    expand = exp_ref[...]
    dt_full = _dot_01_rhs(dt, expand)
    a_cum_full = _dot_01_rhs(a_cum, expand)
    a_last_full = a_cum_full[q - 1:q, :]
    decay_to_end = jnp.exp(a_last_full - a_cum_full)
    decay_from_start = jnp.exp(a_cum_full)
    chunk_decay = jnp.exp(a_last_full)

    xs = xc[:, :D_SSD]
    xdt = xs * dt_full
    xdt_b = xdt.astype(BF16)
    xde_b = (xdt * decay_to_end).astype(BF16)
    half = lax.broadcasted_iota(jnp.int32, (q, LANES), 1) < SSD_HEAD_DIM

    for g in range(SSD_GROUPS):
        bm = xc[:, D_SSD + g * SSD_STATE:D_SSD + (g + 1) * SSD_STATE]
        cm = xc[:, D_SSD + SSD_GROUPS * SSD_STATE + g * SSD_STATE:
                D_SSD + SSD_GROUPS * SSD_STATE + (g + 1) * SSD_STATE]
        bm_b = bm.astype(BF16)
        cm_b = cm.astype(BF16)
        cb = _dot_nt(cm_b, bm_b)
        y_pairs = []
        for pair in range(heads_per_group // 2):
            c0 = g * gs + pair * LANES
            xp = xdt_b[:, c0:c0 + LANES]
            ys = []
            for e in range(2):
                hl = DT_LANE0 + g * heads_per_group + 2 * pair + e
                seg = a_cum[:, hl:hl + 1] - a_cum_t[hl:hl + 1, :]
                dec = jnp.exp(jnp.where(lower, seg, -jnp.inf))
                ys.append(_dot((cb * dec).astype(BF16), xp))
            y_pairs.append(jnp.where(half, ys[0], ys[1]))
        y_diag = jnp.concatenate(y_pairs, axis=1)
        sl = slice(g * gs, (g + 1) * gs)
        prev = st_scr[g]
        y_off = _dot(cm_b, prev.astype(BF16)) * decay_from_start[:, sl]
        st_scr[g] = prev * chunk_decay[:, sl] + _dot(bm.T.astype(BF16), xde_b[:, sl])
        yg = y_diag + y_off + dfull_ref[:, sl] * xs[:, sl]
        zg = z_ref[0, :, sl].astype(F32)
        yg = yg * (zg * _sigmoid(zg))
        yg = yg * lax.rsqrt(jnp.mean(yg * yg, axis=-1, keepdims=True) + NORM_EPS)
        o_ref[0, :, sl] = (yg * nw_ref[:, sl]).astype(o_ref.dtype)


def _ssd(xbc, z, fdt, cw, cb, dtb, alog, dfull, nw, expand, b, lp):
    q = SSD_CHUNK
    return pl.pallas_call(
        _ssd_body,
        out_shape=jax.ShapeDtypeStruct((b, lp, D_SSD), BF16),
        grid=(b, lp // q),
        in_specs=[pl.BlockSpec((1, q, D_XBC), lambda bi, ci: (bi, ci, 0)),
                  pl.BlockSpec((1, q, D_SSD), lambda bi, ci: (bi, ci, 0)),
                  pl.BlockSpec((1, q, FDT_COLS), lambda bi, ci: (bi, ci, 0)),
                  _const_spec((SSD_CONV, D_XBC)),
                  _const_spec((1, D_XBC)),
                  _const_spec((1, LANES)),
                  _const_spec((1, LANES)),
                  _const_spec((1, D_SSD)),
                  _const_spec((1, D_SSD)),
                  _const_spec((LANES, D_SSD))],
        out_specs=pl.BlockSpec((1, q, D_SSD), lambda bi, ci: (bi, ci, 0)),
        scratch_shapes=[pltpu.VMEM((SUBLANES + q, D_XBC), F32),
                        pltpu.VMEM((SSD_GROUPS, SSD_STATE, D_SSD // SSD_GROUPS), F32)],
        compiler_params=_cparams(("parallel", "arbitrary")),
        name="ssd",
    )(xbc, z, fdt, cw, cb, dtb, alog, dfull, nw, expand)


def _lru_body(xr_ref, gate_ref, cw_ref, cb_ref, w2_ref, ba_ref, bx_ref, lam_ref, o_ref,
              xp_scr, h_scr):
    rows = SCAN_ROWS
    first_tile = pl.program_id(1) == 0

    @pl.when(first_tile)
    def _():
        h_scr[...] = jnp.zeros(h_scr.shape, F32)

    xc = _causal_conv(xr_ref[0].astype(F32), xp_scr, cw_ref, cb_ref, first_tile, rows, LRU_CONV)
    xc_b = xc.astype(BF16)
    pre = [_dot(xc_b[:, j * LANES:(j + 1) * LANES], w2_ref[j]) for j in range(D_LRU // LANES)]
    pre_a = jnp.concatenate([p[:, :LANES] for p in pre], axis=1)
    pre_x = jnp.concatenate([p[:, LANES:] for p in pre], axis=1)
    r = _sigmoid(pre_a + ba_ref[...])
    i = _sigmoid(pre_x + bx_ref[...])
    log_a = LRU_C * r * _log_sigmoid(lam_ref[...])
    a = jnp.exp(log_a)
    mult = jnp.sqrt(-jnp.tanh(log_a) * (a * a + 1.0))
    row0 = lax.broadcasted_iota(jnp.int32, (SUBLANES, D_LRU), 0) == 0
    mult = jnp.concatenate([jnp.where(first_tile & row0, 1.0, mult[:SUBLANES]), mult[SUBLANES:]], axis=0)
    u = mult * (i * xc)

    groups = rows // SUBLANES
    a3 = a.reshape(groups, SUBLANES, D_LRU)
    u3 = u.reshape(groups, SUBLANES, D_LRU)
    sub = lax.broadcasted_iota(jnp.int32, (groups, SUBLANES, D_LRU), 1)
    d = 1
    while d < SUBLANES:
        keep = sub >= d
        a_s = jnp.where(keep, pltpu.roll(a3, d, 1), 1.0)
        u_s = jnp.where(keep, pltpu.roll(u3, d, 1), 0.0)
        u3 = a3 * u_s + u3
        a3 = a3 * a_s
        d *= 2
    h_prev = h_scr[0:1, :]
    hs = []
    for r in range(groups):
        h_r = a3[r] * h_prev + u3[r]
        hs.append(h_r)
        h_prev = h_r[SUBLANES - 1:SUBLANES, :]
    h = jnp.concatenate(hs, axis=0)
    h_scr[0:1, :] = h[rows - 1:rows, :]
    o_ref[0] = (h * jax.nn.gelu(gate_ref[0].astype(F32))).astype(o_ref.dtype)


def _lru(xr, gate, cw, cb, w2, ba, bx, lam, b, lp):
    rows = SCAN_ROWS
    return pl.pallas_call(
        _lru_body,
        out_shape=jax.ShapeDtypeStruct((b, lp, D_LRU), BF16),
        grid=(b, lp // rows),
        in_specs=[pl.BlockSpec((1, rows, D_LRU), lambda bi, ti: (bi, ti, 0)),
                  pl.BlockSpec((1, rows, D_LRU), lambda bi, ti: (bi, ti, 0)),
                  _const_spec((LRU_CONV, D_LRU)),
                  _const_spec((1, D_LRU)),
                  _const_spec((D_LRU // LANES, LANES, 2 * LANES)),
                  _const_spec((1, D_LRU)),
                  _const_spec((1, D_LRU)),
                  _const_spec((1, D_LRU))],
        out_specs=pl.BlockSpec((1, rows, D_LRU), lambda bi, ti: (bi, ti, 0)),
        scratch_shapes=[pltpu.VMEM((SUBLANES + rows, D_LRU), F32),
                        pltpu.VMEM((SUBLANES, D_LRU), F32)],
        compiler_params=_cparams(("parallel", "arbitrary")),
        name="rglru",
    )(xr, gate, cw, cb, w2, ba, bx, lam)


def _merge_body(h_ref, ya_ref, yb_ref, yc_ref, m_ref, wa_ref, wb_ref, wc_ref, wo_ref, o_ref):
    gate = lambda i: _sigmoid(m_ref[:, i * D_MODEL:(i + 1) * D_MODEL].astype(F32))
    mixed = gate(0) * _dot(ya_ref[...], wa_ref[...])
    mixed = mixed + gate(1) * _dot(yb_ref[...], wb_ref[...])
    mixed = mixed + gate(2) * _dot(yc_ref[...], wc_ref[...])
    o_ref[...] = h_ref[...] + _dot(mixed.astype(BF16), wo_ref[...])


def _merge(h, ya, yb, yc, m, wa, wb, wc, wo):
    t = h.shape[0]
    tm = MERGE_ROWS
    row_spec = lambda width: pl.BlockSpec((tm, width), lambda i: (i, 0))
    return pl.pallas_call(
        _merge_body,
        out_shape=jax.ShapeDtypeStruct((t, D_MODEL), F32),
        grid=(t // tm,),
        in_specs=[row_spec(D_MODEL), row_spec(D_ATTN), row_spec(D_SSD), row_spec(D_LRU),
                  row_spec(N_BRANCH * D_MODEL),
                  _const_spec((D_ATTN, D_MODEL)), _const_spec((D_SSD, D_MODEL)),
                  _const_spec((D_LRU, D_MODEL)), _const_spec((D_MODEL, D_MODEL))],
        out_specs=row_spec(D_MODEL),
        compiler_params=_cparams(("parallel",)),
        name="merge_out",
    )(h, ya, yb, yc, m, wa, wb, wc, wo)


def _prep_w_in(w_in):
    sizes = (D_ATTN, D_ATTN, D_ATTN, ATTN_HEADS, D_SSD, D_XBC, SSD_HEADS, D_LRU, D_LRU, N_BRANCH * D_MODEL)
    offs = [0]
    for s in sizes:
        offs.append(offs[-1] + s)
    part = lambda i: w_in[:, offs[i]:offs[i + 1]]
    q, k, v, f, z, xbc, dt, xr, gate, merge = (part(i) for i in range(10))
    pad = jnp.zeros((D_MODEL, FDT_COLS - ATTN_HEADS - SSD_HEADS), w_in.dtype)
    w = jnp.concatenate([k, z, xbc, xr, gate, merge, f, dt, pad], axis=1).astype(BF16)
    wt = jnp.concatenate([q.T, v.T], axis=0).astype(BF16)
    return w, wt


def _pad_lanes(vec, lane0):
    out = jnp.zeros((1, LANES), F32)
    return out.at[0, lane0:lane0 + vec.shape[0]].set(vec.astype(F32))


def _lru_gate_weights(w_a, w_x):
    def blockdiag_pairs(w):
        w = w.reshape(LRU_BLOCKS // 2, 2, LRU_BLOCK_DIM, LRU_BLOCK_DIM)
        zero = jnp.zeros_like(w[:, 0])
        top = jnp.concatenate([w[:, 0], zero], axis=2)
        bot = jnp.concatenate([zero, w[:, 1]], axis=2)
        return jnp.concatenate([top, bot], axis=1)
    return jnp.concatenate([blockdiag_pairs(w_a), blockdiag_pairs(w_x)], axis=2).astype(BF16)


def _head_expand():
    rows = jnp.arange(LANES)[:, None]
    cols = jnp.arange(D_SSD)[None, :]
    return (rows == DT_LANE0 + cols // SSD_HEAD_DIM).astype(BF16)


def kernel(x, meta_tokens, ffn1_norm, ffn1_w_gate_up, ffn1_w_down, mix_norm, w_in, fox_forget_bias,
           ssd_conv_w, ssd_conv_b, ssd_dt_bias, ssd_a_log, ssd_d, ssd_norm,
           lru_conv_w, lru_conv_b, lru_w_a, lru_b_a, lru_w_x, lru_b_x, lru_lambda,
           w_branch_attn, w_branch_ssd, w_branch_lru, w_out,
           ffn2_norm, ffn2_w_gate_up, ffn2_w_down, final_norm):
    b, s, d = x.shape
    depth = w_in.shape[0]
    length = N_META + s
    lp = -(-length // SEQ_ALIGN) * SEQ_ALIGN
    t = b * lp
    assert d == D_MODEL and t % FFN_ROWS == 0 and t % MERGE_ROWS == 0

    meta = jnp.broadcast_to(meta_tokens.astype(x.dtype)[None], (b, N_META, d))
    h = jnp.concatenate([meta, x, jnp.zeros((b, lp - length, d), x.dtype)], axis=1).reshape(t, d)

    row = lambda vec: vec.astype(F32).reshape(1, -1)
    expand = _head_expand()
    pk, pqt = _aux_constants()
    fg = row(final_norm)
    for l in range(depth):
        h = _ffn(h, row(ffn1_norm[l]), ffn1_w_gate_up[l, :, :D_FF].astype(BF16),
                 ffn1_w_gate_up[l, :, D_FF:].astype(BF16), ffn1_w_down[l].astype(BF16), fg, False)

        w_std, w_t = _prep_w_in(w_in[l])
        qt, ka, vt, z, xbc, xr, gate, merge, fdt = _inproj(
            h, row(mix_norm[l]), w_std, w_t, _pad_lanes(fox_forget_bias[l], 0), pk, pqt, b, lp)
        y_a = _attention(qt, ka, vt, b, lp).reshape(t, D_ATTN)
        y_b = _ssd(xbc, z, fdt, ssd_conv_w[l].astype(F32), row(ssd_conv_b[l]),
                   _pad_lanes(ssd_dt_bias[l], DT_LANE0), _pad_lanes(ssd_a_log[l], DT_LANE0),
                   row(jnp.repeat(ssd_d[l], SSD_HEAD_DIM)), row(ssd_norm[l]), expand, b, lp).reshape(t, D_SSD)
        y_c = _lru(xr, gate, lru_conv_w[l].astype(F32), row(lru_conv_b[l]),
                   _lru_gate_weights(lru_w_a[l], lru_w_x[l]), row(lru_b_a[l]), row(lru_b_x[l]),
                   row(lru_lambda[l]), b, lp).reshape(t, D_LRU)
        h = _merge(h, y_a, y_b, y_c, merge.reshape(t, N_BRANCH * D_MODEL), w_branch_attn[l].astype(BF16),
                   w_branch_ssd[l].astype(BF16), w_branch_lru[l].astype(BF16), w_out[l].astype(BF16))

        h = _ffn(h, row(ffn2_norm[l]), ffn2_w_gate_up[l, :, :D_FF].astype(BF16),
                 ffn2_w_gate_up[l, :, D_FF:].astype(BF16), ffn2_w_down[l].astype(BF16), fg,
                 l == depth - 1)
    return h.reshape(b, lp, d)[:, N_META:N_META + s]
```

```python
import functools

import jax
import jax.numpy as jnp
from jax import lax
from jax.experimental import pallas as pl
from jax.experimental.pallas import tpu as pltpu

F32 = jnp.float32
BF16 = jnp.bfloat16

D_MODEL = 1024
N_META = 16
SSD_CHUNK = 128
NORM_EPS = 1e-6
ATTN_HEADS = 16
ATTN_HEAD_DIM = 64
D_ATTN = ATTN_HEADS * ATTN_HEAD_DIM
SSD_HEAD_DIM = 64
D_SSD = D_MODEL
SSD_HEADS = D_SSD // SSD_HEAD_DIM
SSD_GROUPS = 2
SSD_STATE = 128
SSD_CONV = 4
D_XBC = D_SSD + 2 * SSD_GROUPS * SSD_STATE
D_LRU = D_MODEL
LRU_BLOCKS = 16
LRU_BLOCK_DIM = D_LRU // LRU_BLOCKS
LRU_CONV = 4
LRU_C = 8.0
D_FF = 2816
N_BRANCH = 3

LANES = 128
SUBLANES = 8
VMEM_LIMIT_BYTES = 56 * 1024 * 1024

SEQ_ALIGN = 256
FFN_ROWS = 1024
FFN_CHUNK = 256
PROJ_ROWS = 256
PROJ_CHUNK = 512
ATTN_TQ = 256
ATTN_HPS = 16
MERGE_ROWS = 512
SCAN_ROWS = 256
SSD_ROWS = 256
FDT_COLS = LANES
DT_LANE0 = ATTN_HEADS
NEG_BIG = -1e30
LOG2E = 1.4426950408889634
Q_SCALE = ATTN_HEAD_DIM ** -0.5 * LOG2E
AUX_PARTS = 3


def _cparams(sem):
    return pltpu.CompilerParams(dimension_semantics=sem, vmem_limit_bytes=VMEM_LIMIT_BYTES)


def _const_spec(shape):
    nd = len(shape)
    return pl.BlockSpec(shape, lambda *_: (0,) * nd, pipeline_mode=pl.Buffered(1))


def _rms(x, g):
    ms = jnp.mean(x * x, axis=-1, keepdims=True)
    return (x * lax.rsqrt(ms + NORM_EPS)) * g


def _dot(a, b):
    return jnp.dot(a, b, preferred_element_type=F32)


def _dot_nt(a, b):
    return lax.dot_general(a, b, (((1,), (1,)), ((), ())), preferred_element_type=F32)


def _split3(x):
    hi = x.astype(BF16)
    r1 = x - hi.astype(F32)
    mid = r1.astype(BF16)
    lo = (r1 - mid.astype(F32)).astype(BF16)
    return hi, mid, lo


def _dot_01_lhs(sel, x):
    hi, mid, lo = _split3(x)
    return _dot(sel, hi) + _dot(sel, mid) + _dot(sel, lo)


def _dot_01_rhs(x, sel):
    hi, mid, lo = _split3(x)
    return _dot(hi, sel) + _dot(mid, sel) + _dot(lo, sel)


def _log_sigmoid(x):
    return -(jnp.maximum(-x, 0.0) + jnp.log1p(jnp.exp(-jnp.abs(x))))


def _softplus(x):
    return jnp.maximum(x, 0.0) + jnp.log1p(jnp.exp(-jnp.abs(x)))


def _sigmoid(x):
    return 1.0 / (1.0 + jnp.exp(-x))


def _ffn_body(x_ref, g_ref, wg_ref, wu_ref, wd_ref, fg_ref, o_ref, a_scr, *, final_norm):
    x = x_ref[...]
    hn = _rms(x, g_ref[...]).astype(BF16)
    for c0 in range(0, D_FF, FFN_CHUNK):
        gate = _dot(hn, wg_ref[:, c0:c0 + FFN_CHUNK])
        up = _dot(hn, wu_ref[:, c0:c0 + FFN_CHUNK])
        a_scr[:, c0:c0 + FFN_CHUNK] = ((gate * _sigmoid(gate)) * up).astype(BF16)
    y = x + 0.5 * _dot(a_scr[...], wd_ref[...])
    if final_norm:
        y = _rms(y, fg_ref[...])
    o_ref[...] = y


def _ffn(x, g, wg, wu, wd, fg, final_norm):
    t = x.shape[0]
    tm = FFN_ROWS
    return pl.pallas_call(
        functools.partial(_ffn_body, final_norm=final_norm),
        out_shape=jax.ShapeDtypeStruct((t, D_MODEL), F32),
        grid=(t // tm,),
        in_specs=[pl.BlockSpec((tm, D_MODEL), lambda i: (i, 0))] + _ffn_weight_specs(),
        out_specs=pl.BlockSpec((tm, D_MODEL), lambda i: (i, 0)),
        scratch_shapes=[pltpu.VMEM((tm, D_FF), BF16)],
        compiler_params=_cparams(("parallel",)),
        name="ffn",
    )(x, g, wg, wu, wd, fg)


def _ffn_weight_specs():
    return [_const_spec((1, D_MODEL)), _const_spec((D_MODEL, D_FF)), _const_spec((D_MODEL, D_FF)),
            _const_spec((D_FF, D_MODEL)), _const_spec((1, D_MODEL))]


def _ffn_final_body(x_ref, g_ref, wg_ref, wu_ref, wd_ref, fg_ref, o_ref, a_scr):
    _ffn_body(x_ref.at[0], g_ref, wg_ref, wu_ref, wd_ref, fg_ref, o_ref.at[0], a_scr, final_norm=True)


def _ffn_final(x, g, wg, wu, wd, fg, b, lp, s_out):
    tm = FFN_ROWS
    return pl.pallas_call(
        _ffn_final_body,
        out_shape=jax.ShapeDtypeStruct((b, s_out, D_MODEL), F32),
        grid=(b, s_out // tm),
        in_specs=[pl.BlockSpec((pl.Element(1), pl.Element(tm), pl.Element(D_MODEL)),
                               lambda bi, i: (bi, pl.multiple_of(N_META + i * tm, SUBLANES), 0))]
        + _ffn_weight_specs(),
        out_specs=pl.BlockSpec((1, tm, D_MODEL), lambda bi, i: (bi, i, 0)),
        scratch_shapes=[pltpu.VMEM((tm, D_FF), BF16)],
        compiler_params=_cparams(("parallel", "parallel")),
        name="ffn_final",
    )(x.reshape(b, lp, D_MODEL), g, wg, wu, wd, fg)


_PLAIN_GROUPS = (("z", D_SSD), ("xbc", D_XBC), ("xr", D_LRU), ("gate", D_LRU), ("merge", N_BRANCH * D_MODEL))
OFF_K = 0
OFF_PLAIN = D_ATTN
OFF_FDT = OFF_PLAIN + sum(w for _, w in _PLAIN_GROUPS)
N_PROJ = OFF_FDT + FDT_COLS
D_AUG = ATTN_HEADS * LANES
ONE_LANE = LANES - 1


def _inproj_body(x_ref, g_ref, w_ref, wt_ref, fb_ref, pk_ref, pqt_ref, qt_ref, ka_ref, vt_ref,
                 z_ref, xbc_ref, xr_ref, gate_ref, merge_ref, fdt_ref, carry_scr):
    tm = x_ref.shape[1]

    @pl.when(pl.program_id(1) == 0)
    def _():
        carry_scr[...] = jnp.zeros(carry_scr.shape, F32)

    hn = _rms(x_ref[0], g_ref[...]).astype(BF16)

    def mm(c0, width):
        return _dot(hn, w_ref[:, c0:c0 + width])

    fdt = mm(OFF_FDT, FDT_COLS)
    fdt_ref[0] = fdt
    lane = lax.broadcasted_iota(jnp.int32, (tm, LANES), 1)
    lf = jnp.where(lane < ATTN_HEADS, _log_sigmoid(fdt + fb_ref[...]), 0.0)
    row = lax.broadcasted_iota(jnp.int32, (tm, tm), 0)
    col = lax.broadcasted_iota(jnp.int32, (tm, tm), 1)
    c = _dot_01_lhs((row >= col).astype(BF16), lf) + carry_scr[0:1, :]
    carry_scr[0:1, :] = c[tm - 1:tm, :]
    hi, mid, lo = (part.astype(F32) for part in _split3(c * LOG2E))
    cparts = (hi + pltpu.roll(mid, ATTN_HEADS, 1) + pltpu.roll(lo, 2 * ATTN_HEADS, 1)
              + jnp.where(lane == ONE_LANE, 1.0, 0.0))

    cparts_b = cparts.astype(BF16)
    first = lane < ATTN_HEAD_DIM
    for c0 in range(0, D_ATTN, PROJ_CHUNK):
        kv = mm(OFF_K + c0, PROJ_CHUNK)
        for pr in range(PROJ_CHUNK // LANES):
            pair = c0 // LANES + pr
            k_aux = _dot(cparts_b, pk_ref[:, pair * 2 * LANES:(pair + 1) * 2 * LANES])
            k_data = kv[:, pr * LANES:(pr + 1) * LANES]
            ka_ref[0, :, 2 * pair * LANES:(2 * pair + 1) * LANES] = jnp.where(
                first, k_data, k_aux[:, :LANES]).astype(ka_ref.dtype)
            ka_ref[0, :, (2 * pair + 1) * LANES:(2 * pair + 2) * LANES] = jnp.where(
                first, k_aux[:, LANES:], k_data).astype(ka_ref.dtype)

    hd = ATTN_HEAD_DIM
    q_t = _dot_nt(wt_ref[0:D_ATTN, :], hn) * Q_SCALE
    v_t = _dot_nt(wt_ref[D_ATTN:2 * D_ATTN, :], hn)
    aux_t = _dot(pqt_ref[...], cparts.T.astype(BF16))
    ones = jnp.ones((hd, tm), vt_ref.dtype)
    for h in range(ATTN_HEADS):
        data0 = h * LANES + (0 if h % 2 == 0 else hd)
        aux0 = h * LANES + (hd if h % 2 == 0 else 0)
        qt_ref[0, data0:data0 + hd, :] = q_t[h * hd:(h + 1) * hd, :].astype(qt_ref.dtype)
        qt_ref[0, aux0:aux0 + hd, :] = aux_t[h * hd:(h + 1) * hd, :].astype(qt_ref.dtype)
        vt_ref[0, h * LANES:h * LANES + hd, :] = v_t[h * hd:(h + 1) * hd, :].astype(vt_ref.dtype)
        vt_ref[0, h * LANES + hd:(h + 1) * LANES, :] = ones

    off = OFF_PLAIN
    for (_, width), o_ref in zip(_PLAIN_GROUPS, (z_ref, xbc_ref, xr_ref, gate_ref, merge_ref)):
        for c0 in range(0, width, PROJ_CHUNK):
            o_ref[0, :, c0:c0 + PROJ_CHUNK] = mm(off + c0, PROJ_CHUNK).astype(o_ref.dtype)
        off += width


def _inproj(x, g, w, wt, fb, pk, pqt, b, lp):
    tm = PROJ_ROWS
    row_spec = lambda width: pl.BlockSpec((1, tm, width), lambda bi, ti: (bi, ti, 0))
    col_spec = pl.BlockSpec((1, D_AUG, tm), lambda bi, ti: (bi, 0, ti))
    plain = [w_ for _, w_ in _PLAIN_GROUPS]
    out_shape = ([jax.ShapeDtypeStruct((b, D_AUG, lp), BF16), jax.ShapeDtypeStruct((b, lp, D_AUG), BF16),
                  jax.ShapeDtypeStruct((b, D_AUG, lp), BF16)]
                 + [jax.ShapeDtypeStruct((b, lp, w_), BF16) for w_ in plain]
                 + [jax.ShapeDtypeStruct((b, lp, FDT_COLS), F32)])
    return pl.pallas_call(
        _inproj_body,
        out_shape=out_shape,
        grid=(b, lp // tm),
        in_specs=[row_spec(D_MODEL),
                  _const_spec((1, D_MODEL)),
                  _const_spec((D_MODEL, N_PROJ)),
                  _const_spec((2 * D_ATTN, D_MODEL)),
                  _const_spec((1, LANES)),
                  _const_spec((LANES, D_AUG)),
                  _const_spec((ATTN_HEADS * ATTN_HEAD_DIM, LANES))],
        out_specs=[col_spec, row_spec(D_AUG), col_spec] + [row_spec(w_) for w_ in plain] + [row_spec(FDT_COLS)],
        scratch_shapes=[pltpu.VMEM((SUBLANES, LANES), F32)],
        compiler_params=_cparams(("parallel", "arbitrary")),
        name="inproj",
    )(x.reshape(b, lp, D_MODEL), g, w, wt, fb, pk, pqt)


def _aux_constants():
    hd = ATTN_HEAD_DIM
    src = jnp.arange(LANES)[:, None]
    col = jnp.arange(D_AUG)[None, :]
    head = col // LANES
    pos = col % LANES - jnp.where(head % 2 == 0, hd, 0)
    part = src // ATTN_HEADS
    minus = (part < AUX_PARTS) & (src % ATTN_HEADS == head) & (pos == part + AUX_PARTS)
    plus = (src == ONE_LANE) & (pos >= 0) & (pos < AUX_PARTS)
    pk = plus.astype(F32) - minus.astype(F32)
    r = jnp.arange(ATTN_HEADS * hd)[:, None]
    src = jnp.arange(LANES)[None, :]
    part = src // ATTN_HEADS
    cpart = (part < AUX_PARTS) & (src % ATTN_HEADS == r // hd) & (r % hd == part)
    one = (src == ONE_LANE) & (r % hd >= AUX_PARTS) & (r % hd < 2 * AUX_PARTS)
    pqt = cpart.astype(F32) + one.astype(F32)
    return pk.astype(BF16), pqt.astype(BF16)


def _attn_body(qt_ref, qn_ref, ka_ref, vt_ref, mb_ref, o_ref, m_scr, acc_scr, s_scr, *, tq, hps):
    iq = pl.program_id(2)
    m_scr[...] = jnp.full(m_scr.shape, NEG_BIG, F32)
    acc_scr[...] = jnp.zeros(acc_scr.shape, F32)
    hd = ATTN_HEAD_DIM

    def scores(j, h):
        ks = pl.multiple_of(j * tq, tq)
        ka = ka_ref[0, pl.ds(ks, tq), h * LANES:(h + 1) * LANES]
        return _dot(ka, qt_ref[0, h * LANES:(h + 1) * LANES, :])

    def softmax_pv(j, h, st):
        ks = pl.multiple_of(j * tq, tq)
        vt = vt_ref[0, h * LANES:(h + 1) * LANES, pl.ds(ks, tq)]
        m_prev = m_scr[h]
        m_new = jnp.maximum(m_prev, jnp.max(st, axis=0, keepdims=True))
        alpha = jnp.exp2(m_prev - m_new)
        pt = jnp.exp2((st - m_new).astype(vt.dtype))
        acc_scr[h] = alpha * acc_scr[h] + _dot(vt, pt)
        m_scr[h] = m_new

    @pl.when(iq == 0)
    def _():
        for h in range(hps):
            s_scr[h] = scores(0, h)

    def loop_body(j, carry):
        for h in range(hps):
            s_next = scores(j + 1, h)
            softmax_pv(j, h, s_scr[h])
            s_scr[h] = s_next
        return carry

    lax.fori_loop(0, iq, loop_body, 0)
    for h in range(hps):
        s_next = _dot(ka_ref[0, 0:tq, h * LANES:(h + 1) * LANES], qn_ref[0, h * LANES:(h + 1) * LANES, :])
        softmax_pv(iq, h, s_scr[h] + mb_ref[...])
        s_scr[h] = s_next

    for hp in range(hps // 2):
        outs = []
        for e in range(2):
            acc = acc_scr[2 * hp + e]
            outs.append(acc[0:hd, :] / acc[hd:hd + 1, :])
        o_ref[0, :, hp * LANES:(hp + 1) * LANES] = jnp.concatenate(outs, axis=0).T.astype(o_ref.dtype)


def _attention(qt, ka, vt, b, lp):
    tq, hps = ATTN_TQ, ATTN_HPS
    key_i = jnp.arange(tq)[:, None]
    query_i = jnp.arange(tq)[None, :]
    causal_bias = jnp.where(key_i <= query_i, 0.0, NEG_BIG).astype(F32)
    last_tile = lp // tq - 1
    return pl.pallas_call(
        functools.partial(_attn_body, tq=tq, hps=hps),
        out_shape=jax.ShapeDtypeStruct((b, lp, D_ATTN), BF16),
        grid=(b, ATTN_HEADS // hps, lp // tq),
        in_specs=[pl.BlockSpec((1, hps * LANES, tq), lambda bi, hi, qi: (bi, hi, qi)),
                  pl.BlockSpec((1, hps * LANES, tq), lambda bi, hi, qi: (bi, hi, jnp.minimum(qi + 1, last_tile))),
                  pl.BlockSpec((1, lp, hps * LANES), lambda bi, hi, qi: (bi, 0, hi), pipeline_mode=pl.Buffered(1)),
                  pl.BlockSpec((1, hps * LANES, lp), lambda bi, hi, qi: (bi, hi, 0), pipeline_mode=pl.Buffered(1)),
                  pl.BlockSpec((tq, tq), lambda bi, hi, qi: (0, 0))],
        out_specs=pl.BlockSpec((1, tq, hps * ATTN_HEAD_DIM), lambda bi, hi, qi: (bi, qi, hi)),
        scratch_shapes=[pltpu.VMEM((hps, 1, tq), F32),
                        pltpu.VMEM((hps, LANES, tq), F32),
                        pltpu.VMEM((hps, tq, tq), F32)],
        compiler_params=_cparams(("parallel", "parallel", "arbitrary")),
        name="fox_attention",
    )(qt, qt, ka, vt, causal_bias)


def _causal_conv(x, xp_scr, w_ref, b_ref, first_tile, rows, taps):
    @pl.when(first_tile)
    def _():
        xp_scr[0:SUBLANES, :] = jnp.zeros((SUBLANES, x.shape[1]), F32)

    xp_scr[SUBLANES:SUBLANES + rows, :] = x
    y = b_ref[...] + w_ref[taps - 1:taps, :] * x
    for kk in range(taps - 1):
        r0 = SUBLANES - (taps - 1) + kk
        y = y + w_ref[kk:kk + 1, :] * xp_scr[r0:r0 + rows, :]
    xp_scr[0:SUBLANES, :] = x[rows - SUBLANES:rows, :]
    return y


def _ssd_body(xbc_ref, z_ref, fdt_ref, cw_ref, cb_ref, dtb_ref, alog_ref, dfull_ref, nw_ref,
              exp_ref, o_ref, xp_scr, st_scr):
    rows = SSD_ROWS
    first_tile = pl.program_id(1) == 0

    @pl.when(first_tile)
    def _():
        st_scr[...] = jnp.zeros(st_scr.shape, F32)

    y = _causal_conv(xbc_ref[0].astype(F32), xp_scr, cw_ref, cb_ref, first_tile, rows, SSD_CONV)
    xc = y * _sigmoid(y)
    for ci in range(rows // SSD_CHUNK):
        rs = slice(ci * SSD_CHUNK, (ci + 1) * SSD_CHUNK)
        o_ref[0, rs, :] = _ssd_chunk(xc[rs], z_ref[0, rs, :].astype(F32), fdt_ref[0, rs, :], dtb_ref, alog_ref,
                                     dfull_ref, nw_ref, exp_ref, st_scr).astype(o_ref.dtype)


def _ssd_chunk(xc, z, dt_raw, dtb_ref, alog_ref, dfull_ref, nw_ref, exp_ref, st_scr):
    q = SSD_CHUNK
    gs = D_SSD // SSD_GROUPS
    heads_per_group = SSD_HEADS // SSD_GROUPS

    lane = lax.broadcasted_iota(jnp.int32, (q, LANES), 1)
    dt_lane = (lane >= DT_LANE0) & (lane < DT_LANE0 + SSD_HEADS)
    dt = jnp.where(dt_lane, _softplus(dt_raw + dtb_ref[...]), 0.0)
    a = -jnp.exp(alog_ref[...])
    da = dt * a
    row = lax.broadcasted_iota(jnp.int32, (q, q), 0)
    col = lax.broadcasted_iota(jnp.int32, (q, q), 1)
    lower = row >= col
    a_cum = _dot_01_lhs(lower.astype(BF16), da)
    a_cum_t = a_cum.T
    expand = exp_ref[...]
    dt_full = _dot_01_rhs(dt, expand)
    a_cum_full = _dot_01_rhs(a_cum, expand)
    a_last_full = a_cum_full[q - 1:q, :]
    decay_to_end = jnp.exp(a_last_full - a_cum_full)
    decay_from_start = jnp.exp(a_cum_full)
    chunk_decay = jnp.exp(a_last_full)

    xs = xc[:, :D_SSD]
    xdt = xs * dt_full
    xdt_b = xdt.astype(BF16)
    xde_b = (xdt * decay_to_end).astype(BF16)
    half = lax.broadcasted_iota(jnp.int32, (q, LANES), 1) < SSD_HEAD_DIM

    outs = []
    for g in range(SSD_GROUPS):
        bm = xc[:, D_SSD + g * SSD_STATE:D_SSD + (g + 1) * SSD_STATE]
        cm = xc[:, D_SSD + SSD_GROUPS * SSD_STATE + g * SSD_STATE:
                D_SSD + SSD_GROUPS * SSD_STATE + (g + 1) * SSD_STATE]
        bm_b = bm.astype(BF16)
        cm_b = cm.astype(BF16)
        cb = _dot_nt(cm_b, bm_b)
        y_pairs = []
        for pair in range(heads_per_group // 2):
            c0 = g * gs + pair * LANES
            xp = xdt_b[:, c0:c0 + LANES]
            ys = []
            for e in range(2):
                hl = DT_LANE0 + g * heads_per_group + 2 * pair + e
                seg = a_cum[:, hl:hl + 1] - a_cum_t[hl:hl + 1, :]
                dec = jnp.exp(jnp.where(lower, seg, -jnp.inf))
                ys.append(_dot((cb * dec).astype(BF16), xp))
            y_pairs.append(jnp.where(half, ys[0], ys[1]))
        y_diag = jnp.concatenate(y_pairs, axis=1)
        sl = slice(g * gs, (g + 1) * gs)
        prev = st_scr[g]
        y_off = _dot(cm_b, prev.astype(BF16)) * decay_from_start[:, sl]
        st_scr[g] = prev * chunk_decay[:, sl] + _dot(bm.T.astype(BF16), xde_b[:, sl])
        yg = y_diag + y_off + dfull_ref[:, sl] * xs[:, sl]
        zg = z[:, sl]
        yg = yg * (zg * _sigmoid(zg))
        yg = yg * lax.rsqrt(jnp.mean(yg * yg, axis=-1, keepdims=True) + NORM_EPS)
        outs.append(yg * nw_ref[:, sl])
    return jnp.concatenate(outs, axis=1)


def _ssd(xbc, z, fdt, cw, cb, dtb, alog, dfull, nw, expand, b, lp):
    q = SSD_ROWS
    return pl.pallas_call(
        _ssd_body,
        out_shape=jax.ShapeDtypeStruct((b, lp, D_SSD), BF16),
        grid=(b, lp // q),
        in_specs=[pl.BlockSpec((1, q, D_XBC), lambda bi, ci: (bi, ci, 0)),
                  pl.BlockSpec((1, q, D_SSD), lambda bi, ci: (bi, ci, 0)),
                  pl.BlockSpec((1, q, FDT_COLS), lambda bi, ci: (bi, ci, 0)),
                  _const_spec((SSD_CONV, D_XBC)),
                  _const_spec((1, D_XBC)),
                  _const_spec((1, LANES)),
                  _const_spec((1, LANES)),
                  _const_spec((1, D_SSD)),
                  _const_spec((1, D_SSD)),
                  _const_spec((LANES, D_SSD))],
        out_specs=pl.BlockSpec((1, q, D_SSD), lambda bi, ci: (bi, ci, 0)),
        scratch_shapes=[pltpu.VMEM((SUBLANES + q, D_XBC), F32),
                        pltpu.VMEM((SSD_GROUPS, SSD_STATE, D_SSD // SSD_GROUPS), F32)],
        compiler_params=_cparams(("parallel", "arbitrary")),
        name="ssd",
    )(xbc, z, fdt, cw, cb, dtb, alog, dfull, nw, expand)


def _lru_body(xr_ref, gate_ref, cw_ref, cb_ref, w2_ref, ba_ref, bx_ref, lam_ref, o_ref,
              xp_scr, h_scr):
    rows = SCAN_ROWS
    first_tile = pl.program_id(1) == 0

    @pl.when(first_tile)
    def _():
        h_scr[...] = jnp.zeros(h_scr.shape, F32)

    xc = _causal_conv(xr_ref[0].astype(F32), xp_scr, cw_ref, cb_ref, first_tile, rows, LRU_CONV)
    xc_b = xc.astype(BF16)
    pre = [_dot(xc_b[:, j * LANES:(j + 1) * LANES], w2_ref[j]) for j in range(D_LRU // LANES)]
    pre_a = jnp.concatenate([p[:, :LANES] for p in pre], axis=1)
    pre_x = jnp.concatenate([p[:, LANES:] for p in pre], axis=1)
    r = _sigmoid(pre_a + ba_ref[...])
    i = _sigmoid(pre_x + bx_ref[...])
    log_a = LRU_C * r * _log_sigmoid(lam_ref[...])
    a = jnp.exp(log_a)
    mult = jnp.sqrt(-jnp.tanh(log_a) * (a * a + 1.0))
    row0 = lax.broadcasted_iota(jnp.int32, (SUBLANES, D_LRU), 0) == 0
    mult = jnp.concatenate([jnp.where(first_tile & row0, 1.0, mult[:SUBLANES]), mult[SUBLANES:]], axis=0)
    u = mult * (i * xc)

    groups = rows // SUBLANES
    a3 = a.reshape(groups, SUBLANES, D_LRU)
    u3 = u.reshape(groups, SUBLANES, D_LRU)
    sub = lax.broadcasted_iota(jnp.int32, (groups, SUBLANES, D_LRU), 1)
    d = 1
    while d < SUBLANES:
        keep = sub >= d
        a_s = jnp.where(keep, pltpu.roll(a3, d, 1), 1.0)
        u_s = jnp.where(keep, pltpu.roll(u3, d, 1), 0.0)
        u3 = a3 * u_s + u3
        a3 = a3 * a_s
        d *= 2
    h_prev = h_scr[0:1, :]
    hs = []
    for r in range(groups):
        h_r = a3[r] * h_prev + u3[r]
        hs.append(h_r)
        h_prev = h_r[SUBLANES - 1:SUBLANES, :]
    h = jnp.concatenate(hs, axis=0)
    h_scr[0:1, :] = h[rows - 1:rows, :]
    o_ref[0] = (h * jax.nn.gelu(gate_ref[0].astype(F32))).astype(o_ref.dtype)


def _lru(xr, gate, cw, cb, w2, ba, bx, lam, b, lp):
    rows = SCAN_ROWS
    return pl.pallas_call(
        _lru_body,
        out_shape=jax.ShapeDtypeStruct((b, lp, D_LRU), BF16),
        grid=(b, lp // rows),
        in_specs=[pl.BlockSpec((1, rows, D_LRU), lambda bi, ti: (bi, ti, 0)),
                  pl.BlockSpec((1, rows, D_LRU), lambda bi, ti: (bi, ti, 0)),
                  _const_spec((LRU_CONV, D_LRU)),
                  _const_spec((1, D_LRU)),
                  _const_spec((D_LRU // LANES, LANES, 2 * LANES)),
                  _const_spec((1, D_LRU)),
                  _const_spec((1, D_LRU)),
                  _const_spec((1, D_LRU))],
        out_specs=pl.BlockSpec((1, rows, D_LRU), lambda bi, ti: (bi, ti, 0)),
        scratch_shapes=[pltpu.VMEM((SUBLANES + rows, D_LRU), F32),
                        pltpu.VMEM((SUBLANES, D_LRU), F32)],
        compiler_params=_cparams(("parallel", "arbitrary")),
        name="rglru",
    )(xr, gate, cw, cb, w2, ba, bx, lam)


def _merge_body(h_ref, ya_ref, yb_ref, yc_ref, m_ref, wa_ref, wb_ref, wc_ref, wo_ref, o_ref):
    gate = lambda i: _sigmoid(m_ref[:, i * D_MODEL:(i + 1) * D_MODEL].astype(F32))
    mixed = gate(0) * _dot(ya_ref[...], wa_ref[...])
    mixed = mixed + gate(1) * _dot(yb_ref[...], wb_ref[...])
    mixed = mixed + gate(2) * _dot(yc_ref[...], wc_ref[...])
    o_ref[...] = h_ref[...] + _dot(mixed.astype(BF16), wo_ref[...])


def _merge(h, ya, yb, yc, m, wa, wb, wc, wo):
    t = h.shape[0]
    tm = MERGE_ROWS
    row_spec = lambda width: pl.BlockSpec((tm, width), lambda i: (i, 0))
    return pl.pallas_call(
        _merge_body,
        out_shape=jax.ShapeDtypeStruct((t, D_MODEL), F32),
        grid=(t // tm,),
        in_specs=[row_spec(D_MODEL), row_spec(D_ATTN), row_spec(D_SSD), row_spec(D_LRU),
                  row_spec(N_BRANCH * D_MODEL),
                  _const_spec((D_ATTN, D_MODEL)), _const_spec((D_SSD, D_MODEL)),
                  _const_spec((D_LRU, D_MODEL)), _const_spec((D_MODEL, D_MODEL))],
        out_specs=row_spec(D_MODEL),
        compiler_params=_cparams(("parallel",)),
        name="merge_out",
    )(h, ya, yb, yc, m, wa, wb, wc, wo)


def _prep_w_in(w_in):
    sizes = (D_ATTN, D_ATTN, D_ATTN, ATTN_HEADS, D_SSD, D_XBC, SSD_HEADS, D_LRU, D_LRU, N_BRANCH * D_MODEL)
    offs = [0]
    for s in sizes:
        offs.append(offs[-1] + s)
    part = lambda i: w_in[:, offs[i]:offs[i + 1]]
    q, k, v, f, z, xbc, dt, xr, gate, merge = (part(i) for i in range(10))
    pad = jnp.zeros((D_MODEL, FDT_COLS - ATTN_HEADS - SSD_HEADS), w_in.dtype)
    w = jnp.concatenate([k, z, xbc, xr, gate, merge, f, dt, pad], axis=1).astype(BF16)
    wt = jnp.concatenate([q.T, v.T], axis=0).astype(BF16)
    return w, wt


def _pad_lanes(vec, lane0):
    out = jnp.zeros((1, LANES), F32)
    return out.at[0, lane0:lane0 + vec.shape[0]].set(vec.astype(F32))


def _lru_gate_weights(w_a, w_x):
    def blockdiag_pairs(w):
        w = w.reshape(LRU_BLOCKS // 2, 2, LRU_BLOCK_DIM, LRU_BLOCK_DIM)
        zero = jnp.zeros_like(w[:, 0])
        top = jnp.concatenate([w[:, 0], zero], axis=2)
        bot = jnp.concatenate([zero, w[:, 1]], axis=2)
        return jnp.concatenate([top, bot], axis=1)
    return jnp.concatenate([blockdiag_pairs(w_a), blockdiag_pairs(w_x)], axis=2).astype(BF16)


def _head_expand():
    rows = jnp.arange(LANES)[:, None]
    cols = jnp.arange(D_SSD)[None, :]
    return (rows == DT_LANE0 + cols // SSD_HEAD_DIM).astype(BF16)


def kernel(x, meta_tokens, ffn1_norm, ffn1_w_gate_up, ffn1_w_down, mix_norm, w_in, fox_forget_bias,
           ssd_conv_w, ssd_conv_b, ssd_dt_bias, ssd_a_log, ssd_d, ssd_norm,
           lru_conv_w, lru_conv_b, lru_w_a, lru_b_a, lru_w_x, lru_b_x, lru_lambda,
           w_branch_attn, w_branch_ssd, w_branch_lru, w_out,
           ffn2_norm, ffn2_w_gate_up, ffn2_w_down, final_norm):
    b, s, d = x.shape
    depth = w_in.shape[0]
    length = N_META + s
    lp = -(-length // SEQ_ALIGN) * SEQ_ALIGN
    t = b * lp
    assert d == D_MODEL and t % FFN_ROWS == 0 and t % MERGE_ROWS == 0 and s % FFN_ROWS == 0

    meta = jnp.broadcast_to(meta_tokens.astype(x.dtype)[None], (b, N_META, d))
    h = jnp.concatenate([meta, x, jnp.zeros((b, lp - length, d), x.dtype)], axis=1).reshape(t, d)

    row = lambda vec: vec.astype(F32).reshape(1, -1)
    expand = _head_expand()
    pk, pqt = _aux_constants()
    fg = row(final_norm)
    for l in range(depth):
        h = _ffn(h, row(ffn1_norm[l]), ffn1_w_gate_up[l, :, :D_FF].astype(BF16),
                 ffn1_w_gate_up[l, :, D_FF:].astype(BF16), ffn1_w_down[l].astype(BF16), fg, False)

        w_std, w_t = _prep_w_in(w_in[l])
        qt, ka, vt, z, xbc, xr, gate, merge, fdt = _inproj(
            h, row(mix_norm[l]), w_std, w_t, _pad_lanes(fox_forget_bias[l], 0), pk, pqt, b, lp)
        y_a = _attention(qt, ka, vt, b, lp).reshape(t, D_ATTN)
        y_b = _ssd(xbc, z, fdt, ssd_conv_w[l].astype(F32), row(ssd_conv_b[l]),
                   _pad_lanes(ssd_dt_bias[l], DT_LANE0), _pad_lanes(ssd_a_log[l], DT_LANE0),
                   row(jnp.repeat(ssd_d[l], SSD_HEAD_DIM)), row(ssd_norm[l]), expand, b, lp).reshape(t, D_SSD)
        y_c = _lru(xr, gate, lru_conv_w[l].astype(F32), row(lru_conv_b[l]),
                   _lru_gate_weights(lru_w_a[l], lru_w_x[l]), row(lru_b_a[l]), row(lru_b_x[l]),
                   row(lru_lambda[l]), b, lp).reshape(t, D_LRU)
        h = _merge(h, y_a, y_b, y_c, merge.reshape(t, N_BRANCH * D_MODEL), w_branch_attn[l].astype(BF16),
                   w_branch_ssd[l].astype(BF16), w_branch_lru[l].astype(BF16), w_out[l].astype(BF16))

        ffn2 = (row(ffn2_norm[l]), ffn2_w_gate_up[l, :, :D_FF].astype(BF16),
                ffn2_w_gate_up[l, :, D_FF:].astype(BF16), ffn2_w_down[l].astype(BF16), fg)
        if l < depth - 1:
            h = _ffn(h, *ffn2, False)
    return _ffn_final(h, *ffn2, b, lp, s)
```

```python
import functools

import jax
import jax.numpy as jnp
from jax import lax
from jax.experimental import pallas as pl
from jax.experimental.pallas import tpu as pltpu

F32 = jnp.float32
BF16 = jnp.bfloat16

D_MODEL = 1024
N_META = 16
SSD_CHUNK = 128
NORM_EPS = 1e-6
ATTN_HEADS = 16
ATTN_HEAD_DIM = 64
D_ATTN = ATTN_HEADS * ATTN_HEAD_DIM
SSD_HEAD_DIM = 64
D_SSD = D_MODEL
SSD_HEADS = D_SSD // SSD_HEAD_DIM
SSD_GROUPS = 2
SSD_STATE = 128
SSD_CONV = 4
D_XBC = D_SSD + 2 * SSD_GROUPS * SSD_STATE
D_LRU = D_MODEL
LRU_BLOCKS = 16
LRU_BLOCK_DIM = D_LRU // LRU_BLOCKS
LRU_CONV = 4
LRU_C = 8.0
D_FF = 2816
N_BRANCH = 3

LANES = 128
SUBLANES = 8
VMEM_LIMIT_BYTES = 56 * 1024 * 1024

SEQ_ALIGN = 256
FFN_ROWS = 1024
FFN_CHUNK = 256
PROJ_ROWS = 256
PROJ_CHUNK = 512
ATTN_TQ = 256
ATTN_HPS = 16
MERGE_ROWS = 1024
SCAN_ROWS = 256
SSD_ROWS = 256
FDT_COLS = LANES
DT_LANE0 = ATTN_HEADS
NEG_BIG = -1e30
LOG2E = 1.4426950408889634
Q_SCALE = ATTN_HEAD_DIM ** -0.5 * LOG2E
AUX_PARTS = 3
AUX_SLOTS = 8


def _cparams(sem):
    return pltpu.CompilerParams(dimension_semantics=sem, vmem_limit_bytes=VMEM_LIMIT_BYTES)


def _const_spec(shape):
    nd = len(shape)
    return pl.BlockSpec(shape, lambda *_: (0,) * nd, pipeline_mode=pl.Buffered(1))


def _rms(x, g):
    ms = jnp.mean(x * x, axis=-1, keepdims=True)
    return (x * lax.rsqrt(ms + NORM_EPS)) * g


def _dot(a, b):
    return jnp.dot(a, b, preferred_element_type=F32)


def _dot_nt(a, b):
    return lax.dot_general(a, b, (((1,), (1,)), ((), ())), preferred_element_type=F32)


def _split3(x):
    hi = x.astype(BF16)
    r1 = x - hi.astype(F32)
    mid = r1.astype(BF16)
    lo = (r1 - mid.astype(F32)).astype(BF16)
    return hi, mid, lo


def _dot_01_lhs(sel, x):
    hi, mid, lo = _split3(x)
    return _dot(sel, hi) + _dot(sel, mid) + _dot(sel, lo)


def _dot_01_rhs(x, sel):
    hi, mid, lo = _split3(x)
    return _dot(hi, sel) + _dot(mid, sel) + _dot(lo, sel)


def _log_sigmoid(x):
    return -(jnp.maximum(-x, 0.0) + jnp.log1p(jnp.exp(-jnp.abs(x))))


def _softplus(x):
    return jnp.maximum(x, 0.0) + jnp.log1p(jnp.exp(-jnp.abs(x)))


def _sigmoid(x):
    return 1.0 / (1.0 + jnp.exp(-x))


def _ffn_body(x_ref, g_ref, wg_ref, wu_ref, wd_ref, fg_ref, o_ref, a_scr, *, final_norm):
    x = x_ref[...]
    hn = _rms(x, g_ref[...]).astype(BF16)
    for c0 in range(0, D_FF, FFN_CHUNK):
        gate = _dot(hn, wg_ref[:, c0:c0 + FFN_CHUNK])
        up = _dot(hn, wu_ref[:, c0:c0 + FFN_CHUNK])
        a_scr[:, c0:c0 + FFN_CHUNK] = ((gate * _sigmoid(gate)) * up).astype(BF16)
    y = x + 0.5 * _dot(a_scr[...], wd_ref[...])
    if final_norm:
        y = _rms(y, fg_ref[...])
    o_ref[...] = y


def _ffn(x, g, wg, wu, wd, fg, final_norm):
    t = x.shape[0]
    tm = FFN_ROWS
    return pl.pallas_call(
        functools.partial(_ffn_body, final_norm=final_norm),
        out_shape=jax.ShapeDtypeStruct((t, D_MODEL), F32),
        grid=(t // tm,),
        in_specs=[pl.BlockSpec((tm, D_MODEL), lambda i: (i, 0))] + _ffn_weight_specs(),
        out_specs=pl.BlockSpec((tm, D_MODEL), lambda i: (i, 0)),
        scratch_shapes=[pltpu.VMEM((tm, D_FF), BF16)],
        compiler_params=_cparams(("parallel",)),
        name="ffn",
    )(x, g, wg, wu, wd, fg)


def _ffn_weight_specs():
    return [_const_spec((1, D_MODEL)), _const_spec((D_MODEL, D_FF)), _const_spec((D_MODEL, D_FF)),
            _const_spec((D_FF, D_MODEL)), _const_spec((1, D_MODEL))]


def _ffn_final_body(x_ref, g_ref, wg_ref, wu_ref, wd_ref, fg_ref, o_ref, a_scr):
    _ffn_body(x_ref.at[0], g_ref, wg_ref, wu_ref, wd_ref, fg_ref, o_ref.at[0], a_scr, final_norm=True)


def _ffn_final(x, g, wg, wu, wd, fg, b, lp, s_out):
    tm = FFN_ROWS
    return pl.pallas_call(
        _ffn_final_body,
        out_shape=jax.ShapeDtypeStruct((b, s_out, D_MODEL), F32),
        grid=(b, s_out // tm),
        in_specs=[pl.BlockSpec((pl.Element(1), pl.Element(tm), pl.Element(D_MODEL)),
                               lambda bi, i: (bi, pl.multiple_of(N_META + i * tm, SUBLANES), 0))]
        + _ffn_weight_specs(),
        out_specs=pl.BlockSpec((1, tm, D_MODEL), lambda bi, i: (bi, i, 0)),
        scratch_shapes=[pltpu.VMEM((tm, D_FF), BF16)],
        compiler_params=_cparams(("parallel", "parallel")),
        name="ffn_final",
    )(x.reshape(b, lp, D_MODEL), g, wg, wu, wd, fg)


_PLAIN_GROUPS = (("z", D_SSD), ("xbc", D_XBC), ("xr", D_LRU), ("gate", D_LRU), ("merge", N_BRANCH * D_MODEL))
OFF_K = 0
OFF_PLAIN = D_ATTN
OFF_FDT = OFF_PLAIN + sum(w for _, w in _PLAIN_GROUPS)
N_PROJ = OFF_FDT + FDT_COLS
D_AUG = ATTN_HEADS * LANES
ONE_LANE = LANES - 1


def _inproj_body(x_ref, g_ref, w_ref, wt_ref, fb_ref, pw_ref, qt_ref, ka_ref, vt_ref,
                 z_ref, xbc_ref, xr_ref, gate_ref, merge_ref, fdt_ref, carry_scr):
    tm = x_ref.shape[1]

    @pl.when(pl.program_id(1) == 0)
    def _():
        carry_scr[...] = jnp.zeros(carry_scr.shape, F32)

    hn = _rms(x_ref[0], g_ref[...]).astype(BF16)

    def mm(c0, width):
        return _dot(hn, w_ref[:, c0:c0 + width])

    fdt = mm(OFF_FDT, FDT_COLS)
    fdt_ref[0] = fdt
    off = OFF_PLAIN
    for (_, width), o_ref in zip(_PLAIN_GROUPS, (z_ref, xbc_ref, xr_ref, gate_ref, merge_ref)):
        for c0 in range(0, width, PROJ_CHUNK):
            o_ref[0, :, c0:c0 + PROJ_CHUNK] = mm(off + c0, PROJ_CHUNK).astype(o_ref.dtype)
        off += width

    lane = lax.broadcasted_iota(jnp.int32, (tm, LANES), 1)
    lf = jnp.where(lane < ATTN_HEADS, _log_sigmoid(fdt + fb_ref[...]), 0.0)
    row = lax.broadcasted_iota(jnp.int32, (tm, tm), 0)
    col = lax.broadcasted_iota(jnp.int32, (tm, tm), 1)
    c = _dot_01_lhs((row >= col).astype(BF16), lf) + carry_scr[0:1, :]
    carry_scr[0:1, :] = c[tm - 1:tm, :]
    hi, mid, lo = (part.astype(F32) for part in _split3(c * LOG2E))
    cparts = (hi + pltpu.roll(mid, ATTN_HEADS, 1) + pltpu.roll(lo, 2 * ATTN_HEADS, 1)
              + jnp.where(lane == ONE_LANE, 1.0, 0.0))

    w_aux = _dot(cparts.astype(BF16), pw_ref[...])
    k_aux = w_aux[:, :LANES]
    q_aux_t = w_aux[:, LANES:].T

    first = lane < ATTN_HEAD_DIM
    for c0 in range(0, D_ATTN, PROJ_CHUNK):
        kv = mm(OFF_K + c0, PROJ_CHUNK)
        for pr in range(PROJ_CHUNK // LANES):
            pair = c0 // LANES + pr
            k_data = kv[:, pr * LANES:(pr + 1) * LANES]
            ka_ref[0, :, 2 * pair * LANES:(2 * pair + 1) * LANES] = jnp.where(
                first, k_data, k_aux).astype(ka_ref.dtype)
            ka_ref[0, :, (2 * pair + 1) * LANES:(2 * pair + 2) * LANES] = jnp.where(
                first, k_aux, k_data).astype(ka_ref.dtype)

    hd = ATTN_HEAD_DIM
    q_t = _dot_nt(wt_ref[0:D_ATTN, :], hn) * Q_SCALE
    v_t = _dot_nt(wt_ref[D_ATTN:2 * D_ATTN, :], hn)
    ones = jnp.ones((hd, tm), vt_ref.dtype)
    for h in range(ATTN_HEADS):
        even = h % 2 == 0
        data0 = h * LANES + (0 if even else hd)
        aux0 = h * LANES + (hd if even else 0)
        a0 = (hd if even else 0) + AUX_SLOTS * (h // 2)
        before, after = AUX_SLOTS * (h // 2), hd - AUX_SLOTS * (h // 2 + 1)
        pieces = ([jnp.zeros((before, tm), F32)] if before else []) + [q_aux_t[a0:a0 + AUX_SLOTS, :]]
        pieces += [jnp.zeros((after, tm), F32)] if after else []
        qt_ref[0, data0:data0 + hd, :] = q_t[h * hd:(h + 1) * hd, :].astype(qt_ref.dtype)
        qt_ref[0, aux0:aux0 + hd, :] = jnp.concatenate(pieces, axis=0).astype(qt_ref.dtype)
        vt_ref[0, h * LANES:h * LANES + hd, :] = v_t[h * hd:(h + 1) * hd, :].astype(vt_ref.dtype)
        vt_ref[0, h * LANES + hd:(h + 1) * LANES, :] = ones


def _inproj(x, g, w, wt, fb, pw, b, lp):
    tm = PROJ_ROWS
    row_spec = lambda width: pl.BlockSpec((1, tm, width), lambda bi, ti: (bi, ti, 0))
    col_spec = pl.BlockSpec((1, D_AUG, tm), lambda bi, ti: (bi, 0, ti))
    plain = [w_ for _, w_ in _PLAIN_GROUPS]
    out_shape = ([jax.ShapeDtypeStruct((b, D_AUG, lp), BF16), jax.ShapeDtypeStruct((b, lp, D_AUG), BF16),
                  jax.ShapeDtypeStruct((b, D_AUG, lp), BF16)]
                 + [jax.ShapeDtypeStruct((b, lp, w_), BF16) for w_ in plain]
                 + [jax.ShapeDtypeStruct((b, lp, FDT_COLS), F32)])
    return pl.pallas_call(
        _inproj_body,
        out_shape=out_shape,
        grid=(b, lp // tm),
        in_specs=[row_spec(D_MODEL),
                  _const_spec((1, D_MODEL)),
                  _const_spec((D_MODEL, N_PROJ)),
                  _const_spec((2 * D_ATTN, D_MODEL)),
                  _const_spec((1, LANES)),
                  _const_spec((LANES, 2 * LANES))],
        out_specs=[col_spec, row_spec(D_AUG), col_spec] + [row_spec(w_) for w_ in plain] + [row_spec(FDT_COLS)],
        scratch_shapes=[pltpu.VMEM((SUBLANES, LANES), F32)],
        compiler_params=_cparams(("parallel", "arbitrary")),
        name="inproj",
    )(x.reshape(b, lp, D_MODEL), g, w, wt, fb, pw)


def _aux_constants():
    src = jnp.arange(LANES)[:, None]
    lane = jnp.arange(LANES)[None, :]
    half = jnp.where(lane >= ATTN_HEAD_DIM, 0, 1)
    slot = lane % ATTN_HEAD_DIM
    head = 2 * (slot // AUX_SLOTS) + half
    idx = slot % AUX_SLOTS
    part = src // ATTN_HEADS
    part_src = (part < AUX_PARTS) & (src % ATTN_HEADS == head)
    one_src = src == ONE_LANE
    key = (one_src & (idx < AUX_PARTS)).astype(F32) - (part_src & (idx == part + AUX_PARTS)).astype(F32)
    qry = (part_src & (idx == part)).astype(F32) + (one_src & (idx >= AUX_PARTS) & (idx < 2 * AUX_PARTS)).astype(F32)
    return jnp.concatenate([key, qry], axis=1).astype(BF16)


def _attn_body(qt_ref, qn_ref, ka_ref, vt_ref, mb_ref, o_ref, m_scr, acc_scr, s_scr, *, tq, hps):
    iq = pl.program_id(2)
    m_scr[...] = jnp.full(m_scr.shape, NEG_BIG, F32)
    acc_scr[...] = jnp.zeros(acc_scr.shape, F32)
    hd = ATTN_HEAD_DIM

    def scores(j, h):
        ks = pl.multiple_of(j * tq, tq)
        ka = ka_ref[0, pl.ds(ks, tq), h * LANES:(h + 1) * LANES]
        return _dot(ka, qt_ref[0, h * LANES:(h + 1) * LANES, :])

    def softmax_pv(j, h, st):
        ks = pl.multiple_of(j * tq, tq)
        vt = vt_ref[0, h * LANES:(h + 1) * LANES, pl.ds(ks, tq)]
        m_prev = m_scr[h]
        m_new = jnp.maximum(m_prev, jnp.max(st, axis=0, keepdims=True))
        alpha = jnp.exp2(m_prev - m_new)
        pt = jnp.exp2((st - m_new).astype(vt.dtype))
        acc_scr[h] = alpha * acc_scr[h] + _dot(vt, pt)
        m_scr[h] = m_new

    @pl.when(iq == 0)
    def _():
        for h in range(hps):
            s_scr[h] = scores(0, h)

    def loop_body(j, carry):
        for h in range(hps):
            s_next = scores(j + 1, h)
            softmax_pv(j, h, s_scr[h])
            s_scr[h] = s_next
        return carry

    lax.fori_loop(0, iq, loop_body, 0)
    for h in range(hps):
        s_next = _dot(ka_ref[0, 0:tq, h * LANES:(h + 1) * LANES], qn_ref[0, h * LANES:(h + 1) * LANES, :])
        softmax_pv(iq, h, s_scr[h] + mb_ref[...])
        s_scr[h] = s_next

    for hp in range(hps // 2):
        outs = []
        for e in range(2):
            acc = acc_scr[2 * hp + e]
            outs.append(acc[0:hd, :] / acc[hd:hd + 1, :])
        o_ref[0, :, hp * LANES:(hp + 1) * LANES] = jnp.concatenate(outs, axis=0).T.astype(o_ref.dtype)


def _attention(qt, ka, vt, b, lp):
    tq, hps = ATTN_TQ, ATTN_HPS
    key_i = jnp.arange(tq)[:, None]
    query_i = jnp.arange(tq)[None, :]
    causal_bias = jnp.where(key_i <= query_i, 0.0, NEG_BIG).astype(F32)
    last_tile = lp // tq - 1
    return pl.pallas_call(
        functools.partial(_attn_body, tq=tq, hps=hps),
        out_shape=jax.ShapeDtypeStruct((b, lp, D_ATTN), BF16),
        grid=(b, ATTN_HEADS // hps, lp // tq),
        in_specs=[pl.BlockSpec((1, hps * LANES, tq), lambda bi, hi, qi: (bi, hi, qi)),
                  pl.BlockSpec((1, hps * LANES, tq), lambda bi, hi, qi: (bi, hi, jnp.minimum(qi + 1, last_tile))),
                  pl.BlockSpec((1, lp, hps * LANES), lambda bi, hi, qi: (bi, 0, hi), pipeline_mode=pl.Buffered(1)),
                  pl.BlockSpec((1, hps * LANES, lp), lambda bi, hi, qi: (bi, hi, 0), pipeline_mode=pl.Buffered(1)),
                  pl.BlockSpec((tq, tq), lambda bi, hi, qi: (0, 0))],
        out_specs=pl.BlockSpec((1, tq, hps * ATTN_HEAD_DIM), lambda bi, hi, qi: (bi, qi, hi)),
        scratch_shapes=[pltpu.VMEM((hps, 1, tq), F32),
                        pltpu.VMEM((hps, LANES, tq), F32),
                        pltpu.VMEM((hps, tq, tq), F32)],
        compiler_params=_cparams(("parallel", "parallel", "arbitrary")),
        name="fox_attention",
    )(qt, qt, ka, vt, causal_bias)


def _causal_conv(x, xp_scr, w_ref, b_ref, first_tile, rows, taps):
    @pl.when(first_tile)
    def _():
        xp_scr[0:SUBLANES, :] = jnp.zeros((SUBLANES, x.shape[1]), F32)

    xp_scr[SUBLANES:SUBLANES + rows, :] = x
    y = b_ref[...] + w_ref[taps - 1:taps, :] * x
    for kk in range(taps - 1):
        r0 = SUBLANES - (taps - 1) + kk
        y = y + w_ref[kk:kk + 1, :] * xp_scr[r0:r0 + rows, :]
    xp_scr[0:SUBLANES, :] = x[rows - SUBLANES:rows, :]
    return y


def _ssd_body(xbc_ref, z_ref, fdt_ref, cw_ref, cb_ref, dtb_ref, alog_ref, dfull_ref, nw_ref,
              exp_ref, o_ref, xp_scr, st_scr):
    rows = SSD_ROWS
    first_tile = pl.program_id(1) == 0

    @pl.when(first_tile)
    def _():
        st_scr[...] = jnp.zeros(st_scr.shape, F32)

    y = _causal_conv(xbc_ref[0].astype(F32), xp_scr, cw_ref, cb_ref, first_tile, rows, SSD_CONV)
    xc = y * _sigmoid(y)
    for ci in range(rows // SSD_CHUNK):
        rs = slice(ci * SSD_CHUNK, (ci + 1) * SSD_CHUNK)
        o_ref[0, rs, :] = _ssd_chunk(xc[rs], z_ref[0, rs, :].astype(F32), fdt_ref[0, rs, :], dtb_ref, alog_ref,
                                     dfull_ref, nw_ref, exp_ref, st_scr).astype(o_ref.dtype)


def _ssd_chunk(xc, z, dt_raw, dtb_ref, alog_ref, dfull_ref, nw_ref, exp_ref, st_scr):
    q = SSD_CHUNK
    gs = D_SSD // SSD_GROUPS
    heads_per_group = SSD_HEADS // SSD_GROUPS

    lane = lax.broadcasted_iota(jnp.int32, (q, LANES), 1)
    dt_lane = (lane >= DT_LANE0) & (lane < DT_LANE0 + SSD_HEADS)
    dt = jnp.where(dt_lane, _softplus(dt_raw + dtb_ref[...]), 0.0)
    a = -jnp.exp(alog_ref[...])
    da = dt * a
    row = lax.broadcasted_iota(jnp.int32, (q, q), 0)
    col = lax.broadcasted_iota(jnp.int32, (q, q), 1)
    lower = row >= col
    a_cum = _dot_01_lhs(lower.astype(BF16), da)
    a_cum_t = a_cum.T
    expand = exp_ref[...]
    dt_full = _dot_01_rhs(dt, expand)
    a_cum_full = _dot_01_rhs(a_cum, expand)
    a_last_full = a_cum_full[q - 1:q, :]
    decay_to_end = jnp.exp(a_last_full - a_cum_full)
    decay_from_start = jnp.exp(a_cum_full)
    chunk_decay = jnp.exp(a_last_full)

    xs = xc[:, :D_SSD]
    xdt = xs * dt_full
    xdt_b = xdt.astype(BF16)
    xde_b = (xdt * decay_to_end).astype(BF16)
    half = lax.broadcasted_iota(jnp.int32, (q, LANES), 1) < SSD_HEAD_DIM

    outs = []
    for g in range(SSD_GROUPS):
        bm = xc[:, D_SSD + g * SSD_STATE:D_SSD + (g + 1) * SSD_STATE]
        cm = xc[:, D_SSD + SSD_GROUPS * SSD_STATE + g * SSD_STATE:
                D_SSD + SSD_GROUPS * SSD_STATE + (g + 1) * SSD_STATE]
        bm_b = bm.astype(BF16)
        cm_b = cm.astype(BF16)
        cb = _dot_nt(cm_b, bm_b)
        y_pairs = []
        for pair in range(heads_per_group // 2):
            c0 = g * gs + pair * LANES
            xp = xdt_b[:, c0:c0 + LANES]
            ys = []
            for e in range(2):
                hl = DT_LANE0 + g * heads_per_group + 2 * pair + e
                seg = a_cum[:, hl:hl + 1] - a_cum_t[hl:hl + 1, :]
                dec = jnp.exp(jnp.where(lower, seg, -jnp.inf))
                ys.append(_dot((cb * dec).astype(BF16), xp))
            y_pairs.append(jnp.where(half, ys[0], ys[1]))
        y_diag = jnp.concatenate(y_pairs, axis=1)
        sl = slice(g * gs, (g + 1) * gs)
        prev = st_scr[g]
        y_off = _dot(cm_b, prev.astype(BF16)) * decay_from_start[:, sl]
        st_scr[g] = prev * chunk_decay[:, sl] + _dot(bm.T.astype(BF16), xde_b[:, sl])
        yg = y_diag + y_off + dfull_ref[:, sl] * xs[:, sl]
        zg = z[:, sl]
        yg = yg * (zg * _sigmoid(zg))
        yg = yg * lax.rsqrt(jnp.mean(yg * yg, axis=-1, keepdims=True) + NORM_EPS)
        outs.append(yg * nw_ref[:, sl])
    return jnp.concatenate(outs, axis=1)


def _ssd(xbc, z, fdt, cw, cb, dtb, alog, dfull, nw, expand, b, lp):
    q = SSD_ROWS
    return pl.pallas_call(
        _ssd_body,
        out_shape=jax.ShapeDtypeStruct((b, lp, D_SSD), BF16),
        grid=(b, lp // q),
        in_specs=[pl.BlockSpec((1, q, D_XBC), lambda bi, ci: (bi, ci, 0)),
                  pl.BlockSpec((1, q, D_SSD), lambda bi, ci: (bi, ci, 0)),
                  pl.BlockSpec((1, q, FDT_COLS), lambda bi, ci: (bi, ci, 0)),
                  _const_spec((SSD_CONV, D_XBC)),
                  _const_spec((1, D_XBC)),
                  _const_spec((1, LANES)),
                  _const_spec((1, LANES)),
                  _const_spec((1, D_SSD)),
                  _const_spec((1, D_SSD)),
                  _const_spec((LANES, D_SSD))],
        out_specs=pl.BlockSpec((1, q, D_SSD), lambda bi, ci: (bi, ci, 0)),
        scratch_shapes=[pltpu.VMEM((SUBLANES + q, D_XBC), F32),
                        pltpu.VMEM((SSD_GROUPS, SSD_STATE, D_SSD // SSD_GROUPS), F32)],
        compiler_params=_cparams(("parallel", "arbitrary")),
        name="ssd",
    )(xbc, z, fdt, cw, cb, dtb, alog, dfull, nw, expand)


def _lru_body(xr_ref, gate_ref, cw_ref, cb_ref, w2_ref, ba_ref, bx_ref, lam_ref, o_ref,
              xp_scr, h_scr):
    rows = SCAN_ROWS
    first_tile = pl.program_id(1) == 0

    @pl.when(first_tile)
    def _():
        h_scr[...] = jnp.zeros(h_scr.shape, F32)

    xc = _causal_conv(xr_ref[0].astype(F32), xp_scr, cw_ref, cb_ref, first_tile, rows, LRU_CONV)
    xc_b = xc.astype(BF16)
    pre = [_dot(xc_b[:, j * LANES:(j + 1) * LANES], w2_ref[j]) for j in range(D_LRU // LANES)]
    pre_a = jnp.concatenate([p[:, :LANES] for p in pre], axis=1)
    pre_x = jnp.concatenate([p[:, LANES:] for p in pre], axis=1)
    r = _sigmoid(pre_a + ba_ref[...])
    i = _sigmoid(pre_x + bx_ref[...])
    log_a = LRU_C * r * _log_sigmoid(lam_ref[...])
    a = jnp.exp(log_a)
    mult = jnp.sqrt(-jnp.tanh(log_a) * (a * a + 1.0))
    row0 = lax.broadcasted_iota(jnp.int32, (SUBLANES, D_LRU), 0) == 0
    mult = jnp.concatenate([jnp.where(first_tile & row0, 1.0, mult[:SUBLANES]), mult[SUBLANES:]], axis=0)
    u = mult * (i * xc)

    groups = rows // SUBLANES
    a3 = a.reshape(groups, SUBLANES, D_LRU)
    u3 = u.reshape(groups, SUBLANES, D_LRU)
    sub = lax.broadcasted_iota(jnp.int32, (groups, SUBLANES, D_LRU), 1)
    d = 1
    while d < SUBLANES:
        keep = sub >= d
        a_s = jnp.where(keep, pltpu.roll(a3, d, 1), 1.0)
        u_s = jnp.where(keep, pltpu.roll(u3, d, 1), 0.0)
        u3 = a3 * u_s + u3
        a3 = a3 * a_s
        d *= 2
    h_prev = h_scr[0:1, :]
    hs = []
    for r in range(groups):
        h_r = a3[r] * h_prev + u3[r]
        hs.append(h_r)
        h_prev = h_r[SUBLANES - 1:SUBLANES, :]
    h = jnp.concatenate(hs, axis=0)
    h_scr[0:1, :] = h[rows - 1:rows, :]
    o_ref[0] = (h * jax.nn.gelu(gate_ref[0].astype(F32))).astype(o_ref.dtype)


def _lru(xr, gate, cw, cb, w2, ba, bx, lam, b, lp):
    rows = SCAN_ROWS
    return pl.pallas_call(
        _lru_body,
        out_shape=jax.ShapeDtypeStruct((b, lp, D_LRU), BF16),
        grid=(b, lp // rows),
        in_specs=[pl.BlockSpec((1, rows, D_LRU), lambda bi, ti: (bi, ti, 0)),
                  pl.BlockSpec((1, rows, D_LRU), lambda bi, ti: (bi, ti, 0)),
                  _const_spec((LRU_CONV, D_LRU)),
                  _const_spec((1, D_LRU)),
                  _const_spec((D_LRU // LANES, LANES, 2 * LANES)),
                  _const_spec((1, D_LRU)),
                  _const_spec((1, D_LRU)),
                  _const_spec((1, D_LRU))],
        out_specs=pl.BlockSpec((1, rows, D_LRU), lambda bi, ti: (bi, ti, 0)),
        scratch_shapes=[pltpu.VMEM((SUBLANES + rows, D_LRU), F32),
                        pltpu.VMEM((SUBLANES, D_LRU), F32)],
        compiler_params=_cparams(("parallel", "arbitrary")),
        name="rglru",
    )(xr, gate, cw, cb, w2, ba, bx, lam)


def _merge_body(h_ref, ya_ref, yb_ref, yc_ref, m_ref, wa_ref, wb_ref, wc_ref, wo_ref, o_ref):
    gate = lambda i: _sigmoid(m_ref[:, i * D_MODEL:(i + 1) * D_MODEL].astype(F32))
    mixed = gate(0) * _dot(ya_ref[...], wa_ref[...])
    mixed = mixed + gate(1) * _dot(yb_ref[...], wb_ref[...])
    mixed = mixed + gate(2) * _dot(yc_ref[...], wc_ref[...])
    o_ref[...] = h_ref[...] + _dot(mixed.astype(BF16), wo_ref[...])


def _merge(h, ya, yb, yc, m, wa, wb, wc, wo):
    t = h.shape[0]
    tm = MERGE_ROWS
    row_spec = lambda width: pl.BlockSpec((tm, width), lambda i: (i, 0))
    return pl.pallas_call(
        _merge_body,
        out_shape=jax.ShapeDtypeStruct((t, D_MODEL), F32),
        grid=(t // tm,),
        in_specs=[row_spec(D_MODEL), row_spec(D_ATTN), row_spec(D_SSD), row_spec(D_LRU),
                  row_spec(N_BRANCH * D_MODEL),
                  _const_spec((D_ATTN, D_MODEL)), _const_spec((D_SSD, D_MODEL)),
                  _const_spec((D_LRU, D_MODEL)), _const_spec((D_MODEL, D_MODEL))],
        out_specs=row_spec(D_MODEL),
        compiler_params=_cparams(("parallel",)),
        name="merge_out",
    )(h, ya, yb, yc, m, wa, wb, wc, wo)


def _prep_w_in(w_in):
    sizes = (D_ATTN, D_ATTN, D_ATTN, ATTN_HEADS, D_SSD, D_XBC, SSD_HEADS, D_LRU, D_LRU, N_BRANCH * D_MODEL)
    offs = [0]
    for s in sizes:
        offs.append(offs[-1] + s)
    part = lambda i: w_in[:, offs[i]:offs[i + 1]]
    q, k, v, f, z, xbc, dt, xr, gate, merge = (part(i) for i in range(10))
    pad = jnp.zeros((D_MODEL, FDT_COLS - ATTN_HEADS - SSD_HEADS), w_in.dtype)
    w = jnp.concatenate([k, z, xbc, xr, gate, merge, f, dt, pad], axis=1).astype(BF16)
    wt = jnp.concatenate([q.T, v.T], axis=0).astype(BF16)
    return w, wt


def _pad_lanes(vec, lane0):
    out = jnp.zeros((1, LANES), F32)
    return out.at[0, lane0:lane0 + vec.shape[0]].set(vec.astype(F32))


def _lru_gate_weights(w_a, w_x):
    def blockdiag_pairs(w):
        w = w.reshape(LRU_BLOCKS // 2, 2, LRU_BLOCK_DIM, LRU_BLOCK_DIM)
        zero = jnp.zeros_like(w[:, 0])
        top = jnp.concatenate([w[:, 0], zero], axis=2)
        bot = jnp.concatenate([zero, w[:, 1]], axis=2)
        return jnp.concatenate([top, bot], axis=1)
    return jnp.concatenate([blockdiag_pairs(w_a), blockdiag_pairs(w_x)], axis=2).astype(BF16)


def _head_expand():
    rows = jnp.arange(LANES)[:, None]
    cols = jnp.arange(D_SSD)[None, :]
    return (rows == DT_LANE0 + cols // SSD_HEAD_DIM).astype(BF16)


def kernel(x, meta_tokens, ffn1_norm, ffn1_w_gate_up, ffn1_w_down, mix_norm, w_in, fox_forget_bias,
           ssd_conv_w, ssd_conv_b, ssd_dt_bias, ssd_a_log, ssd_d, ssd_norm,
           lru_conv_w, lru_conv_b, lru_w_a, lru_b_a, lru_w_x, lru_b_x, lru_lambda,
           w_branch_attn, w_branch_ssd, w_branch_lru, w_out,
           ffn2_norm, ffn2_w_gate_up, ffn2_w_down, final_norm):
    b, s, d = x.shape
    depth = w_in.shape[0]
    length = N_META + s
    lp = -(-length // SEQ_ALIGN) * SEQ_ALIGN
    t = b * lp
    assert d == D_MODEL and t % FFN_ROWS == 0 and t % MERGE_ROWS == 0 and s % FFN_ROWS == 0

    meta = jnp.broadcast_to(meta_tokens.astype(x.dtype)[None], (b, N_META, d))
    h = jnp.concatenate([meta, x, jnp.zeros((b, lp - length, d), x.dtype)], axis=1).reshape(t, d)

    row = lambda vec: vec.astype(F32).reshape(1, -1)
    expand = _head_expand()
    pw = _aux_constants()
    fg = row(final_norm)
    for l in range(depth):
        h = _ffn(h, row(ffn1_norm[l]), ffn1_w_gate_up[l, :, :D_FF].astype(BF16),
                 ffn1_w_gate_up[l, :, D_FF:].astype(BF16), ffn1_w_down[l].astype(BF16), fg, False)

        w_std, w_t = _prep_w_in(w_in[l])
        qt, ka, vt, z, xbc, xr, gate, merge, fdt = _inproj(
            h, row(mix_norm[l]), w_std, w_t, _pad_lanes(fox_forget_bias[l], 0), pw, b, lp)
        y_a = _attention(qt, ka, vt, b, lp).reshape(t, D_ATTN)
        y_b = _ssd(xbc, z, fdt, ssd_conv_w[l].astype(F32), row(ssd_conv_b[l]),
                   _pad_lanes(ssd_dt_bias[l], DT_LANE0), _pad_lanes(ssd_a_log[l], DT_LANE0),
                   row(jnp.repeat(ssd_d[l], SSD_HEAD_DIM)), row(ssd_norm[l]), expand, b, lp).reshape(t, D_SSD)
        y_c = _lru(xr, gate, lru_conv_w[l].astype(F32), row(lru_conv_b[l]),
                   _lru_gate_weights(lru_w_a[l], lru_w_x[l]), row(lru_b_a[l]), row(lru_b_x[l]),
                   row(lru_lambda[l]), b, lp).reshape(t, D_LRU)
        h = _merge(h, y_a, y_b, y_c, merge.reshape(t, N_BRANCH * D_MODEL), w_branch_attn[l].astype(BF16),
                   w_branch_ssd[l].astype(BF16), w_branch_lru[l].astype(BF16), w_out[l].astype(BF16))

        ffn2 = (row(ffn2_norm[l]), ffn2_w_gate_up[l, :, :D_FF].astype(BF16),
                ffn2_w_gate_up[l, :, D_FF:].astype(BF16), ffn2_w_down[l].astype(BF16), fg)
        if l < depth - 1:
            h = _ffn(h, *ffn2, False)
    return _ffn_final(h, *ffn2, b, lp, s)
```

```python
import functools

import jax
import jax.numpy as jnp
from jax import lax
from jax.experimental import pallas as pl
from jax.experimental.pallas import tpu as pltpu

F32 = jnp.float32
BF16 = jnp.bfloat16

D_MODEL = 1024
N_META = 16
SSD_CHUNK = 128
NORM_EPS = 1e-6
ATTN_HEADS = 16
ATTN_HEAD_DIM = 64
D_ATTN = ATTN_HEADS * ATTN_HEAD_DIM
SSD_HEAD_DIM = 64
D_SSD = D_MODEL
SSD_HEADS = D_SSD // SSD_HEAD_DIM
SSD_GROUPS = 2
SSD_STATE = 128
SSD_CONV = 4
D_XBC = D_SSD + 2 * SSD_GROUPS * SSD_STATE
D_LRU = D_MODEL
LRU_BLOCKS = 16
LRU_BLOCK_DIM = D_LRU // LRU_BLOCKS
LRU_CONV = 4
LRU_C = 8.0
D_FF = 2816
N_BRANCH = 3

LANES = 128
SUBLANES = 8
VMEM_LIMIT_BYTES = 56 * 1024 * 1024

SEQ_ALIGN = 256
FFN_ROWS = 1024
FFN_CHUNK = 256
PROJ_ROWS = 256
PROJ_CHUNK = 512
ATTN_TQ = 256
ATTN_HPS = 16
MERGE_ROWS = 1024
SCAN_ROWS = 256
SSD_ROWS = 256
FDT_COLS = LANES
DT_LANE0 = ATTN_HEADS
NEG_BIG = -1e30
LOG2E = 1.4426950408889634
Q_SCALE = ATTN_HEAD_DIM ** -0.5 * LOG2E
AUX_PARTS = 3
AUX_SLOTS = 8


def _cparams(sem):
    return pltpu.CompilerParams(dimension_semantics=sem, vmem_limit_bytes=VMEM_LIMIT_BYTES)


def _const_spec(shape):
    nd = len(shape)
    return pl.BlockSpec(shape, lambda *_: (0,) * nd, pipeline_mode=pl.Buffered(1))


def _rms(x, g):
    ms = jnp.mean(x * x, axis=-1, keepdims=True)
    return (x * lax.rsqrt(ms + NORM_EPS)) * g


def _dot(a, b):
    return jnp.dot(a, b, preferred_element_type=F32)


def _dot_nt(a, b):
    return lax.dot_general(a, b, (((1,), (1,)), ((), ())), preferred_element_type=F32)


def _split3(x):
    hi = x.astype(BF16)
    r1 = x - hi.astype(F32)
    mid = r1.astype(BF16)
    lo = (r1 - mid.astype(F32)).astype(BF16)
    return hi, mid, lo


def _dot_01_lhs(sel, x):
    hi, mid, lo = _split3(x)
    return _dot(sel, hi) + _dot(sel, mid) + _dot(sel, lo)


def _dot_01_rhs(x, sel):
    hi, mid, lo = _split3(x)
    return _dot(hi, sel) + _dot(mid, sel) + _dot(lo, sel)


def _log_sigmoid(x):
    return -(jnp.maximum(-x, 0.0) + jnp.log1p(jnp.exp(-jnp.abs(x))))


def _softplus(x):
    return jnp.maximum(x, 0.0) + jnp.log1p(jnp.exp(-jnp.abs(x)))


def _sigmoid(x):
    return 1.0 / (1.0 + jnp.exp(-x))


def _ffn_body(x_ref, g_ref, wg_ref, wu_ref, wd_ref, fg_ref, o_ref, a_scr, *, final_norm):
    x = x_ref[...]
    hn = _rms(x, g_ref[...]).astype(BF16)
    for c0 in range(0, D_FF, FFN_CHUNK):
        gate = _dot(hn, wg_ref[:, c0:c0 + FFN_CHUNK])
        up = _dot(hn, wu_ref[:, c0:c0 + FFN_CHUNK])
        a_scr[:, c0:c0 + FFN_CHUNK] = ((gate * _sigmoid(gate)) * up).astype(BF16)
    y = x + 0.5 * _dot(a_scr[...], wd_ref[...])
    if final_norm:
        y = _rms(y, fg_ref[...])
    o_ref[...] = y


def _ffn(x, g, wg, wu, wd, fg, final_norm):
    t = x.shape[0]
    tm = FFN_ROWS
    return pl.pallas_call(
        functools.partial(_ffn_body, final_norm=final_norm),
        out_shape=jax.ShapeDtypeStruct((t, D_MODEL), F32),
        grid=(t // tm,),
        in_specs=[pl.BlockSpec((tm, D_MODEL), lambda i: (i, 0))] + _ffn_weight_specs(),
        out_specs=pl.BlockSpec((tm, D_MODEL), lambda i: (i, 0)),
        scratch_shapes=[pltpu.VMEM((tm, D_FF), BF16)],
        compiler_params=_cparams(("parallel",)),
        name="ffn",
    )(x, g, wg, wu, wd, fg)


def _ffn_weight_specs():
    return [_const_spec((1, D_MODEL)), _const_spec((D_MODEL, D_FF)), _const_spec((D_MODEL, D_FF)),
            _const_spec((D_FF, D_MODEL)), _const_spec((1, D_MODEL))]


def _ffn_final_body(x_ref, g_ref, wg_ref, wu_ref, wd_ref, fg_ref, o_ref, a_scr):
    _ffn_body(x_ref.at[0], g_ref, wg_ref, wu_ref, wd_ref, fg_ref, o_ref.at[0], a_scr, final_norm=True)


def _ffn_final(x, g, wg, wu, wd, fg, b, lp, s_out):
    tm = FFN_ROWS
    return pl.pallas_call(
        _ffn_final_body,
        out_shape=jax.ShapeDtypeStruct((b, s_out, D_MODEL), F32),
        grid=(b, s_out // tm),
        in_specs=[pl.BlockSpec((pl.Element(1), pl.Element(tm), pl.Element(D_MODEL)),
                               lambda bi, i: (bi, pl.multiple_of(N_META + i * tm, SUBLANES), 0))]
        + _ffn_weight_specs(),
        out_specs=pl.BlockSpec((1, tm, D_MODEL), lambda bi, i: (bi, i, 0)),
        scratch_shapes=[pltpu.VMEM((tm, D_FF), BF16)],
        compiler_params=_cparams(("parallel", "parallel")),
        name="ffn_final",
    )(x.reshape(b, lp, D_MODEL), g, wg, wu, wd, fg)


_PLAIN_GROUPS = (("z", D_SSD), ("xbc", D_XBC), ("xr", D_LRU), ("gate", D_LRU), ("merge", N_BRANCH * D_MODEL))
OFF_K = 0
OFF_PLAIN = D_ATTN
OFF_FDT = OFF_PLAIN + sum(w for _, w in _PLAIN_GROUPS)
N_PROJ = OFF_FDT + FDT_COLS
D_AUG = ATTN_HEADS * LANES
V_ROWS = ATTN_HEAD_DIM + 16
ONE_LANE = LANES - 1


def _inproj_body(x_ref, g_ref, w_ref, wt_ref, fb_ref, pw_ref, qt_ref, ka_ref, vt_ref,
                 z_ref, xbc_ref, xr_ref, gate_ref, merge_ref, fdt_ref, carry_scr):
    tm = x_ref.shape[1]

    @pl.when(pl.program_id(1) == 0)
    def _():
        carry_scr[...] = jnp.zeros(carry_scr.shape, F32)

    hn = _rms(x_ref[0], g_ref[...]).astype(BF16)

    def mm(c0, width):
        return _dot(hn, w_ref[:, c0:c0 + width])

    fdt = mm(OFF_FDT, FDT_COLS)
    fdt_ref[0] = fdt
    off = OFF_PLAIN
    for (_, width), o_ref in zip(_PLAIN_GROUPS, (z_ref, xbc_ref, xr_ref, gate_ref, merge_ref)):
        for c0 in range(0, width, PROJ_CHUNK):
            o_ref[0, :, c0:c0 + PROJ_CHUNK] = mm(off + c0, PROJ_CHUNK).astype(o_ref.dtype)
        off += width

    lane = lax.broadcasted_iota(jnp.int32, (tm, LANES), 1)
    lf = jnp.where(lane < ATTN_HEADS, _log_sigmoid(fdt + fb_ref[...]), 0.0)
    row = lax.broadcasted_iota(jnp.int32, (tm, tm), 0)
    col = lax.broadcasted_iota(jnp.int32, (tm, tm), 1)
    c = _dot_01_lhs((row >= col).astype(BF16), lf) + carry_scr[0:1, :]
    carry_scr[0:1, :] = c[tm - 1:tm, :]
    hi, mid, lo = (part.astype(F32) for part in _split3(c * LOG2E))
    cparts = (hi + pltpu.roll(mid, ATTN_HEADS, 1) + pltpu.roll(lo, 2 * ATTN_HEADS, 1)
              + jnp.where(lane == ONE_LANE, 1.0, 0.0))

    w_aux = _dot(cparts.astype(BF16), pw_ref[...])
    k_aux = w_aux[:, :LANES]
    q_aux_t = w_aux[:, LANES:].T

    first = lane < ATTN_HEAD_DIM
    for c0 in range(0, D_ATTN, PROJ_CHUNK):
        kv = mm(OFF_K + c0, PROJ_CHUNK)
        for pr in range(PROJ_CHUNK // LANES):
            pair = c0 // LANES + pr
            k_data = kv[:, pr * LANES:(pr + 1) * LANES]
            ka_ref[0, :, 2 * pair * LANES:(2 * pair + 1) * LANES] = jnp.where(
                first, k_data, k_aux).astype(ka_ref.dtype)
            ka_ref[0, :, (2 * pair + 1) * LANES:(2 * pair + 2) * LANES] = jnp.where(
                first, k_aux, k_data).astype(ka_ref.dtype)

    hd = ATTN_HEAD_DIM
    q_t = _dot_nt(wt_ref[0:D_ATTN, :], hn) * Q_SCALE
    v_t = _dot_nt(wt_ref[D_ATTN:2 * D_ATTN, :], hn)
    ones = jnp.ones((V_ROWS - hd, tm), vt_ref.dtype)
    for h in range(ATTN_HEADS):
        even = h % 2 == 0
        data0 = h * LANES + (0 if even else hd)
        aux0 = h * LANES + (hd if even else 0)
        a0 = (hd if even else 0) + AUX_SLOTS * (h // 2)
        before, after = AUX_SLOTS * (h // 2), hd - AUX_SLOTS * (h // 2 + 1)
        pieces = ([jnp.zeros((before, tm), F32)] if before else []) + [q_aux_t[a0:a0 + AUX_SLOTS, :]]
        pieces += [jnp.zeros((after, tm), F32)] if after else []
        qt_ref[0, data0:data0 + hd, :] = q_t[h * hd:(h + 1) * hd, :].astype(qt_ref.dtype)
        qt_ref[0, aux0:aux0 + hd, :] = jnp.concatenate(pieces, axis=0).astype(qt_ref.dtype)
        vt_ref[0, h * V_ROWS:h * V_ROWS + hd, :] = v_t[h * hd:(h + 1) * hd, :].astype(vt_ref.dtype)
        vt_ref[0, h * V_ROWS + hd:(h + 1) * V_ROWS, :] = ones


def _inproj(x, g, w, wt, fb, pw, b, lp):
    tm = PROJ_ROWS
    row_spec = lambda width: pl.BlockSpec((1, tm, width), lambda bi, ti: (bi, ti, 0))
    col_spec = lambda height: pl.BlockSpec((1, height, tm), lambda bi, ti: (bi, 0, ti))
    plain = [w_ for _, w_ in _PLAIN_GROUPS]
    out_shape = ([jax.ShapeDtypeStruct((b, D_AUG, lp), BF16), jax.ShapeDtypeStruct((b, lp, D_AUG), BF16),
                  jax.ShapeDtypeStruct((b, ATTN_HEADS * V_ROWS, lp), BF16)]
                 + [jax.ShapeDtypeStruct((b, lp, w_), BF16) for w_ in plain]
                 + [jax.ShapeDtypeStruct((b, lp, FDT_COLS), F32)])
    return pl.pallas_call(
        _inproj_body,
        out_shape=out_shape,
        grid=(b, lp // tm),
        in_specs=[row_spec(D_MODEL),
                  _const_spec((1, D_MODEL)),
                  _const_spec((D_MODEL, N_PROJ)),
                  _const_spec((2 * D_ATTN, D_MODEL)),
                  _const_spec((1, LANES)),
                  _const_spec((LANES, 2 * LANES))],
        out_specs=([col_spec(D_AUG), row_spec(D_AUG), col_spec(ATTN_HEADS * V_ROWS)]
                   + [row_spec(w_) for w_ in plain] + [row_spec(FDT_COLS)]),
        scratch_shapes=[pltpu.VMEM((SUBLANES, LANES), F32)],
        compiler_params=_cparams(("parallel", "arbitrary")),
        name="inproj",
    )(x.reshape(b, lp, D_MODEL), g, w, wt, fb, pw)


def _aux_constants():
    src = jnp.arange(LANES)[:, None]
    lane = jnp.arange(LANES)[None, :]
    half = jnp.where(lane >= ATTN_HEAD_DIM, 0, 1)
    slot = lane % ATTN_HEAD_DIM
    head = 2 * (slot // AUX_SLOTS) + half
    idx = slot % AUX_SLOTS
    part = src // ATTN_HEADS
    part_src = (part < AUX_PARTS) & (src % ATTN_HEADS == head)
    one_src = src == ONE_LANE
    key = (one_src & (idx < AUX_PARTS)).astype(F32) - (part_src & (idx == part + AUX_PARTS)).astype(F32)
    qry = (part_src & (idx == part)).astype(F32) + (one_src & (idx >= AUX_PARTS) & (idx < 2 * AUX_PARTS)).astype(F32)
    return jnp.concatenate([key, qry], axis=1).astype(BF16)


def _attn_body(qt_ref, qn_ref, ka_ref, vt_ref, mb_ref, o_ref, m_scr, acc_scr, s_scr, *, tq, hps):
    iq = pl.program_id(2)
    m_scr[...] = jnp.full(m_scr.shape, NEG_BIG, F32)
    acc_scr[...] = jnp.zeros(acc_scr.shape, F32)
    hd = ATTN_HEAD_DIM

    def scores(j, h):
        ks = pl.multiple_of(j * tq, tq)
        ka = ka_ref[0, pl.ds(ks, tq), h * LANES:(h + 1) * LANES]
        return _dot(ka, qt_ref[0, h * LANES:(h + 1) * LANES, :])

    def softmax_pv(j, h, st):
        ks = pl.multiple_of(j * tq, tq)
        vt = vt_ref[0, h * V_ROWS:(h + 1) * V_ROWS, pl.ds(ks, tq)]
        m_prev = m_scr[h]
        m_new = jnp.maximum(m_prev, jnp.max(st, axis=0, keepdims=True))
        alpha = jnp.exp2(m_prev - m_new)
        pt = jnp.exp2((st - m_new).astype(vt.dtype))
        acc_scr[h] = alpha * acc_scr[h] + _dot(vt, pt)
        m_scr[h] = m_new

    @pl.when(iq == 0)
    def _():
        for h in range(hps):
            s_scr[h] = scores(0, h)

    def loop_body(j, carry):
        for h in range(hps):
            s_next = scores(j + 1, h)
            softmax_pv(j, h, s_scr[h])
            s_scr[h] = s_next
        return carry

    lax.fori_loop(0, iq, loop_body, 0)
    for h in range(hps):
        s_next = _dot(ka_ref[0, 0:tq, h * LANES:(h + 1) * LANES], qn_ref[0, h * LANES:(h + 1) * LANES, :])
        softmax_pv(iq, h, s_scr[h] + mb_ref[...])
        s_scr[h] = s_next

    for hp in range(hps // 2):
        outs = []
        for e in range(2):
            acc = acc_scr[2 * hp + e]
            outs.append(acc[0:hd, :] / acc[hd:hd + 1, :])
        o_ref[0, :, hp * LANES:(hp + 1) * LANES] = jnp.concatenate(outs, axis=0).T.astype(o_ref.dtype)


def _attention(qt, ka, vt, b, lp):
    tq, hps = ATTN_TQ, ATTN_HPS
    key_i = jnp.arange(tq)[:, None]
    query_i = jnp.arange(tq)[None, :]
    causal_bias = jnp.where(key_i <= query_i, 0.0, NEG_BIG).astype(F32)
    last_tile = lp // tq - 1
    return pl.pallas_call(
        functools.partial(_attn_body, tq=tq, hps=hps),
        out_shape=jax.ShapeDtypeStruct((b, lp, D_ATTN), BF16),
        grid=(b, ATTN_HEADS // hps, lp // tq),
        in_specs=[pl.BlockSpec((1, hps * LANES, tq), lambda bi, hi, qi: (bi, hi, qi)),
                  pl.BlockSpec((1, hps * LANES, tq), lambda bi, hi, qi: (bi, hi, jnp.minimum(qi + 1, last_tile))),
                  pl.BlockSpec((1, lp, hps * LANES), lambda bi, hi, qi: (bi, 0, hi), pipeline_mode=pl.Buffered(1)),
                  pl.BlockSpec((1, hps * V_ROWS, lp), lambda bi, hi, qi: (bi, hi, 0)),
                  pl.BlockSpec((tq, tq), lambda bi, hi, qi: (0, 0))],
        out_specs=pl.BlockSpec((1, tq, hps * ATTN_HEAD_DIM), lambda bi, hi, qi: (bi, qi, hi)),
        scratch_shapes=[pltpu.VMEM((hps, 1, tq), F32),
                        pltpu.VMEM((hps, V_ROWS, tq), F32),
                        pltpu.VMEM((hps, tq, tq), F32)],
        compiler_params=_cparams(("parallel", "parallel", "arbitrary")),
        name="fox_attention",
    )(qt, qt, ka, vt, causal_bias)


def _causal_conv(x, xp_scr, w_ref, b_ref, first_tile, rows, taps):
    @pl.when(first_tile)
    def _():
        xp_scr[0:SUBLANES, :] = jnp.zeros((SUBLANES, x.shape[1]), F32)

    xp_scr[SUBLANES:SUBLANES + rows, :] = x
    y = b_ref[...] + w_ref[taps - 1:taps, :] * x
    for kk in range(taps - 1):
        r0 = SUBLANES - (taps - 1) + kk
        y = y + w_ref[kk:kk + 1, :] * xp_scr[r0:r0 + rows, :]
    xp_scr[0:SUBLANES, :] = x[rows - SUBLANES:rows, :]
    return y


def _ssd_body(xbc_ref, z_ref, fdt_ref, cw_ref, cb_ref, dtb_ref, alog_ref, dfull_ref, nw_ref,
              exp_ref, o_ref, xp_scr, st_scr):
    rows = SSD_ROWS
    first_tile = pl.program_id(1) == 0

    @pl.when(first_tile)
    def _():
        st_scr[...] = jnp.zeros(st_scr.shape, F32)

    y = _causal_conv(xbc_ref[0].astype(F32), xp_scr, cw_ref, cb_ref, first_tile, rows, SSD_CONV)
    xc = y * _sigmoid(y)
    for ci in range(rows // SSD_CHUNK):
        rs = slice(ci * SSD_CHUNK, (ci + 1) * SSD_CHUNK)
        o_ref[0, rs, :] = _ssd_chunk(xc[rs], z_ref[0, rs, :].astype(F32), fdt_ref[0, rs, :], dtb_ref, alog_ref,
                                     dfull_ref, nw_ref, exp_ref, st_scr).astype(o_ref.dtype)


def _ssd_chunk(xc, z, dt_raw, dtb_ref, alog_ref, dfull_ref, nw_ref, exp_ref, st_scr):
    q = SSD_CHUNK
    gs = D_SSD // SSD_GROUPS
    heads_per_group = SSD_HEADS // SSD_GROUPS

    lane = lax.broadcasted_iota(jnp.int32, (q, LANES), 1)
    dt_lane = (lane >= DT_LANE0) & (lane < DT_LANE0 + SSD_HEADS)
    dt = jnp.where(dt_lane, _softplus(dt_raw + dtb_ref[...]), 0.0)
    a = -jnp.exp(alog_ref[...])
    da = dt * a
    row = lax.broadcasted_iota(jnp.int32, (q, q), 0)
    col = lax.broadcasted_iota(jnp.int32, (q, q), 1)
    lower = row >= col
    a_cum = _dot_01_lhs(lower.astype(BF16), da)
    a_cum_t = a_cum.T
    expand = exp_ref[...]
    dt_full = _dot_01_rhs(dt, expand)
    a_cum_full = _dot_01_rhs(a_cum, expand)
    a_last_full = a_cum_full[q - 1:q, :]
    decay_to_end = jnp.exp(a_last_full - a_cum_full)
    decay_from_start = jnp.exp(a_cum_full)
    chunk_decay = jnp.exp(a_last_full)

    xs = xc[:, :D_SSD]
    xdt = xs * dt_full
    xdt_b = xdt.astype(BF16)
    xde_b = (xdt * decay_to_end).astype(BF16)
    half = lax.broadcasted_iota(jnp.int32, (q, LANES), 1) < SSD_HEAD_DIM

    outs = []
    for g in range(SSD_GROUPS):
        bm = xc[:, D_SSD + g * SSD_STATE:D_SSD + (g + 1) * SSD_STATE]
        cm = xc[:, D_SSD + SSD_GROUPS * SSD_STATE + g * SSD_STATE:
                D_SSD + SSD_GROUPS * SSD_STATE + (g + 1) * SSD_STATE]
        bm_b = bm.astype(BF16)
        cm_b = cm.astype(BF16)
        cb = _dot_nt(cm_b, bm_b)
        y_pairs = []
        for pair in range(heads_per_group // 2):
            c0 = g * gs + pair * LANES
            xp = xdt_b[:, c0:c0 + LANES]
            ys = []
            for e in range(2):
                hl = DT_LANE0 + g * heads_per_group + 2 * pair + e
                seg = a_cum[:, hl:hl + 1] - a_cum_t[hl:hl + 1, :]
                dec = jnp.exp(jnp.where(lower, seg, -jnp.inf))
                ys.append(_dot((cb * dec).astype(BF16), xp))
            y_pairs.append(jnp.where(half, ys[0], ys[1]))
        y_diag = jnp.concatenate(y_pairs, axis=1)
        sl = slice(g * gs, (g + 1) * gs)
        prev = st_scr[g]
        y_off = _dot(cm_b, prev.astype(BF16)) * decay_from_start[:, sl]
        st_scr[g] = prev * chunk_decay[:, sl] + _dot(bm.T.astype(BF16), xde_b[:, sl])
        yg = y_diag + y_off + dfull_ref[:, sl] * xs[:, sl]
        zg = z[:, sl]
        yg = yg * (zg * _sigmoid(zg))
        yg = yg * lax.rsqrt(jnp.mean(yg * yg, axis=-1, keepdims=True) + NORM_EPS)
        outs.append(yg * nw_ref[:, sl])
    return jnp.concatenate(outs, axis=1)


def _ssd(xbc, z, fdt, cw, cb, dtb, alog, dfull, nw, expand, b, lp):
    q = SSD_ROWS
    return pl.pallas_call(
        _ssd_body,
        out_shape=jax.ShapeDtypeStruct((b, lp, D_SSD), BF16),
        grid=(b, lp // q),
        in_specs=[pl.BlockSpec((1, q, D_XBC), lambda bi, ci: (bi, ci, 0)),
                  pl.BlockSpec((1, q, D_SSD), lambda bi, ci: (bi, ci, 0)),
                  pl.BlockSpec((1, q, FDT_COLS), lambda bi, ci: (bi, ci, 0)),
                  _const_spec((SSD_CONV, D_XBC)),
                  _const_spec((1, D_XBC)),
                  _const_spec((1, LANES)),
                  _const_spec((1, LANES)),
                  _const_spec((1, D_SSD)),
                  _const_spec((1, D_SSD)),
                  _const_spec((LANES, D_SSD))],
        out_specs=pl.BlockSpec((1, q, D_SSD), lambda bi, ci: (bi, ci, 0)),
        scratch_shapes=[pltpu.VMEM((SUBLANES + q, D_XBC), F32),
                        pltpu.VMEM((SSD_GROUPS, SSD_STATE, D_SSD // SSD_GROUPS), F32)],
        compiler_params=_cparams(("parallel", "arbitrary")),
        name="ssd",
    )(xbc, z, fdt, cw, cb, dtb, alog, dfull, nw, expand)


def _lru_body(xr_ref, gate_ref, cw_ref, cb_ref, w2_ref, ba_ref, bx_ref, lam_ref, o_ref,
              xp_scr, h_scr):
    rows = SCAN_ROWS
    first_tile = pl.program_id(1) == 0

    @pl.when(first_tile)
    def _():
        h_scr[...] = jnp.zeros(h_scr.shape, F32)

    xc = _causal_conv(xr_ref[0].astype(F32), xp_scr, cw_ref, cb_ref, first_tile, rows, LRU_CONV)
    xc_b = xc.astype(BF16)
    pre = [_dot(xc_b[:, j * LANES:(j + 1) * LANES], w2_ref[j]) for j in range(D_LRU // LANES)]
    pre_a = jnp.concatenate([p[:, :LANES] for p in pre], axis=1)
    pre_x = jnp.concatenate([p[:, LANES:] for p in pre], axis=1)
    r = _sigmoid(pre_a + ba_ref[...])
    i = _sigmoid(pre_x + bx_ref[...])
    log_a = LRU_C * r * _log_sigmoid(lam_ref[...])
    a = jnp.exp(log_a)
    mult = jnp.sqrt(-jnp.tanh(log_a) * (a * a + 1.0))
    row0 = lax.broadcasted_iota(jnp.int32, (SUBLANES, D_LRU), 0) == 0
    mult = jnp.concatenate([jnp.where(first_tile & row0, 1.0, mult[:SUBLANES]), mult[SUBLANES:]], axis=0)
    u = mult * (i * xc)

    groups = rows // SUBLANES
    a3 = a.reshape(groups, SUBLANES, D_LRU)
    u3 = u.reshape(groups, SUBLANES, D_LRU)
    sub = lax.broadcasted_iota(jnp.int32, (groups, SUBLANES, D_LRU), 1)
    d = 1
    while d < SUBLANES:
        keep = sub >= d
        a_s = jnp.where(keep, pltpu.roll(a3, d, 1), 1.0)
        u_s = jnp.where(keep, pltpu.roll(u3, d, 1), 0.0)
        u3 = a3 * u_s + u3
        a3 = a3 * a_s
        d *= 2
    h_prev = h_scr[0:1, :]
    hs = []
    for r in range(groups):
        h_r = a3[r] * h_prev + u3[r]
        hs.append(h_r)
        h_prev = h_r[SUBLANES - 1:SUBLANES, :]
    h = jnp.concatenate(hs, axis=0)
    h_scr[0:1, :] = h[rows - 1:rows, :]
    o_ref[0] = (h * jax.nn.gelu(gate_ref[0].astype(F32))).astype(o_ref.dtype)


def _lru(xr, gate, cw, cb, w2, ba, bx, lam, b, lp):
    rows = SCAN_ROWS
    return pl.pallas_call(
        _lru_body,
        out_shape=jax.ShapeDtypeStruct((b, lp, D_LRU), BF16),
        grid=(b, lp // rows),
        in_specs=[pl.BlockSpec((1, rows, D_LRU), lambda bi, ti: (bi, ti, 0)),
                  pl.BlockSpec((1, rows, D_LRU), lambda bi, ti: (bi, ti, 0)),
                  _const_spec((LRU_CONV, D_LRU)),
                  _const_spec((1, D_LRU)),
                  _const_spec((D_LRU // LANES, LANES, 2 * LANES)),
                  _const_spec((1, D_LRU)),
                  _const_spec((1, D_LRU)),
                  _const_spec((1, D_LRU))],
        out_specs=pl.BlockSpec((1, rows, D_LRU), lambda bi, ti: (bi, ti, 0)),
        scratch_shapes=[pltpu.VMEM((SUBLANES + rows, D_LRU), F32),
                        pltpu.VMEM((SUBLANES, D_LRU), F32)],
        compiler_params=_cparams(("parallel", "arbitrary")),
        name="rglru",
    )(xr, gate, cw, cb, w2, ba, bx, lam)


def _merge_body(h_ref, ya_ref, yb_ref, yc_ref, m_ref, wa_ref, wb_ref, wc_ref, wo_ref, o_ref):
    gate = lambda i: _sigmoid(m_ref[:, i * D_MODEL:(i + 1) * D_MODEL].astype(F32))
    mixed = gate(0) * _dot(ya_ref[...], wa_ref[...])
    mixed = mixed + gate(1) * _dot(yb_ref[...], wb_ref[...])
    mixed = mixed + gate(2) * _dot(yc_ref[...], wc_ref[...])
    o_ref[...] = h_ref[...] + _dot(mixed.astype(BF16), wo_ref[...])


def _merge(h, ya, yb, yc, m, wa, wb, wc, wo):
    t = h.shape[0]
    tm = MERGE_ROWS
    row_spec = lambda width: pl.BlockSpec((tm, width), lambda i: (i, 0))
    return pl.pallas_call(
        _merge_body,
        out_shape=jax.ShapeDtypeStruct((t, D_MODEL), F32),
        grid=(t // tm,),
        in_specs=[row_spec(D_MODEL), row_spec(D_ATTN), row_spec(D_SSD), row_spec(D_LRU),
                  row_spec(N_BRANCH * D_MODEL),
                  _const_spec((D_ATTN, D_MODEL)), _const_spec((D_SSD, D_MODEL)),
                  _const_spec((D_LRU, D_MODEL)), _const_spec((D_MODEL, D_MODEL))],
        out_specs=row_spec(D_MODEL),
        compiler_params=_cparams(("parallel",)),
        name="merge_out",
    )(h, ya, yb, yc, m, wa, wb, wc, wo)


def _prep_w_in(w_in):
    sizes = (D_ATTN, D_ATTN, D_ATTN, ATTN_HEADS, D_SSD, D_XBC, SSD_HEADS, D_LRU, D_LRU, N_BRANCH * D_MODEL)
    offs = [0]
    for s in sizes:
        offs.append(offs[-1] + s)
    part = lambda i: w_in[:, offs[i]:offs[i + 1]]
    q, k, v, f, z, xbc, dt, xr, gate, merge = (part(i) for i in range(10))
    pad = jnp.zeros((D_MODEL, FDT_COLS - ATTN_HEADS - SSD_HEADS), w_in.dtype)
    w = jnp.concatenate([k, z, xbc, xr, gate, merge, f, dt, pad], axis=1).astype(BF16)
    wt = jnp.concatenate([q.T, v.T], axis=0).astype(BF16)
    return w, wt


def _pad_lanes(vec, lane0):
    out = jnp.zeros((1, LANES), F32)
    return out.at[0, lane0:lane0 + vec.shape[0]].set(vec.astype(F32))


def _lru_gate_weights(w_a, w_x):
    def blockdiag_pairs(w):
        w = w.reshape(LRU_BLOCKS // 2, 2, LRU_BLOCK_DIM, LRU_BLOCK_DIM)
        zero = jnp.zeros_like(w[:, 0])
        top = jnp.concatenate([w[:, 0], zero], axis=2)
        bot = jnp.concatenate([zero, w[:, 1]], axis=2)
        return jnp.concatenate([top, bot], axis=1)
    return jnp.concatenate([blockdiag_pairs(w_a), blockdiag_pairs(w_x)], axis=2).astype(BF16)


def _head_expand():
    rows = jnp.arange(LANES)[:, None]
    cols = jnp.arange(D_SSD)[None, :]
    return (rows == DT_LANE0 + cols // SSD_HEAD_DIM).astype(BF16)


def kernel(x, meta_tokens, ffn1_norm, ffn1_w_gate_up, ffn1_w_down, mix_norm, w_in, fox_forget_bias,
           ssd_conv_w, ssd_conv_b, ssd_dt_bias, ssd_a_log, ssd_d, ssd_norm,
           lru_conv_w, lru_conv_b, lru_w_a, lru_b_a, lru_w_x, lru_b_x, lru_lambda,
           w_branch_attn, w_branch_ssd, w_branch_lru, w_out,
           ffn2_norm, ffn2_w_gate_up, ffn2_w_down, final_norm):
    b, s, d = x.shape
    depth = w_in.shape[0]
    length = N_META + s
    lp = -(-length // SEQ_ALIGN) * SEQ_ALIGN
    t = b * lp
    assert d == D_MODEL and t % FFN_ROWS == 0 and t % MERGE_ROWS == 0 and s % FFN_ROWS == 0

    meta = jnp.broadcast_to(meta_tokens.astype(x.dtype)[None], (b, N_META, d))
    h = jnp.concatenate([meta, x, jnp.zeros((b, lp - length, d), x.dtype)], axis=1).reshape(t, d)

    row = lambda vec: vec.astype(F32).reshape(1, -1)
    expand = _head_expand()
    pw = _aux_constants()
    fg = row(final_norm)
    for l in range(depth):
        h = _ffn(h, row(ffn1_norm[l]), ffn1_w_gate_up[l, :, :D_FF].astype(BF16),
                 ffn1_w_gate_up[l, :, D_FF:].astype(BF16), ffn1_w_down[l].astype(BF16), fg, False)

        w_std, w_t = _prep_w_in(w_in[l])
        qt, ka, vt, z, xbc, xr, gate, merge, fdt = _inproj(
            h, row(mix_norm[l]), w_std, w_t, _pad_lanes(fox_forget_bias[l], 0), pw, b, lp)
        y_a = _attention(qt, ka, vt, b, lp).reshape(t, D_ATTN)
        y_b = _ssd(xbc, z, fdt, ssd_conv_w[l].astype(F32), row(ssd_conv_b[l]),
                   _pad_lanes(ssd_dt_bias[l], DT_LANE0), _pad_lanes(ssd_a_log[l], DT_LANE0),
                   row(jnp.repeat(ssd_d[l], SSD_HEAD_DIM)), row(ssd_norm[l]), expand, b, lp).reshape(t, D_SSD)
        y_c = _lru(xr, gate, lru_conv_w[l].astype(F32), row(lru_conv_b[l]),
                   _lru_gate_weights(lru_w_a[l], lru_w_x[l]), row(lru_b_a[l]), row(lru_b_x[l]),
                   row(lru_lambda[l]), b, lp).reshape(t, D_LRU)
        h = _merge(h, y_a, y_b, y_c, merge.reshape(t, N_BRANCH * D_MODEL), w_branch_attn[l].astype(BF16),
                   w_branch_ssd[l].astype(BF16), w_branch_lru[l].astype(BF16), w_out[l].astype(BF16))

        ffn2 = (row(ffn2_norm[l]), ffn2_w_gate_up[l, :, :D_FF].astype(BF16),
                ffn2_w_gate_up[l, :, D_FF:].astype(BF16), ffn2_w_down[l].astype(BF16), fg)
        if l < depth - 1:
            h = _ffn(h, *ffn2, False)
    return _ffn_final(h, *ffn2, b, lp, s)
```

```python
import functools

import jax
import jax.numpy as jnp
from jax import lax
from jax.experimental import pallas as pl
from jax.experimental.pallas import tpu as pltpu

F32 = jnp.float32
BF16 = jnp.bfloat16

D_MODEL = 1024
N_META = 16
SSD_CHUNK = 128
NORM_EPS = 1e-6
ATTN_HEADS = 16
ATTN_HEAD_DIM = 64
D_ATTN = ATTN_HEADS * ATTN_HEAD_DIM
SSD_HEAD_DIM = 64
D_SSD = D_MODEL
SSD_HEADS = D_SSD // SSD_HEAD_DIM
SSD_GROUPS = 2
SSD_STATE = 128
SSD_CONV = 4
D_XBC = D_SSD + 2 * SSD_GROUPS * SSD_STATE
D_LRU = D_MODEL
LRU_BLOCKS = 16
LRU_BLOCK_DIM = D_LRU // LRU_BLOCKS
LRU_CONV = 4
LRU_C = 8.0
D_FF = 2816
N_BRANCH = 3

LANES = 128
SUBLANES = 8
VMEM_LIMIT_BYTES = 56 * 1024 * 1024

SEQ_ALIGN = 256
FFN_ROWS = 1024
FFN_CHUNK = 256
PROJ_ROWS = 256
PROJ_CHUNK = 512
ATTN_TQ = 256
ATTN_HPS = 16
MERGE_ROWS = 1024
SCAN_ROWS = 256
SSD_ROWS = 256
FDT_COLS = LANES
DT_LANE0 = ATTN_HEADS
NEG_BIG = -1e30
LOG2E = 1.4426950408889634
Q_SCALE = ATTN_HEAD_DIM ** -0.5 * LOG2E
AUX_PARTS = 3
AUX_SLOTS = 8


def _cparams(sem):
    return pltpu.CompilerParams(dimension_semantics=sem, vmem_limit_bytes=VMEM_LIMIT_BYTES)


def _const_spec(shape):
    nd = len(shape)
    return pl.BlockSpec(shape, lambda *_: (0,) * nd, pipeline_mode=pl.Buffered(1))


def _rms(x, g):
    ms = jnp.mean(x * x, axis=-1, keepdims=True)
    return (x * lax.rsqrt(ms + NORM_EPS)) * g


def _dot(a, b):
    return jnp.dot(a, b, preferred_element_type=F32)


def _dot_nt(a, b):
    return lax.dot_general(a, b, (((1,), (1,)), ((), ())), preferred_element_type=F32)


def _split3(x):
    hi = x.astype(BF16)
    r1 = x - hi.astype(F32)
    mid = r1.astype(BF16)
    lo = (r1 - mid.astype(F32)).astype(BF16)
    return hi, mid, lo


def _dot_01_lhs(sel, x):
    hi, mid, lo = _split3(x)
    return _dot(sel, hi) + _dot(sel, mid) + _dot(sel, lo)


def _dot_01_rhs(x, sel):
    hi, mid, lo = _split3(x)
    return _dot(hi, sel) + _dot(mid, sel) + _dot(lo, sel)


def _log_sigmoid(x):
    return -(jnp.maximum(-x, 0.0) + jnp.log1p(jnp.exp(-jnp.abs(x))))


def _softplus(x):
    return jnp.maximum(x, 0.0) + jnp.log1p(jnp.exp(-jnp.abs(x)))


def _sigmoid(x):
    return 1.0 / (1.0 + jnp.exp(-x))


def _ffn_body(x_ref, g_ref, wg_ref, wu_ref, wd_ref, fg_ref, o_ref, a_scr, *, final_norm):
    x = x_ref[...]
    hn = _rms(x, g_ref[...]).astype(BF16)
    for c0 in range(0, D_FF, FFN_CHUNK):
        gate = _dot(hn, wg_ref[:, c0:c0 + FFN_CHUNK])
        up = _dot(hn, wu_ref[:, c0:c0 + FFN_CHUNK])
        a_scr[:, c0:c0 + FFN_CHUNK] = ((gate * _sigmoid(gate)) * up).astype(BF16)
    y = x + 0.5 * _dot(a_scr[...], wd_ref[...])
    if final_norm:
        y = _rms(y, fg_ref[...])
    o_ref[...] = y


def _ffn(x, g, wg, wu, wd, fg, final_norm):
    t = x.shape[0]
    tm = FFN_ROWS
    return pl.pallas_call(
        functools.partial(_ffn_body, final_norm=final_norm),
        out_shape=jax.ShapeDtypeStruct((t, D_MODEL), F32),
        grid=(t // tm,),
        in_specs=[pl.BlockSpec((tm, D_MODEL), lambda i: (i, 0))] + _ffn_weight_specs(),
        out_specs=pl.BlockSpec((tm, D_MODEL), lambda i: (i, 0)),
        scratch_shapes=[pltpu.VMEM((tm, D_FF), BF16)],
        compiler_params=_cparams(("parallel",)),
        name="ffn",
    )(x, g, wg, wu, wd, fg)


def _ffn_weight_specs():
    return [_const_spec((1, D_MODEL)), _const_spec((D_MODEL, D_FF)), _const_spec((D_MODEL, D_FF)),
            _const_spec((D_FF, D_MODEL)), _const_spec((1, D_MODEL))]


def _ffn_final_body(x_ref, g_ref, wg_ref, wu_ref, wd_ref, fg_ref, o_ref, a_scr):
    _ffn_body(x_ref.at[0], g_ref, wg_ref, wu_ref, wd_ref, fg_ref, o_ref.at[0], a_scr, final_norm=True)


def _ffn_final(x, g, wg, wu, wd, fg, b, lp, s_out):
    tm = FFN_ROWS
    return pl.pallas_call(
        _ffn_final_body,
        out_shape=jax.ShapeDtypeStruct((b, s_out, D_MODEL), F32),
        grid=(b, s_out // tm),
        in_specs=[pl.BlockSpec((pl.Element(1), pl.Element(tm), pl.Element(D_MODEL)),
                               lambda bi, i: (bi, pl.multiple_of(N_META + i * tm, SUBLANES), 0))]
        + _ffn_weight_specs(),
        out_specs=pl.BlockSpec((1, tm, D_MODEL), lambda bi, i: (bi, i, 0)),
        scratch_shapes=[pltpu.VMEM((tm, D_FF), BF16)],
        compiler_params=_cparams(("parallel", "parallel")),
        name="ffn_final",
    )(x.reshape(b, lp, D_MODEL), g, wg, wu, wd, fg)


_PLAIN_GROUPS = (("z", D_SSD), ("xbc", D_XBC), ("xr", D_LRU), ("gate", D_LRU), ("merge", N_BRANCH * D_MODEL))
OFF_K = 0
OFF_PLAIN = D_ATTN
OFF_FDT = OFF_PLAIN + sum(w for _, w in _PLAIN_GROUPS)
N_PROJ = OFF_FDT + FDT_COLS
D_AUG = ATTN_HEADS * LANES
V_ROWS = ATTN_HEAD_DIM + 16
ONE_LANE = LANES - 1


def _inproj_body(x_ref, g_ref, w_ref, wt_ref, fb_ref, pw_ref, qt_ref, ka_ref, vt_ref,
                 z_ref, xbc_ref, xr_ref, gate_ref, merge_ref, fdt_ref, carry_scr):
    tm = x_ref.shape[1]

    @pl.when(pl.program_id(1) == 0)
    def _():
        carry_scr[...] = jnp.zeros(carry_scr.shape, F32)

    hn = _rms(x_ref[0], g_ref[...]).astype(BF16)

    def mm(c0, width):
        return _dot(hn, w_ref[:, c0:c0 + width])

    fdt = mm(OFF_FDT, FDT_COLS)
    fdt_ref[0] = fdt
    off = OFF_PLAIN
    for (_, width), o_ref in zip(_PLAIN_GROUPS, (z_ref, xbc_ref, xr_ref, gate_ref, merge_ref)):
        for c0 in range(0, width, PROJ_CHUNK):
            o_ref[0, :, c0:c0 + PROJ_CHUNK] = mm(off + c0, PROJ_CHUNK).astype(o_ref.dtype)
        off += width

    lane = lax.broadcasted_iota(jnp.int32, (tm, LANES), 1)
    lf = jnp.where(lane < ATTN_HEADS, _log_sigmoid(fdt + fb_ref[...]), 0.0)
    row = lax.broadcasted_iota(jnp.int32, (tm, tm), 0)
    col = lax.broadcasted_iota(jnp.int32, (tm, tm), 1)
    c = _dot_01_lhs((row >= col).astype(BF16), lf) + carry_scr[0:1, :]
    carry_scr[0:1, :] = c[tm - 1:tm, :]
    hi, mid, lo = (part.astype(F32) for part in _split3(c * LOG2E))
    cparts = (hi + pltpu.roll(mid, ATTN_HEADS, 1) + pltpu.roll(lo, 2 * ATTN_HEADS, 1)
              + jnp.where(lane == ONE_LANE, 1.0, 0.0))

    w_aux = _dot(cparts.astype(BF16), pw_ref[...])
    k_aux = w_aux[:, :LANES]
    q_aux_t = w_aux[:, LANES:].T

    first = lane < ATTN_HEAD_DIM
    for c0 in range(0, D_ATTN, PROJ_CHUNK):
        kv = mm(OFF_K + c0, PROJ_CHUNK)
        for pr in range(PROJ_CHUNK // LANES):
            pair = c0 // LANES + pr
            k_data = kv[:, pr * LANES:(pr + 1) * LANES]
            ka_ref[0, :, 2 * pair * LANES:(2 * pair + 1) * LANES] = jnp.where(
                first, k_data, k_aux).astype(ka_ref.dtype)
            ka_ref[0, :, (2 * pair + 1) * LANES:(2 * pair + 2) * LANES] = jnp.where(
                first, k_aux, k_data).astype(ka_ref.dtype)

    hd = ATTN_HEAD_DIM
    q_t = _dot_nt(wt_ref[0:D_ATTN, :], hn) * Q_SCALE
    v_t = _dot_nt(wt_ref[D_ATTN:2 * D_ATTN, :], hn)
    ones = jnp.ones((V_ROWS - hd, tm), vt_ref.dtype)
    for h in range(ATTN_HEADS):
        even = h % 2 == 0
        data0 = h * LANES + (0 if even else hd)
        aux0 = h * LANES + (hd if even else 0)
        a0 = (hd if even else 0) + AUX_SLOTS * (h // 2)
        before, after = AUX_SLOTS * (h // 2), hd - AUX_SLOTS * (h // 2 + 1)
        pieces = ([jnp.zeros((before, tm), F32)] if before else []) + [q_aux_t[a0:a0 + AUX_SLOTS, :]]
        pieces += [jnp.zeros((after, tm), F32)] if after else []
        qt_ref[0, data0:data0 + hd, :] = q_t[h * hd:(h + 1) * hd, :].astype(qt_ref.dtype)
        qt_ref[0, aux0:aux0 + hd, :] = jnp.concatenate(pieces, axis=0).astype(qt_ref.dtype)
        vt_ref[0, h * V_ROWS:h * V_ROWS + hd, :] = v_t[h * hd:(h + 1) * hd, :].astype(vt_ref.dtype)
        vt_ref[0, h * V_ROWS + hd:(h + 1) * V_ROWS, :] = ones


def _inproj(x, g, w, wt, fb, pw, b, lp):
    tm = PROJ_ROWS
    row_spec = lambda width: pl.BlockSpec((1, tm, width), lambda bi, ti: (bi, ti, 0))
    col_spec = lambda height: pl.BlockSpec((1, height, tm), lambda bi, ti: (bi, 0, ti))
    plain = [w_ for _, w_ in _PLAIN_GROUPS]
    out_shape = ([jax.ShapeDtypeStruct((b, D_AUG, lp), BF16), jax.ShapeDtypeStruct((b, lp, D_AUG), BF16),
                  jax.ShapeDtypeStruct((b, ATTN_HEADS * V_ROWS, lp), BF16)]
                 + [jax.ShapeDtypeStruct((b, lp, w_), BF16) for w_ in plain]
                 + [jax.ShapeDtypeStruct((b, lp, FDT_COLS), F32)])
    return pl.pallas_call(
        _inproj_body,
        out_shape=out_shape,
        grid=(b, lp // tm),
        in_specs=[row_spec(D_MODEL),
                  _const_spec((1, D_MODEL)),
                  _const_spec((D_MODEL, N_PROJ)),
                  _const_spec((2 * D_ATTN, D_MODEL)),
                  _const_spec((1, LANES)),
                  _const_spec((LANES, 2 * LANES))],
        out_specs=([col_spec(D_AUG), row_spec(D_AUG), col_spec(ATTN_HEADS * V_ROWS)]
                   + [row_spec(w_) for w_ in plain] + [row_spec(FDT_COLS)]),
        scratch_shapes=[pltpu.VMEM((SUBLANES, LANES), F32)],
        compiler_params=_cparams(("parallel", "arbitrary")),
        name="inproj",
    )(x.reshape(b, lp, D_MODEL), g, w, wt, fb, pw)


def _aux_constants():
    src = jnp.arange(LANES)[:, None]
    lane = jnp.arange(LANES)[None, :]
    half = jnp.where(lane >= ATTN_HEAD_DIM, 0, 1)
    slot = lane % ATTN_HEAD_DIM
    head = 2 * (slot // AUX_SLOTS) + half
    idx = slot % AUX_SLOTS
    part = src // ATTN_HEADS
    part_src = (part < AUX_PARTS) & (src % ATTN_HEADS == head)
    one_src = src == ONE_LANE
    key = (one_src & (idx < AUX_PARTS)).astype(F32) - (part_src & (idx == part + AUX_PARTS)).astype(F32)
    qry = (part_src & (idx == part)).astype(F32) + (one_src & (idx >= AUX_PARTS) & (idx < 2 * AUX_PARTS)).astype(F32)
    return jnp.concatenate([key, qry], axis=1).astype(BF16)


def _attn_body(qt_ref, qn_ref, ka_ref, vt_ref, mb_ref, o_ref, m_scr, acc_scr, s_scr, *, tq, hps):
    iq = pl.program_id(2)
    m_scr[...] = jnp.full(m_scr.shape, NEG_BIG, F32)
    acc_scr[...] = jnp.zeros(acc_scr.shape, F32)
    hd = ATTN_HEAD_DIM

    def scores(j, h):
        ks = pl.multiple_of(j * tq, tq)
        ka = ka_ref[0, pl.ds(ks, tq), h * LANES:(h + 1) * LANES]
        return _dot(ka, qt_ref[0, h * LANES:(h + 1) * LANES, :])

    def softmax_pv(j, h, st):
        ks = pl.multiple_of(j * tq, tq)
        vt = vt_ref[0, h * V_ROWS:(h + 1) * V_ROWS, pl.ds(ks, tq)]
        m_prev = m_scr[h]
        m_new = jnp.maximum(m_prev, jnp.max(st, axis=0, keepdims=True))
        alpha = jnp.exp2(m_prev - m_new)
        pt = jnp.exp2((st - m_new).astype(vt.dtype))
        acc_scr[h] = alpha * acc_scr[h] + _dot(vt, pt)
        m_scr[h] = m_new

    @pl.when(iq == 0)
    def _():
        for h in range(hps):
            s_scr[h] = scores(0, h)

    def loop_body(j, carry):
        for h in range(hps):
            s_next = scores(j + 1, h)
            softmax_pv(j, h, s_scr[h])
            s_scr[h] = s_next
        return carry

    lax.fori_loop(0, iq, loop_body, 0)
    for h in range(hps):
        s_next = _dot(ka_ref[0, 0:tq, h * LANES:(h + 1) * LANES], qn_ref[0, h * LANES:(h + 1) * LANES, :])
        softmax_pv(iq, h, s_scr[h] + mb_ref[...])
        s_scr[h] = s_next

    for hp in range(hps // 2):
        outs = []
        for e in range(2):
            acc = acc_scr[2 * hp + e]
            outs.append(acc[0:hd, :] / acc[hd:hd + 1, :])
        o_ref[0, :, hp * LANES:(hp + 1) * LANES] = jnp.concatenate(outs, axis=0).T.astype(o_ref.dtype)


def _attention(qt, ka, vt, b, lp):
    tq, hps = ATTN_TQ, ATTN_HPS
    key_i = jnp.arange(tq)[:, None]
    query_i = jnp.arange(tq)[None, :]
    causal_bias = jnp.where(key_i <= query_i, 0.0, NEG_BIG).astype(F32)
    last_tile = lp // tq - 1
    return pl.pallas_call(
        functools.partial(_attn_body, tq=tq, hps=hps),
        out_shape=jax.ShapeDtypeStruct((b, lp, D_ATTN), BF16),
        grid=(b, ATTN_HEADS // hps, lp // tq),
        in_specs=[pl.BlockSpec((1, hps * LANES, tq), lambda bi, hi, qi: (bi, hi, qi)),
                  pl.BlockSpec((1, hps * LANES, tq), lambda bi, hi, qi: (bi, hi, jnp.minimum(qi + 1, last_tile))),
                  pl.BlockSpec((1, lp, hps * LANES), lambda bi, hi, qi: (bi, 0, hi), pipeline_mode=pl.Buffered(1)),
                  pl.BlockSpec((1, hps * V_ROWS, lp), lambda bi, hi, qi: (bi, hi, 0)),
                  pl.BlockSpec((tq, tq), lambda bi, hi, qi: (0, 0))],
        out_specs=pl.BlockSpec((1, tq, hps * ATTN_HEAD_DIM), lambda bi, hi, qi: (bi, qi, hi)),
        scratch_shapes=[pltpu.VMEM((hps, 1, tq), F32),
                        pltpu.VMEM((hps, V_ROWS, tq), F32),
                        pltpu.VMEM((hps, tq, tq), F32)],
        compiler_params=_cparams(("parallel", "parallel", "arbitrary")),
        name="fox_attention",
    )(qt, qt, ka, vt, causal_bias)


def _causal_conv(x, xp_scr, w_ref, b_ref, first_tile, rows, taps):
    @pl.when(first_tile)
    def _():
        xp_scr[0:SUBLANES, :] = jnp.zeros((SUBLANES, x.shape[1]), F32)

    xp_scr[SUBLANES:SUBLANES + rows, :] = x
    y = b_ref[...] + w_ref[taps - 1:taps, :] * x
    for kk in range(taps - 1):
        r0 = SUBLANES - (taps - 1) + kk
        y = y + w_ref[kk:kk + 1, :] * xp_scr[r0:r0 + rows, :]
    xp_scr[0:SUBLANES, :] = x[rows - SUBLANES:rows, :]
    return y


def _ssd_body(xbc_ref, z_ref, fdt_ref, cw_ref, cb_ref, dtb_ref, alog_ref, dfull_ref, nw_ref,
              exp_ref, o_ref, xp_scr, st_scr):
    rows = SSD_ROWS
    first_tile = pl.program_id(1) == 0

    @pl.when(first_tile)
    def _():
        st_scr[...] = jnp.zeros(st_scr.shape, F32)

    y = _causal_conv(xbc_ref[0].astype(F32), xp_scr, cw_ref, cb_ref, first_tile, rows, SSD_CONV)
    xc = y * _sigmoid(y)
    for ci in range(rows // SSD_CHUNK):
        rs = slice(ci * SSD_CHUNK, (ci + 1) * SSD_CHUNK)
        o_ref[0, rs, :] = _ssd_chunk(xc[rs], z_ref[0, rs, :].astype(F32), fdt_ref[0, rs, :], dtb_ref, alog_ref,
                                     dfull_ref, nw_ref, exp_ref, st_scr).astype(o_ref.dtype)


def _ssd_chunk(xc, z, dt_raw, dtb_ref, alog_ref, dfull_ref, nw_ref, exp_ref, st_scr):
    q = SSD_CHUNK
    gs = D_SSD // SSD_GROUPS
    heads_per_group = SSD_HEADS // SSD_GROUPS

    lane = lax.broadcasted_iota(jnp.int32, (q, LANES), 1)
    dt_lane = (lane >= DT_LANE0) & (lane < DT_LANE0 + SSD_HEADS)
    dt = jnp.where(dt_lane, _softplus(dt_raw + dtb_ref[...]), 0.0)
    a = -jnp.exp(alog_ref[...])
    da = dt * a
    row = lax.broadcasted_iota(jnp.int32, (q, q), 0)
    col = lax.broadcasted_iota(jnp.int32, (q, q), 1)
    lower = row >= col
    a_cum = _dot_01_lhs(lower.astype(BF16), da)
    a_cum_t = a_cum.T
    half = lax.broadcasted_iota(jnp.int32, (q, LANES), 1) < SSD_HEAD_DIM

    outs = []
    for g in range(SSD_GROUPS):
        sl = slice(g * gs, (g + 1) * gs)
        expand = exp_ref[:, sl]
        dt_full = _dot_01_rhs(dt, expand)
        a_cum_full = _dot_01_rhs(a_cum, expand)
        a_last_full = a_cum_full[q - 1:q, :]
        xs = xc[:, sl]
        xdt = xs * dt_full
        xdt_b = xdt.astype(BF16)
        xde_b = (xdt * jnp.exp(a_last_full - a_cum_full)).astype(BF16)

        bm = xc[:, D_SSD + g * SSD_STATE:D_SSD + (g + 1) * SSD_STATE]
        cm = xc[:, D_SSD + SSD_GROUPS * SSD_STATE + g * SSD_STATE:
                D_SSD + SSD_GROUPS * SSD_STATE + (g + 1) * SSD_STATE]
        bm_b = bm.astype(BF16)
        cm_b = cm.astype(BF16)
        cb = _dot_nt(cm_b, bm_b)
        y_pairs = []
        for pair in range(heads_per_group // 2):
            xp = xdt_b[:, pair * LANES:(pair + 1) * LANES]
            ys = []
            for e in range(2):
                hl = DT_LANE0 + g * heads_per_group + 2 * pair + e
                seg = a_cum[:, hl:hl + 1] - a_cum_t[hl:hl + 1, :]
                dec = jnp.exp(jnp.where(lower, seg, -jnp.inf))
                ys.append(_dot((cb * dec).astype(BF16), xp))
            y_pairs.append(jnp.where(half, ys[0], ys[1]))
        y_diag = jnp.concatenate(y_pairs, axis=1)
        prev = st_scr[g]
        y_off = _dot(cm_b, prev.astype(BF16)) * jnp.exp(a_cum_full)
        st_scr[g] = prev * jnp.exp(a_last_full) + _dot(bm.T.astype(BF16), xde_b)
        yg = y_diag + y_off + dfull_ref[:, sl] * xs
        zg = z[:, sl]
        yg = yg * (zg * _sigmoid(zg))
        yg = yg * lax.rsqrt(jnp.mean(yg * yg, axis=-1, keepdims=True) + NORM_EPS)
        outs.append(yg * nw_ref[:, sl])
    return jnp.concatenate(outs, axis=1)


def _ssd(xbc, z, fdt, cw, cb, dtb, alog, dfull, nw, expand, b, lp):
    q = SSD_ROWS
    return pl.pallas_call(
        _ssd_body,
        out_shape=jax.ShapeDtypeStruct((b, lp, D_SSD), BF16),
        grid=(b, lp // q),
        in_specs=[pl.BlockSpec((1, q, D_XBC), lambda bi, ci: (bi, ci, 0)),
                  pl.BlockSpec((1, q, D_SSD), lambda bi, ci: (bi, ci, 0)),
                  pl.BlockSpec((1, q, FDT_COLS), lambda bi, ci: (bi, ci, 0)),
                  _const_spec((SSD_CONV, D_XBC)),
                  _const_spec((1, D_XBC)),
                  _const_spec((1, LANES)),
                  _const_spec((1, LANES)),
                  _const_spec((1, D_SSD)),
                  _const_spec((1, D_SSD)),
                  _const_spec((LANES, D_SSD))],
        out_specs=pl.BlockSpec((1, q, D_SSD), lambda bi, ci: (bi, ci, 0)),
        scratch_shapes=[pltpu.VMEM((SUBLANES + q, D_XBC), F32),
                        pltpu.VMEM((SSD_GROUPS, SSD_STATE, D_SSD // SSD_GROUPS), F32)],
        compiler_params=_cparams(("parallel", "arbitrary")),
        name="ssd",
    )(xbc, z, fdt, cw, cb, dtb, alog, dfull, nw, expand)


def _lru_body(xr_ref, gate_ref, cw_ref, cb_ref, w2_ref, ba_ref, bx_ref, lam_ref, o_ref,
              xp_scr, h_scr):
    rows = SCAN_ROWS
    first_tile = pl.program_id(1) == 0

    @pl.when(first_tile)
    def _():
        h_scr[...] = jnp.zeros(h_scr.shape, F32)

    xc = _causal_conv(xr_ref[0].astype(F32), xp_scr, cw_ref, cb_ref, first_tile, rows, LRU_CONV)
    xc_b = xc.astype(BF16)
    pre = [_dot(xc_b[:, j * LANES:(j + 1) * LANES], w2_ref[j]) for j in range(D_LRU // LANES)]
    pre_a = jnp.concatenate([p[:, :LANES] for p in pre], axis=1)
    pre_x = jnp.concatenate([p[:, LANES:] for p in pre], axis=1)
    r = _sigmoid(pre_a + ba_ref[...])
    i = _sigmoid(pre_x + bx_ref[...])
    log_a = LRU_C * r * _log_sigmoid(lam_ref[...])
    a = jnp.exp(log_a)
    mult = jnp.sqrt(-jnp.tanh(log_a) * (a * a + 1.0))
    row0 = lax.broadcasted_iota(jnp.int32, (SUBLANES, D_LRU), 0) == 0
    mult = jnp.concatenate([jnp.where(first_tile & row0, 1.0, mult[:SUBLANES]), mult[SUBLANES:]], axis=0)
    u = mult * (i * xc)

    groups = rows // SUBLANES
    a3 = a.reshape(groups, SUBLANES, D_LRU)
    u3 = u.reshape(groups, SUBLANES, D_LRU)
    sub = lax.broadcasted_iota(jnp.int32, (groups, SUBLANES, D_LRU), 1)
    d = 1
    while d < SUBLANES:
        keep = sub >= d
        a_s = jnp.where(keep, pltpu.roll(a3, d, 1), 1.0)
        u_s = jnp.where(keep, pltpu.roll(u3, d, 1), 0.0)
        u3 = a3 * u_s + u3
        a3 = a3 * a_s
        d *= 2
    h_prev = h_scr[0:1, :]
    hs = []
    for r in range(groups):
        h_r = a3[r] * h_prev + u3[r]
        hs.append(h_r)
        h_prev = h_r[SUBLANES - 1:SUBLANES, :]
    h = jnp.concatenate(hs, axis=0)
    h_scr[0:1, :] = h[rows - 1:rows, :]
    o_ref[0] = (h * jax.nn.gelu(gate_ref[0].astype(F32))).astype(o_ref.dtype)


def _lru(xr, gate, cw, cb, w2, ba, bx, lam, b, lp):
    rows = SCAN_ROWS
    return pl.pallas_call(
        _lru_body,
        out_shape=jax.ShapeDtypeStruct((b, lp, D_LRU), BF16),
        grid=(b, lp // rows),
        in_specs=[pl.BlockSpec((1, rows, D_LRU), lambda bi, ti: (bi, ti, 0)),
                  pl.BlockSpec((1, rows, D_LRU), lambda bi, ti: (bi, ti, 0)),
                  _const_spec((LRU_CONV, D_LRU)),
                  _const_spec((1, D_LRU)),
                  _const_spec((D_LRU // LANES, LANES, 2 * LANES)),
                  _const_spec((1, D_LRU)),
                  _const_spec((1, D_LRU)),
                  _const_spec((1, D_LRU))],
        out_specs=pl.BlockSpec((1, rows, D_LRU), lambda bi, ti: (bi, ti, 0)),
        scratch_shapes=[pltpu.VMEM((SUBLANES + rows, D_LRU), F32),
                        pltpu.VMEM((SUBLANES, D_LRU), F32)],
        compiler_params=_cparams(("parallel", "arbitrary")),
        name="rglru",
    )(xr, gate, cw, cb, w2, ba, bx, lam)


def _merge_body(h_ref, ya_ref, yb_ref, yc_ref, m_ref, wa_ref, wb_ref, wc_ref, wo_ref, o_ref):
    gate = lambda i: _sigmoid(m_ref[:, i * D_MODEL:(i + 1) * D_MODEL].astype(F32))
    mixed = gate(0) * _dot(ya_ref[...], wa_ref[...])
    mixed = mixed + gate(1) * _dot(yb_ref[...], wb_ref[...])
    mixed = mixed + gate(2) * _dot(yc_ref[...], wc_ref[...])
    o_ref[...] = h_ref[...] + _dot(mixed.astype(BF16), wo_ref[...])


def _merge(h, ya, yb, yc, m, wa, wb, wc, wo):
    t = h.shape[0]
    tm = MERGE_ROWS
    row_spec = lambda width: pl.BlockSpec((tm, width), lambda i: (i, 0))
    return pl.pallas_call(
        _merge_body,
        out_shape=jax.ShapeDtypeStruct((t, D_MODEL), F32),
        grid=(t // tm,),
        in_specs=[row_spec(D_MODEL), row_spec(D_ATTN), row_spec(D_SSD), row_spec(D_LRU),
                  row_spec(N_BRANCH * D_MODEL),
                  _const_spec((D_ATTN, D_MODEL)), _const_spec((D_SSD, D_MODEL)),
                  _const_spec((D_LRU, D_MODEL)), _const_spec((D_MODEL, D_MODEL))],
        out_specs=row_spec(D_MODEL),
        compiler_params=_cparams(("parallel",)),
        name="merge_out",
    )(h, ya, yb, yc, m, wa, wb, wc, wo)


def _prep_w_in(w_in):
    sizes = (D_ATTN, D_ATTN, D_ATTN, ATTN_HEADS, D_SSD, D_XBC, SSD_HEADS, D_LRU, D_LRU, N_BRANCH * D_MODEL)
    offs = [0]
    for s in sizes:
        offs.append(offs[-1] + s)
    part = lambda i: w_in[:, offs[i]:offs[i + 1]]
    q, k, v, f, z, xbc, dt, xr, gate, merge = (part(i) for i in range(10))
    pad = jnp.zeros((D_MODEL, FDT_COLS - ATTN_HEADS - SSD_HEADS), w_in.dtype)
    w = jnp.concatenate([k, z, xbc, xr, gate, merge, f, dt, pad], axis=1).astype(BF16)
    wt = jnp.concatenate([q.T, v.T], axis=0).astype(BF16)
    return w, wt


def _pad_lanes(vec, lane0):
    out = jnp.zeros((1, LANES), F32)
    return out.at[0, lane0:lane0 + vec.shape[0]].set(vec.astype(F32))


def _lru_gate_weights(w_a, w_x):
    def blockdiag_pairs(w):
        w = w.reshape(LRU_BLOCKS // 2, 2, LRU_BLOCK_DIM, LRU_BLOCK_DIM)
        zero = jnp.zeros_like(w[:, 0])
        top = jnp.concatenate([w[:, 0], zero], axis=2)
        bot = jnp.concatenate([zero, w[:, 1]], axis=2)
        return jnp.concatenate([top, bot], axis=1)
    return jnp.concatenate([blockdiag_pairs(w_a), blockdiag_pairs(w_x)], axis=2).astype(BF16)


def _head_expand():
    rows = jnp.arange(LANES)[:, None]
    cols = jnp.arange(D_SSD)[None, :]
    return (rows == DT_LANE0 + cols // SSD_HEAD_DIM).astype(BF16)


def kernel(x, meta_tokens, ffn1_norm, ffn1_w_gate_up, ffn1_w_down, mix_norm, w_in, fox_forget_bias,
           ssd_conv_w, ssd_conv_b, ssd_dt_bias, ssd_a_log, ssd_d, ssd_norm,
           lru_conv_w, lru_conv_b, lru_w_a, lru_b_a, lru_w_x, lru_b_x, lru_lambda,
           w_branch_attn, w_branch_ssd, w_branch_lru, w_out,
           ffn2_norm, ffn2_w_gate_up, ffn2_w_down, final_norm):
    b, s, d = x.shape
    depth = w_in.shape[0]
    length = N_META + s
    lp = -(-length // SEQ_ALIGN) * SEQ_ALIGN
    t = b * lp
    assert d == D_MODEL and t % FFN_ROWS == 0 and t % MERGE_ROWS == 0 and s % FFN_ROWS == 0

    meta = jnp.broadcast_to(meta_tokens.astype(x.dtype)[None], (b, N_META, d))
    h = jnp.concatenate([meta, x, jnp.zeros((b, lp - length, d), x.dtype)], axis=1).reshape(t, d)

    row = lambda vec: vec.astype(F32).reshape(1, -1)
    expand = _head_expand()
    pw = _aux_constants()
    fg = row(final_norm)
    for l in range(depth):
        h = _ffn(h, row(ffn1_norm[l]), ffn1_w_gate_up[l, :, :D_FF].astype(BF16),
                 ffn1_w_gate_up[l, :, D_FF:].astype(BF16), ffn1_w_down[l].astype(BF16), fg, False)

        w_std, w_t = _prep_w_in(w_in[l])
        qt, ka, vt, z, xbc, xr, gate, merge, fdt = _inproj(
            h, row(mix_norm[l]), w_std, w_t, _pad_lanes(fox_forget_bias[l], 0), pw, b, lp)
        y_a = _attention(qt, ka, vt, b, lp).reshape(t, D_ATTN)
        y_b = _ssd(xbc, z, fdt, ssd_conv_w[l].astype(F32), row(ssd_conv_b[l]),
                   _pad_lanes(ssd_dt_bias[l], DT_LANE0), _pad_lanes(ssd_a_log[l], DT_LANE0),
                   row(jnp.repeat(ssd_d[l], SSD_HEAD_DIM)), row(ssd_norm[l]), expand, b, lp).reshape(t, D_SSD)
        y_c = _lru(xr, gate, lru_conv_w[l].astype(F32), row(lru_conv_b[l]),
                   _lru_gate_weights(lru_w_a[l], lru_w_x[l]), row(lru_b_a[l]), row(lru_b_x[l]),
                   row(lru_lambda[l]), b, lp).reshape(t, D_LRU)
        h = _merge(h, y_a, y_b, y_c, merge.reshape(t, N_BRANCH * D_MODEL), w_branch_attn[l].astype(BF16),
                   w_branch_ssd[l].astype(BF16), w_branch_lru[l].astype(BF16), w_out[l].astype(BF16))

        ffn2 = (row(ffn2_norm[l]), ffn2_w_gate_up[l, :, :D_FF].astype(BF16),
                ffn2_w_gate_up[l, :, D_FF:].astype(BF16), ffn2_w_down[l].astype(BF16), fg)
        if l < depth - 1:
            h = _ffn(h, *ffn2, False)
    return _ffn_final(h, *ffn2, b, lp, s)
```

```python
import functools

import jax
import jax.numpy as jnp
from jax import lax
from jax.experimental import pallas as pl
from jax.experimental.pallas import tpu as pltpu

F32 = jnp.float32
BF16 = jnp.bfloat16

D_MODEL = 1024
N_META = 16
SSD_CHUNK = 128
NORM_EPS = 1e-6
ATTN_HEADS = 16
ATTN_HEAD_DIM = 64
D_ATTN = ATTN_HEADS * ATTN_HEAD_DIM
SSD_HEAD_DIM = 64
D_SSD = D_MODEL
SSD_HEADS = D_SSD // SSD_HEAD_DIM
SSD_GROUPS = 2
SSD_STATE = 128
SSD_CONV = 4
D_XBC = D_SSD + 2 * SSD_GROUPS * SSD_STATE
D_LRU = D_MODEL
LRU_BLOCKS = 16
LRU_BLOCK_DIM = D_LRU // LRU_BLOCKS
LRU_CONV = 4
LRU_C = 8.0
D_FF = 2816
N_BRANCH = 3

LANES = 128
SUBLANES = 8
VMEM_LIMIT_BYTES = 56 * 1024 * 1024

SEQ_ALIGN = 256
FFN_ROWS = 1024
FFN_CHUNK = 256
PROJ_ROWS = 256
PROJ_CHUNK = 512
ATTN_TQ = 256
ATTN_HPS = 16
MERGE_ROWS = 1024
SCAN_ROWS = 256
SSD_ROWS = 256
FDT_COLS = LANES
DT_LANE0 = ATTN_HEADS
NEG_BIG = -1e30
LOG2E = 1.4426950408889634
Q_SCALE = ATTN_HEAD_DIM ** -0.5 * LOG2E
AUX_PARTS = 3
AUX_SLOTS = 8


def _cparams(sem):
    return pltpu.CompilerParams(dimension_semantics=sem, vmem_limit_bytes=VMEM_LIMIT_BYTES)


def _const_spec(shape):
    nd = len(shape)
    return pl.BlockSpec(shape, lambda *_: (0,) * nd, pipeline_mode=pl.Buffered(1))


def _rms(x, g):
    ms = jnp.mean(x * x, axis=-1, keepdims=True)
    return (x * lax.rsqrt(ms + NORM_EPS)) * g


def _dot(a, b):
    return jnp.dot(a, b, preferred_element_type=F32)


def _dot_nt(a, b):
    return lax.dot_general(a, b, (((1,), (1,)), ((), ())), preferred_element_type=F32)


def _split3(x):
    hi = x.astype(BF16)
    r1 = x - hi.astype(F32)
    mid = r1.astype(BF16)
    lo = (r1 - mid.astype(F32)).astype(BF16)
    return hi, mid, lo


def _dot_01_lhs(sel, x):
    hi, mid, lo = _split3(x)
    return _dot(sel, hi) + _dot(sel, mid) + _dot(sel, lo)


def _dot_01_rhs(x, sel):
    hi, mid, lo = _split3(x)
    return _dot(hi, sel) + _dot(mid, sel) + _dot(lo, sel)


def _log_sigmoid(x):
    return -(jnp.maximum(-x, 0.0) + jnp.log1p(jnp.exp(-jnp.abs(x))))


def _softplus(x):
    return jnp.maximum(x, 0.0) + jnp.log1p(jnp.exp(-jnp.abs(x)))


def _sigmoid(x):
    return 1.0 / (1.0 + jnp.exp(-x))


def _ffn_body(x_ref, g_ref, wg_ref, wu_ref, wd_ref, fg_ref, o_ref, a_scr, *, final_norm):
    x = x_ref[...]
    hn = _rms(x, g_ref[...]).astype(BF16)
    for c0 in range(0, D_FF, FFN_CHUNK):
        gate = _dot(hn, wg_ref[:, c0:c0 + FFN_CHUNK])
        up = _dot(hn, wu_ref[:, c0:c0 + FFN_CHUNK])
        a_scr[:, c0:c0 + FFN_CHUNK] = ((gate * _sigmoid(gate)) * up).astype(BF16)
    y = x + 0.5 * _dot(a_scr[...], wd_ref[...])
    if final_norm:
        y = _rms(y, fg_ref[...])
    o_ref[...] = y


def _ffn(x, g, wg, wu, wd, fg, final_norm):
    t = x.shape[0]
    tm = FFN_ROWS
    return pl.pallas_call(
        functools.partial(_ffn_body, final_norm=final_norm),
        out_shape=jax.ShapeDtypeStruct((t, D_MODEL), F32),
        grid=(t // tm,),
        in_specs=[pl.BlockSpec((tm, D_MODEL), lambda i: (i, 0))] + _ffn_weight_specs(),
        out_specs=pl.BlockSpec((tm, D_MODEL), lambda i: (i, 0)),
        scratch_shapes=[pltpu.VMEM((tm, D_FF), BF16)],
        compiler_params=_cparams(("parallel",)),
        name="ffn",
    )(x, g, wg, wu, wd, fg)


def _ffn_weight_specs():
    return [_const_spec((1, D_MODEL)), _const_spec((D_MODEL, D_FF)), _const_spec((D_MODEL, D_FF)),
            _const_spec((D_FF, D_MODEL)), _const_spec((1, D_MODEL))]


def _ffn_final_body(x_ref, g_ref, wg_ref, wu_ref, wd_ref, fg_ref, o_ref, a_scr):
    _ffn_body(x_ref.at[0], g_ref, wg_ref, wu_ref, wd_ref, fg_ref, o_ref.at[0], a_scr, final_norm=True)


def _ffn_final(x, g, wg, wu, wd, fg, b, lp, s_out):
    tm = FFN_ROWS
    return pl.pallas_call(
        _ffn_final_body,
        out_shape=jax.ShapeDtypeStruct((b, s_out, D_MODEL), F32),
        grid=(b, s_out // tm),
        in_specs=[pl.BlockSpec((pl.Element(1), pl.Element(tm), pl.Element(D_MODEL)),
                               lambda bi, i: (bi, pl.multiple_of(N_META + i * tm, SUBLANES), 0))]
        + _ffn_weight_specs(),
        out_specs=pl.BlockSpec((1, tm, D_MODEL), lambda bi, i: (bi, i, 0)),
        scratch_shapes=[pltpu.VMEM((tm, D_FF), BF16)],
        compiler_params=_cparams(("parallel", "parallel")),
        name="ffn_final",
    )(x.reshape(b, lp, D_MODEL), g, wg, wu, wd, fg)


_PLAIN_GROUPS = (("z", D_SSD), ("xbc", D_XBC), ("xr", D_LRU), ("gate", D_LRU), ("merge", N_BRANCH * D_MODEL))
OFF_K = 0
OFF_PLAIN = D_ATTN
OFF_FDT = OFF_PLAIN + sum(w for _, w in _PLAIN_GROUPS)
N_PROJ = OFF_FDT + FDT_COLS
D_AUG = ATTN_HEADS * LANES
V_ROWS = ATTN_HEAD_DIM + 16
ONE_LANE = LANES - 1


def _inproj_body(x_ref, g_ref, w_ref, wt_ref, fb_ref, pw_ref, qt_ref, ka_ref, vt_ref,
                 z_ref, xbc_ref, xr_ref, gate_ref, merge_ref, fdt_ref, carry_scr):
    tm = x_ref.shape[1]

    @pl.when(pl.program_id(1) == 0)
    def _():
        carry_scr[...] = jnp.zeros(carry_scr.shape, F32)

    hn = _rms(x_ref[0], g_ref[...]).astype(BF16)

    def mm(c0, width):
        return _dot(hn, w_ref[:, c0:c0 + width])

    fdt = mm(OFF_FDT, FDT_COLS)
    fdt_ref[0] = fdt
    off = OFF_PLAIN
    for (_, width), o_ref in zip(_PLAIN_GROUPS, (z_ref, xbc_ref, xr_ref, gate_ref, merge_ref)):
        for c0 in range(0, width, PROJ_CHUNK):
            o_ref[0, :, c0:c0 + PROJ_CHUNK] = mm(off + c0, PROJ_CHUNK).astype(o_ref.dtype)
        off += width

    lane = lax.broadcasted_iota(jnp.int32, (tm, LANES), 1)
    lf = jnp.where(lane < ATTN_HEADS, _log_sigmoid(fdt + fb_ref[...]), 0.0)
    row = lax.broadcasted_iota(jnp.int32, (tm, tm), 0)
    col = lax.broadcasted_iota(jnp.int32, (tm, tm), 1)
    c = _dot_01_lhs((row >= col).astype(BF16), lf) + carry_scr[0:1, :]
    carry_scr[0:1, :] = c[tm - 1:tm, :]
    hi, mid, lo = (part.astype(F32) for part in _split3(c * LOG2E))
    cparts = (hi + pltpu.roll(mid, ATTN_HEADS, 1) + pltpu.roll(lo, 2 * ATTN_HEADS, 1)
              + jnp.where(lane == ONE_LANE, 1.0, 0.0))

    w_aux = _dot(cparts.astype(BF16), pw_ref[...])
    k_aux = w_aux[:, :LANES]
    q_aux_t = w_aux[:, LANES:].T

    first = lane < ATTN_HEAD_DIM
    for c0 in range(0, D_ATTN, PROJ_CHUNK):
        kv = mm(OFF_K + c0, PROJ_CHUNK)
        for pr in range(PROJ_CHUNK // LANES):
            pair = c0 // LANES + pr
            k_data = kv[:, pr * LANES:(pr + 1) * LANES]
            ka_ref[0, :, 2 * pair * LANES:(2 * pair + 1) * LANES] = jnp.where(
                first, k_data, k_aux).astype(ka_ref.dtype)
            ka_ref[0, :, (2 * pair + 1) * LANES:(2 * pair + 2) * LANES] = jnp.where(
                first, k_aux, k_data).astype(ka_ref.dtype)

    hd = ATTN_HEAD_DIM
    q_t = _dot_nt(wt_ref[0:D_ATTN, :], hn) * Q_SCALE
    v_t = _dot_nt(wt_ref[D_ATTN:2 * D_ATTN, :], hn)
    ones = jnp.ones((V_ROWS - hd, tm), vt_ref.dtype)
    for h in range(ATTN_HEADS):
        even = h % 2 == 0
        data0 = h * LANES + (0 if even else hd)
        aux0 = h * LANES + (hd if even else 0)
        a0 = (hd if even else 0) + AUX_SLOTS * (h // 2)
        before, after = AUX_SLOTS * (h // 2), hd - AUX_SLOTS * (h // 2 + 1)
        pieces = ([jnp.zeros((before, tm), F32)] if before else []) + [q_aux_t[a0:a0 + AUX_SLOTS, :]]
        pieces += [jnp.zeros((after, tm), F32)] if after else []
        qt_ref[0, data0:data0 + hd, :] = q_t[h * hd:(h + 1) * hd, :].astype(qt_ref.dtype)
        qt_ref[0, aux0:aux0 + hd, :] = jnp.concatenate(pieces, axis=0).astype(qt_ref.dtype)
        vt_ref[0, h * V_ROWS:h * V_ROWS + hd, :] = v_t[h * hd:(h + 1) * hd, :].astype(vt_ref.dtype)
        vt_ref[0, h * V_ROWS + hd:(h + 1) * V_ROWS, :] = ones


def _inproj(x, g, w, wt, fb, pw, b, lp):
    tm = PROJ_ROWS
    row_spec = lambda width: pl.BlockSpec((1, tm, width), lambda bi, ti: (bi, ti, 0))
    col_spec = lambda height: pl.BlockSpec((1, height, tm), lambda bi, ti: (bi, 0, ti))
    plain = [w_ for _, w_ in _PLAIN_GROUPS]
    out_shape = ([jax.ShapeDtypeStruct((b, D_AUG, lp), BF16), jax.ShapeDtypeStruct((b, lp, D_AUG), BF16),
                  jax.ShapeDtypeStruct((b, ATTN_HEADS * V_ROWS, lp), BF16)]
                 + [jax.ShapeDtypeStruct((b, lp, w_), BF16) for w_ in plain]
                 + [jax.ShapeDtypeStruct((b, lp, FDT_COLS), F32)])
    return pl.pallas_call(
        _inproj_body,
        out_shape=out_shape,
        grid=(b, lp // tm),
        in_specs=[row_spec(D_MODEL),
                  _const_spec((1, D_MODEL)),
                  _const_spec((D_MODEL, N_PROJ)),
                  _const_spec((2 * D_ATTN, D_MODEL)),
                  _const_spec((1, LANES)),
                  _const_spec((LANES, 2 * LANES))],
        out_specs=([col_spec(D_AUG), row_spec(D_AUG), col_spec(ATTN_HEADS * V_ROWS)]
                   + [row_spec(w_) for w_ in plain] + [row_spec(FDT_COLS)]),
        scratch_shapes=[pltpu.VMEM((SUBLANES, LANES), F32)],
        compiler_params=_cparams(("parallel", "arbitrary")),
        name="inproj",
    )(x.reshape(b, lp, D_MODEL), g, w, wt, fb, pw)


def _aux_constants():
    src = jnp.arange(LANES)[:, None]
    lane = jnp.arange(LANES)[None, :]
    half = jnp.where(lane >= ATTN_HEAD_DIM, 0, 1)
    slot = lane % ATTN_HEAD_DIM
    head = 2 * (slot // AUX_SLOTS) + half
    idx = slot % AUX_SLOTS
    part = src // ATTN_HEADS
    part_src = (part < AUX_PARTS) & (src % ATTN_HEADS == head)
    one_src = src == ONE_LANE
    key = (one_src & (idx < AUX_PARTS)).astype(F32) - (part_src & (idx == part + AUX_PARTS)).astype(F32)
    qry = (part_src & (idx == part)).astype(F32) + (one_src & (idx >= AUX_PARTS) & (idx < 2 * AUX_PARTS)).astype(F32)
    return jnp.concatenate([key, qry], axis=1).astype(BF16)


def _attn_body(qt_ref, qn_ref, ka_hbm, vt_ref, mb_ref, o_ref, ka_ref, k_sem, m_scr, acc_scr, s_scr, *, tq, hps):
    bi = pl.program_id(0)
    hi = pl.program_id(1)
    iq = pl.program_id(2)

    def key_copy(blk, slot):
        rows = pl.ds(pl.multiple_of(blk * tq, tq), tq)
        return pltpu.make_async_copy(ka_hbm.at[bi, rows, pl.ds(pl.multiple_of(hi * hps * LANES, LANES), hps * LANES)],
                                     ka_ref.at[rows, :], k_sem.at[slot])

    @pl.when(iq == 0)
    def _():
        first_block = key_copy(0, 0)
        first_block.start()
        first_block.wait()

    @pl.when(iq > 0)
    def _():
        key_copy(iq, 1).wait()

    @pl.when(iq + 1 < pl.num_programs(2))
    def _():
        key_copy(iq + 1, 1).start()

    m_scr[...] = jnp.full(m_scr.shape, NEG_BIG, F32)
    acc_scr[...] = jnp.zeros(acc_scr.shape, F32)
    hd = ATTN_HEAD_DIM

    def scores(j, h):
        ks = pl.multiple_of(j * tq, tq)
        ka = ka_ref[pl.ds(ks, tq), h * LANES:(h + 1) * LANES]
        return _dot(ka, qt_ref[0, h * LANES:(h + 1) * LANES, :])

    def softmax_pv(j, h, st):
        ks = pl.multiple_of(j * tq, tq)
        vt = vt_ref[0, h * V_ROWS:(h + 1) * V_ROWS, pl.ds(ks, tq)]
        m_prev = m_scr[h]
        m_new = jnp.maximum(m_prev, jnp.max(st, axis=0, keepdims=True))
        alpha = jnp.exp2(m_prev - m_new)
        pt = jnp.exp2((st - m_new).astype(vt.dtype))
        acc_scr[h] = alpha * acc_scr[h] + _dot(vt, pt)
        m_scr[h] = m_new

    @pl.when(iq == 0)
    def _():
        for h in range(hps):
            s_scr[h] = scores(0, h)

    def loop_body(j, carry):
        for h in range(hps):
            s_next = scores(j + 1, h)
            softmax_pv(j, h, s_scr[h])
            s_scr[h] = s_next
        return carry

    lax.fori_loop(0, iq, loop_body, 0)
    for h in range(hps):
        s_next = _dot(ka_ref[0:tq, h * LANES:(h + 1) * LANES], qn_ref[0, h * LANES:(h + 1) * LANES, :])
        softmax_pv(iq, h, s_scr[h] + mb_ref[...])
        s_scr[h] = s_next

    for hp in range(hps // 2):
        outs = []
        for e in range(2):
            acc = acc_scr[2 * hp + e]
            outs.append(acc[0:hd, :] / acc[hd:hd + 1, :])
        o_ref[0, :, hp * LANES:(hp + 1) * LANES] = jnp.concatenate(outs, axis=0).T.astype(o_ref.dtype)


def _attention(qt, ka, vt, b, lp):
    tq, hps = ATTN_TQ, ATTN_HPS
    key_i = jnp.arange(tq)[:, None]
    query_i = jnp.arange(tq)[None, :]
    causal_bias = jnp.where(key_i <= query_i, 0.0, NEG_BIG).astype(F32)
    last_tile = lp // tq - 1
    return pl.pallas_call(
        functools.partial(_attn_body, tq=tq, hps=hps),
        out_shape=jax.ShapeDtypeStruct((b, lp, D_ATTN), BF16),
        grid=(b, ATTN_HEADS // hps, lp // tq),
        in_specs=[pl.BlockSpec((1, hps * LANES, tq), lambda bi, hi, qi: (bi, hi, qi)),
                  pl.BlockSpec((1, hps * LANES, tq), lambda bi, hi, qi: (bi, hi, jnp.minimum(qi + 1, last_tile))),
                  pl.BlockSpec(memory_space=pl.ANY),
                  pl.BlockSpec((1, hps * V_ROWS, lp), lambda bi, hi, qi: (bi, hi, 0)),
                  pl.BlockSpec((tq, tq), lambda bi, hi, qi: (0, 0))],
        out_specs=pl.BlockSpec((1, tq, hps * ATTN_HEAD_DIM), lambda bi, hi, qi: (bi, qi, hi)),
        scratch_shapes=[pltpu.VMEM((lp, hps * LANES), BF16),
                        pltpu.SemaphoreType.DMA((2,)),
                        pltpu.VMEM((hps, 1, tq), F32),
                        pltpu.VMEM((hps, V_ROWS, tq), F32),
                        pltpu.VMEM((hps, tq, tq), F32)],
        compiler_params=_cparams(("parallel", "parallel", "arbitrary")),
        name="fox_attention",
    )(qt, qt, ka, vt, causal_bias)


def _causal_conv(x, xp_scr, w_ref, b_ref, first_tile, rows, taps):
    @pl.when(first_tile)
    def _():
        xp_scr[0:SUBLANES, :] = jnp.zeros((SUBLANES, x.shape[1]), F32)

    xp_scr[SUBLANES:SUBLANES + rows, :] = x
    y = b_ref[...] + w_ref[taps - 1:taps, :] * x
    for kk in range(taps - 1):
        r0 = SUBLANES - (taps - 1) + kk
        y = y + w_ref[kk:kk + 1, :] * xp_scr[r0:r0 + rows, :]
    xp_scr[0:SUBLANES, :] = x[rows - SUBLANES:rows, :]
    return y


def _ssd_body(xbc_ref, z_ref, fdt_ref, cw_ref, cb_ref, dtb_ref, alog_ref, dfull_ref, nw_ref,
              exp_ref, o_ref, xp_scr, st_scr):
    rows = SSD_ROWS
    first_tile = pl.program_id(1) == 0

    @pl.when(first_tile)
    def _():
        st_scr[...] = jnp.zeros(st_scr.shape, F32)

    y = _causal_conv(xbc_ref[0].astype(F32), xp_scr, cw_ref, cb_ref, first_tile, rows, SSD_CONV)
    xc = y * _sigmoid(y)
    for ci in range(rows // SSD_CHUNK):
        rs = slice(ci * SSD_CHUNK, (ci + 1) * SSD_CHUNK)
        o_ref[0, rs, :] = _ssd_chunk(xc[rs], z_ref[0, rs, :].astype(F32), fdt_ref[0, rs, :], dtb_ref, alog_ref,
                                     dfull_ref, nw_ref, exp_ref, st_scr).astype(o_ref.dtype)


def _ssd_chunk(xc, z, dt_raw, dtb_ref, alog_ref, dfull_ref, nw_ref, exp_ref, st_scr):
    q = SSD_CHUNK
    gs = D_SSD // SSD_GROUPS
    heads_per_group = SSD_HEADS // SSD_GROUPS

    lane = lax.broadcasted_iota(jnp.int32, (q, LANES), 1)
    dt_lane = (lane >= DT_LANE0) & (lane < DT_LANE0 + SSD_HEADS)
    dt = jnp.where(dt_lane, _softplus(dt_raw + dtb_ref[...]), 0.0)
    a = -jnp.exp(alog_ref[...])
    da = dt * a
    row = lax.broadcasted_iota(jnp.int32, (q, q), 0)
    col = lax.broadcasted_iota(jnp.int32, (q, q), 1)
    lower = row >= col
    a_cum = _dot_01_lhs(lower.astype(BF16), da)
    a_cum_t = a_cum.T
    half = lax.broadcasted_iota(jnp.int32, (q, LANES), 1) < SSD_HEAD_DIM

    outs = []
    for g in range(SSD_GROUPS):
        sl = slice(g * gs, (g + 1) * gs)
        expand = exp_ref[:, sl]
        dt_full = _dot_01_rhs(dt, expand)
        a_cum_full = _dot_01_rhs(a_cum, expand)
        a_last_full = a_cum_full[q - 1:q, :]
        xs = xc[:, sl]
        xdt = xs * dt_full
        xdt_b = xdt.astype(BF16)
        xde_b = (xdt * jnp.exp(a_last_full - a_cum_full)).astype(BF16)

        bm = xc[:, D_SSD + g * SSD_STATE:D_SSD + (g + 1) * SSD_STATE]
        cm = xc[:, D_SSD + SSD_GROUPS * SSD_STATE + g * SSD_STATE:
                D_SSD + SSD_GROUPS * SSD_STATE + (g + 1) * SSD_STATE]
        bm_b = bm.astype(BF16)
        cm_b = cm.astype(BF16)
        cb = _dot_nt(cm_b, bm_b)
        y_pairs = []
        for pair in range(heads_per_group // 2):
            xp = xdt_b[:, pair * LANES:(pair + 1) * LANES]
            ys = []
            for e in range(2):
                hl = DT_LANE0 + g * heads_per_group + 2 * pair + e
                seg = a_cum[:, hl:hl + 1] - a_cum_t[hl:hl + 1, :]
                dec = jnp.exp(jnp.where(lower, seg, -jnp.inf))
                ys.append(_dot((cb * dec).astype(BF16), xp))
            y_pairs.append(jnp.where(half, ys[0], ys[1]))
        y_diag = jnp.concatenate(y_pairs, axis=1)
        prev = st_scr[g]
        y_off = _dot(cm_b, prev.astype(BF16)) * jnp.exp(a_cum_full)
        st_scr[g] = prev * jnp.exp(a_last_full) + _dot(bm.T.astype(BF16), xde_b)
        yg = y_diag + y_off + dfull_ref[:, sl] * xs
        zg = z[:, sl]
        yg = yg * (zg * _sigmoid(zg))
        yg = yg * lax.rsqrt(jnp.mean(yg * yg, axis=-1, keepdims=True) + NORM_EPS)
        outs.append(yg * nw_ref[:, sl])
    return jnp.concatenate(outs, axis=1)


def _ssd(xbc, z, fdt, cw, cb, dtb, alog, dfull, nw, expand, b, lp):
    q = SSD_ROWS
    return pl.pallas_call(
        _ssd_body,
        out_shape=jax.ShapeDtypeStruct((b, lp, D_SSD), BF16),
        grid=(b, lp // q),
        in_specs=[pl.BlockSpec((1, q, D_XBC), lambda bi, ci: (bi, ci, 0)),
                  pl.BlockSpec((1, q, D_SSD), lambda bi, ci: (bi, ci, 0)),
                  pl.BlockSpec((1, q, FDT_COLS), lambda bi, ci: (bi, ci, 0)),
                  _const_spec((SSD_CONV, D_XBC)),
                  _const_spec((1, D_XBC)),
                  _const_spec((1, LANES)),
                  _const_spec((1, LANES)),
                  _const_spec((1, D_SSD)),
                  _const_spec((1, D_SSD)),
                  _const_spec((LANES, D_SSD))],
        out_specs=pl.BlockSpec((1, q, D_SSD), lambda bi, ci: (bi, ci, 0)),
        scratch_shapes=[pltpu.VMEM((SUBLANES + q, D_XBC), F32),
                        pltpu.VMEM((SSD_GROUPS, SSD_STATE, D_SSD // SSD_GROUPS), F32)],
        compiler_params=_cparams(("parallel", "arbitrary")),
        name="ssd",
    )(xbc, z, fdt, cw, cb, dtb, alog, dfull, nw, expand)


def _lru_body(xr_ref, gate_ref, cw_ref, cb_ref, w2_ref, ba_ref, bx_ref, lam_ref, o_ref,
              xp_scr, h_scr):
    rows = SCAN_ROWS
    first_tile = pl.program_id(1) == 0

    @pl.when(first_tile)
    def _():
        h_scr[...] = jnp.zeros(h_scr.shape, F32)

    xc = _causal_conv(xr_ref[0].astype(F32), xp_scr, cw_ref, cb_ref, first_tile, rows, LRU_CONV)
    xc_b = xc.astype(BF16)
    pre = [_dot(xc_b[:, j * LANES:(j + 1) * LANES], w2_ref[j]) for j in range(D_LRU // LANES)]
    pre_a = jnp.concatenate([p[:, :LANES] for p in pre], axis=1)
    pre_x = jnp.concatenate([p[:, LANES:] for p in pre], axis=1)
    r = _sigmoid(pre_a + ba_ref[...])
    i = _sigmoid(pre_x + bx_ref[...])
    log_a = LRU_C * r * _log_sigmoid(lam_ref[...])
    a = jnp.exp(log_a)
    mult = jnp.sqrt(-jnp.tanh(log_a) * (a * a + 1.0))
    row0 = lax.broadcasted_iota(jnp.int32, (SUBLANES, D_LRU), 0) == 0
    mult = jnp.concatenate([jnp.where(first_tile & row0, 1.0, mult[:SUBLANES]), mult[SUBLANES:]], axis=0)
    u = mult * (i * xc)

    groups = rows // SUBLANES
    a3 = a.reshape(groups, SUBLANES, D_LRU)
    u3 = u.reshape(groups, SUBLANES, D_LRU)
    sub = lax.broadcasted_iota(jnp.int32, (groups, SUBLANES, D_LRU), 1)
    d = 1
    while d < SUBLANES:
        keep = sub >= d
        a_s = jnp.where(keep, pltpu.roll(a3, d, 1), 1.0)
        u_s = jnp.where(keep, pltpu.roll(u3, d, 1), 0.0)
        u3 = a3 * u_s + u3
        a3 = a3 * a_s
        d *= 2
    h_prev = h_scr[0:1, :]
    hs = []
    for r in range(groups):
        h_r = a3[r] * h_prev + u3[r]
        hs.append(h_r)
        h_prev = h_r[SUBLANES - 1:SUBLANES, :]
    h = jnp.concatenate(hs, axis=0)
    h_scr[0:1, :] = h[rows - 1:rows, :]
    o_ref[0] = (h * jax.nn.gelu(gate_ref[0].astype(F32))).astype(o_ref.dtype)


def _lru(xr, gate, cw, cb, w2, ba, bx, lam, b, lp):
    rows = SCAN_ROWS
    return pl.pallas_call(
        _lru_body,
        out_shape=jax.ShapeDtypeStruct((b, lp, D_LRU), BF16),
        grid=(b, lp // rows),
        in_specs=[pl.BlockSpec((1, rows, D_LRU), lambda bi, ti: (bi, ti, 0)),
                  pl.BlockSpec((1, rows, D_LRU), lambda bi, ti: (bi, ti, 0)),
                  _const_spec((LRU_CONV, D_LRU)),
                  _const_spec((1, D_LRU)),
                  _const_spec((D_LRU // LANES, LANES, 2 * LANES)),
                  _const_spec((1, D_LRU)),
                  _const_spec((1, D_LRU)),
                  _const_spec((1, D_LRU))],
        out_specs=pl.BlockSpec((1, rows, D_LRU), lambda bi, ti: (bi, ti, 0)),
        scratch_shapes=[pltpu.VMEM((SUBLANES + rows, D_LRU), F32),
                        pltpu.VMEM((SUBLANES, D_LRU), F32)],
        compiler_params=_cparams(("parallel", "arbitrary")),
        name="rglru",
    )(xr, gate, cw, cb, w2, ba, bx, lam)


def _merge_body(h_ref, ya_ref, yb_ref, yc_ref, m_ref, wa_ref, wb_ref, wc_ref, wo_ref, o_ref):
    gate = lambda i: _sigmoid(m_ref[:, i * D_MODEL:(i + 1) * D_MODEL].astype(F32))
    mixed = gate(0) * _dot(ya_ref[...], wa_ref[...])
    mixed = mixed + gate(1) * _dot(yb_ref[...], wb_ref[...])
    mixed = mixed + gate(2) * _dot(yc_ref[...], wc_ref[...])
    o_ref[...] = h_ref[...] + _dot(mixed.astype(BF16), wo_ref[...])


def _merge(h, ya, yb, yc, m, wa, wb, wc, wo):
    t = h.shape[0]
    tm = MERGE_ROWS
    row_spec = lambda width: pl.BlockSpec((tm, width), lambda i: (i, 0))
    return pl.pallas_call(
        _merge_body,
        out_shape=jax.ShapeDtypeStruct((t, D_MODEL), F32),
        grid=(t // tm,),
        in_specs=[row_spec(D_MODEL), row_spec(D_ATTN), row_spec(D_SSD), row_spec(D_LRU),
                  row_spec(N_BRANCH * D_MODEL),
                  _const_spec((D_ATTN, D_MODEL)), _const_spec((D_SSD, D_MODEL)),
                  _const_spec((D_LRU, D_MODEL)), _const_spec((D_MODEL, D_MODEL))],
        out_specs=row_spec(D_MODEL),
        compiler_params=_cparams(("parallel",)),
        name="merge_out",
    )(h, ya, yb, yc, m, wa, wb, wc, wo)


def _prep_w_in(w_in):
    sizes = (D_ATTN, D_ATTN, D_ATTN, ATTN_HEADS, D_SSD, D_XBC, SSD_HEADS, D_LRU, D_LRU, N_BRANCH * D_MODEL)
    offs = [0]
    for s in sizes:
        offs.append(offs[-1] + s)
    part = lambda i: w_in[:, offs[i]:offs[i + 1]]
    q, k, v, f, z, xbc, dt, xr, gate, merge = (part(i) for i in range(10))
    pad = jnp.zeros((D_MODEL, FDT_COLS - ATTN_HEADS - SSD_HEADS), w_in.dtype)
    w = jnp.concatenate([k, z, xbc, xr, gate, merge, f, dt, pad], axis=1).astype(BF16)
    wt = jnp.concatenate([q.T, v.T], axis=0).astype(BF16)
    return w, wt


def _pad_lanes(vec, lane0):
    out = jnp.zeros((1, LANES), F32)
    return out.at[0, lane0:lane0 + vec.shape[0]].set(vec.astype(F32))


def _lru_gate_weights(w_a, w_x):
    def blockdiag_pairs(w):
        w = w.reshape(LRU_BLOCKS // 2, 2, LRU_BLOCK_DIM, LRU_BLOCK_DIM)
        zero = jnp.zeros_like(w[:, 0])
        top = jnp.concatenate([w[:, 0], zero], axis=2)
        bot = jnp.concatenate([zero, w[:, 1]], axis=2)
        return jnp.concatenate([top, bot], axis=1)
    return jnp.concatenate([blockdiag_pairs(w_a), blockdiag_pairs(w_x)], axis=2).astype(BF16)


def _head_expand():
    rows = jnp.arange(LANES)[:, None]
    cols = jnp.arange(D_SSD)[None, :]
    return (rows == DT_LANE0 + cols // SSD_HEAD_DIM).astype(BF16)


def kernel(x, meta_tokens, ffn1_norm, ffn1_w_gate_up, ffn1_w_down, mix_norm, w_in, fox_forget_bias,
           ssd_conv_w, ssd_conv_b, ssd_dt_bias, ssd_a_log, ssd_d, ssd_norm,
           lru_conv_w, lru_conv_b, lru_w_a, lru_b_a, lru_w_x, lru_b_x, lru_lambda,
           w_branch_attn, w_branch_ssd, w_branch_lru, w_out,
           ffn2_norm, ffn2_w_gate_up, ffn2_w_down, final_norm):
    b, s, d = x.shape
    depth = w_in.shape[0]
    length = N_META + s
    lp = -(-length // SEQ_ALIGN) * SEQ_ALIGN
    t = b * lp
    assert d == D_MODEL and t % FFN_ROWS == 0 and t % MERGE_ROWS == 0 and s % FFN_ROWS == 0

    meta = jnp.broadcast_to(meta_tokens.astype(x.dtype)[None], (b, N_META, d))
    h = jnp.concatenate([meta, x, jnp.zeros((b, lp - length, d), x.dtype)], axis=1).reshape(t, d)

    row = lambda vec: vec.astype(F32).reshape(1, -1)
    expand = _head_expand()
    pw = _aux_constants()
    fg = row(final_norm)
    for l in range(depth):
        h = _ffn(h, row(ffn1_norm[l]), ffn1_w_gate_up[l, :, :D_FF].astype(BF16),
                 ffn1_w_gate_up[l, :, D_FF:].astype(BF16), ffn1_w_down[l].astype(BF16), fg, False)

        w_std, w_t = _prep_w_in(w_in[l])
        qt, ka, vt, z, xbc, xr, gate, merge, fdt = _inproj(
            h, row(mix_norm[l]), w_std, w_t, _pad_lanes(fox_forget_bias[l], 0), pw, b, lp)
        y_a = _attention(qt, ka, vt, b, lp).reshape(t, D_ATTN)
        y_b = _ssd(xbc, z, fdt, ssd_conv_w[l].astype(F32), row(ssd_conv_b[l]),
                   _pad_lanes(ssd_dt_bias[l], DT_LANE0), _pad_lanes(ssd_a_log[l], DT_LANE0),
                   row(jnp.repeat(ssd_d[l], SSD_HEAD_DIM)), row(ssd_norm[l]), expand, b, lp).reshape(t, D_SSD)
        y_c = _lru(xr, gate, lru_conv_w[l].astype(F32), row(lru_conv_b[l]),
                   _lru_gate_weights(lru_w_a[l], lru_w_x[l]), row(lru_b_a[l]), row(lru_b_x[l]),
                   row(lru_lambda[l]), b, lp).reshape(t, D_LRU)
        h = _merge(h, y_a, y_b, y_c, merge.reshape(t, N_BRANCH * D_MODEL), w_branch_attn[l].astype(BF16),
                   w_branch_ssd[l].astype(BF16), w_branch_lru[l].astype(BF16), w_out[l].astype(BF16))

        ffn2 = (row(ffn2_norm[l]), ffn2_w_gate_up[l, :, :D_FF].astype(BF16),
                ffn2_w_gate_up[l, :, D_FF:].astype(BF16), ffn2_w_down[l].astype(BF16), fg)
        if l < depth - 1:
            h = _ffn(h, *ffn2, False)
    return _ffn_final(h, *ffn2, b, lp, s)
```

```python
import functools

import jax
import jax.numpy as jnp
from jax import lax
from jax.experimental import pallas as pl
from jax.experimental.pallas import tpu as pltpu

F32 = jnp.float32
BF16 = jnp.bfloat16

D_MODEL = 1024
N_META = 16
SSD_CHUNK = 128
NORM_EPS = 1e-6
ATTN_HEADS = 16
ATTN_HEAD_DIM = 64
D_ATTN = ATTN_HEADS * ATTN_HEAD_DIM
SSD_HEAD_DIM = 64
D_SSD = D_MODEL
SSD_HEADS = D_SSD // SSD_HEAD_DIM
SSD_GROUPS = 2
SSD_STATE = 128
SSD_CONV = 4
D_XBC = D_SSD + 2 * SSD_GROUPS * SSD_STATE
D_LRU = D_MODEL
LRU_BLOCKS = 16
LRU_BLOCK_DIM = D_LRU // LRU_BLOCKS
LRU_CONV = 4
LRU_C = 8.0
D_FF = 2816
N_BRANCH = 3

LANES = 128
SUBLANES = 8
VMEM_LIMIT_BYTES = 56 * 1024 * 1024

SEQ_ALIGN = 256
FFN_ROWS = 1024
FFN_CHUNK = 256
PROJ_ROWS = 256
PROJ_CHUNK = 512
ATTN_TQ = 256
ATTN_HPS = 16
MERGE_ROWS = 1024
SCAN_ROWS = 256
SSD_ROWS = 256
FDT_COLS = LANES
DT_LANE0 = ATTN_HEADS
NEG_BIG = -1e30
LOG2E = 1.4426950408889634
Q_SCALE = ATTN_HEAD_DIM ** -0.5 * LOG2E
AUX_PARTS = 3
AUX_SLOTS = 8


def _cparams(sem):
    return pltpu.CompilerParams(dimension_semantics=sem, vmem_limit_bytes=VMEM_LIMIT_BYTES)


def _const_spec(shape):
    nd = len(shape)
    return pl.BlockSpec(shape, lambda *_: (0,) * nd, pipeline_mode=pl.Buffered(1))


def _rms(x, g):
    ms = jnp.mean(x * x, axis=-1, keepdims=True)
    return (x * lax.rsqrt(ms + NORM_EPS)) * g


def _dot(a, b):
    return jnp.dot(a, b, preferred_element_type=F32)


def _dot_nt(a, b):
    return lax.dot_general(a, b, (((1,), (1,)), ((), ())), preferred_element_type=F32)


def _split3(x):
    hi = x.astype(BF16)
    r1 = x - hi.astype(F32)
    mid = r1.astype(BF16)
    lo = (r1 - mid.astype(F32)).astype(BF16)
    return hi, mid, lo


def _dot_01_lhs(sel, x):
    hi, mid, lo = _split3(x)
    return _dot(sel, hi) + _dot(sel, mid) + _dot(sel, lo)


def _dot_01_rhs(x, sel):
    hi, mid, lo = _split3(x)
    return _dot(hi, sel) + _dot(mid, sel) + _dot(lo, sel)


def _log_sigmoid(x):
    return -(jnp.maximum(-x, 0.0) + jnp.log1p(jnp.exp(-jnp.abs(x))))


def _softplus(x):
    return jnp.maximum(x, 0.0) + jnp.log1p(jnp.exp(-jnp.abs(x)))


def _sigmoid(x):
    return 1.0 / (1.0 + jnp.exp(-x))


def _ffn_body(x_ref, g_ref, wg_ref, wu_ref, wd_ref, fg_ref, o_ref, a_scr, *, final_norm):
    x = x_ref[...]
    hn = _rms(x, g_ref[...]).astype(BF16)
    for c0 in range(0, D_FF, FFN_CHUNK):
        gate = _dot(hn, wg_ref[:, c0:c0 + FFN_CHUNK])
        up = _dot(hn, wu_ref[:, c0:c0 + FFN_CHUNK])
        a_scr[:, c0:c0 + FFN_CHUNK] = ((gate * _sigmoid(gate)) * up).astype(BF16)
    y = x + 0.5 * _dot(a_scr[...], wd_ref[...])
    if final_norm:
        y = _rms(y, fg_ref[...])
    o_ref[...] = y


def _ffn(x, g, wg, wu, wd, fg, final_norm):
    t = x.shape[0]
    tm = FFN_ROWS
    return pl.pallas_call(
        functools.partial(_ffn_body, final_norm=final_norm),
        out_shape=jax.ShapeDtypeStruct((t, D_MODEL), F32),
        grid=(t // tm,),
        in_specs=[pl.BlockSpec((tm, D_MODEL), lambda i: (i, 0))] + _ffn_weight_specs(),
        out_specs=pl.BlockSpec((tm, D_MODEL), lambda i: (i, 0)),
        scratch_shapes=[pltpu.VMEM((tm, D_FF), BF16)],
        compiler_params=_cparams(("parallel",)),
        name="ffn",
    )(x, g, wg, wu, wd, fg)


def _ffn_weight_specs():
    return [_const_spec((1, D_MODEL)), _const_spec((D_MODEL, D_FF)), _const_spec((D_MODEL, D_FF)),
            _const_spec((D_FF, D_MODEL)), _const_spec((1, D_MODEL))]


def _ffn_final_body(x_ref, g_ref, wg_ref, wu_ref, wd_ref, fg_ref, o_ref, a_scr):
    _ffn_body(x_ref.at[0], g_ref, wg_ref, wu_ref, wd_ref, fg_ref, o_ref.at[0], a_scr, final_norm=True)


def _ffn_final(x, g, wg, wu, wd, fg, b, lp, s_out):
    tm = FFN_ROWS
    return pl.pallas_call(
        _ffn_final_body,
        out_shape=jax.ShapeDtypeStruct((b, s_out, D_MODEL), F32),
        grid=(b, s_out // tm),
        in_specs=[pl.BlockSpec((pl.Element(1), pl.Element(tm), pl.Element(D_MODEL)),
                               lambda bi, i: (bi, pl.multiple_of(N_META + i * tm, SUBLANES), 0))]
        + _ffn_weight_specs(),
        out_specs=pl.BlockSpec((1, tm, D_MODEL), lambda bi, i: (bi, i, 0)),
        scratch_shapes=[pltpu.VMEM((tm, D_FF), BF16)],
        compiler_params=_cparams(("parallel", "parallel")),
        name="ffn_final",
    )(x.reshape(b, lp, D_MODEL), g, wg, wu, wd, fg)


_PLAIN_GROUPS = (("z", D_SSD), ("xbc", D_XBC), ("xr", D_LRU), ("gate", D_LRU), ("merge", N_BRANCH * D_MODEL))
OFF_K = 0
OFF_PLAIN = D_ATTN
OFF_FDT = OFF_PLAIN + sum(w for _, w in _PLAIN_GROUPS)
N_PROJ = OFF_FDT + FDT_COLS
D_AUG = ATTN_HEADS * LANES
V_ROWS = ATTN_HEAD_DIM + 16
ONE_LANE = LANES - 1


def _inproj_body(x_ref, g_ref, w_ref, wt_ref, fb_ref, pw_ref, qt_ref, ka_ref, vt_ref,
                 z_ref, xbc_ref, xr_ref, gate_ref, merge_ref, fdt_ref, carry_scr):
    tm = x_ref.shape[1]

    @pl.when(pl.program_id(1) == 0)
    def _():
        carry_scr[...] = jnp.zeros(carry_scr.shape, F32)

    hn = _rms(x_ref[0], g_ref[...]).astype(BF16)

    def mm(c0, width):
        return _dot(hn, w_ref[:, c0:c0 + width])

    fdt = mm(OFF_FDT, FDT_COLS)
    fdt_ref[0] = fdt
    off = OFF_PLAIN
    for (_, width), o_ref in zip(_PLAIN_GROUPS, (z_ref, xbc_ref, xr_ref, gate_ref, merge_ref)):
        for c0 in range(0, width, PROJ_CHUNK):
            o_ref[0, :, c0:c0 + PROJ_CHUNK] = mm(off + c0, PROJ_CHUNK).astype(o_ref.dtype)
        off += width

    lane = lax.broadcasted_iota(jnp.int32, (tm, LANES), 1)
    lf = jnp.where(lane < ATTN_HEADS, _log_sigmoid(fdt + fb_ref[...]), 0.0)
    row = lax.broadcasted_iota(jnp.int32, (tm, tm), 0)
    col = lax.broadcasted_iota(jnp.int32, (tm, tm), 1)
    c = _dot_01_lhs((row >= col).astype(BF16), lf) + carry_scr[0:1, :]
    carry_scr[0:1, :] = c[tm - 1:tm, :]
    hi, mid, lo = (part.astype(F32) for part in _split3(c * LOG2E))
    cparts = (hi + pltpu.roll(mid, ATTN_HEADS, 1) + pltpu.roll(lo, 2 * ATTN_HEADS, 1)
              + jnp.where(lane == ONE_LANE, 1.0, 0.0))

    w_aux = _dot(cparts.astype(BF16), pw_ref[...])
    k_aux = w_aux[:, :LANES]
    q_aux_t = w_aux[:, LANES:].T

    first = lane < ATTN_HEAD_DIM
    for c0 in range(0, D_ATTN, PROJ_CHUNK):
        kv = mm(OFF_K + c0, PROJ_CHUNK)
        for pr in range(PROJ_CHUNK // LANES):
            pair = c0 // LANES + pr
            k_data = kv[:, pr * LANES:(pr + 1) * LANES]
            ka_ref[0, :, 2 * pair * LANES:(2 * pair + 1) * LANES] = jnp.where(
                first, k_data, k_aux).astype(ka_ref.dtype)
            ka_ref[0, :, (2 * pair + 1) * LANES:(2 * pair + 2) * LANES] = jnp.where(
                first, k_aux, k_data).astype(ka_ref.dtype)

    hd = ATTN_HEAD_DIM
    q_t = _dot_nt(wt_ref[0:D_ATTN, :], hn) * Q_SCALE
    v_t = _dot_nt(wt_ref[D_ATTN:2 * D_ATTN, :], hn)
    ones = jnp.ones((V_ROWS - hd, tm), vt_ref.dtype)
    for h in range(ATTN_HEADS):
        even = h % 2 == 0
        data0 = h * LANES + (0 if even else hd)
        aux0 = h * LANES + (hd if even else 0)
        a0 = (hd if even else 0) + AUX_SLOTS * (h // 2)
        before, after = AUX_SLOTS * (h // 2), hd - AUX_SLOTS * (h // 2 + 1)
        pieces = ([jnp.zeros((before, tm), F32)] if before else []) + [q_aux_t[a0:a0 + AUX_SLOTS, :]]
        pieces += [jnp.zeros((after, tm), F32)] if after else []
        qt_ref[0, data0:data0 + hd, :] = q_t[h * hd:(h + 1) * hd, :].astype(qt_ref.dtype)
        qt_ref[0, aux0:aux0 + hd, :] = jnp.concatenate(pieces, axis=0).astype(qt_ref.dtype)
        vt_ref[0, h * V_ROWS:h * V_ROWS + hd, :] = v_t[h * hd:(h + 1) * hd, :].astype(vt_ref.dtype)
        vt_ref[0, h * V_ROWS + hd:(h + 1) * V_ROWS, :] = ones


def _inproj(x, g, w, wt, fb, pw, b, lp):
    tm = PROJ_ROWS
    row_spec = lambda width: pl.BlockSpec((1, tm, width), lambda bi, ti: (bi, ti, 0))
    col_spec = lambda height: pl.BlockSpec((1, height, tm), lambda bi, ti: (bi, 0, ti))
    plain = [w_ for _, w_ in _PLAIN_GROUPS]
    out_shape = ([jax.ShapeDtypeStruct((b, D_AUG, lp), BF16), jax.ShapeDtypeStruct((b, lp, D_AUG), BF16),
                  jax.ShapeDtypeStruct((b, ATTN_HEADS * V_ROWS, lp), BF16)]
                 + [jax.ShapeDtypeStruct((b, lp, w_), BF16) for w_ in plain]
                 + [jax.ShapeDtypeStruct((b, lp, FDT_COLS), F32)])
    return pl.pallas_call(
        _inproj_body,
        out_shape=out_shape,
        grid=(b, lp // tm),
        in_specs=[row_spec(D_MODEL),
                  _const_spec((1, D_MODEL)),
                  _const_spec((D_MODEL, N_PROJ)),
                  _const_spec((2 * D_ATTN, D_MODEL)),
                  _const_spec((1, LANES)),
                  _const_spec((LANES, 2 * LANES))],
        out_specs=([col_spec(D_AUG), row_spec(D_AUG), col_spec(ATTN_HEADS * V_ROWS)]
                   + [row_spec(w_) for w_ in plain] + [row_spec(FDT_COLS)]),
        scratch_shapes=[pltpu.VMEM((SUBLANES, LANES), F32)],
        compiler_params=_cparams(("parallel", "arbitrary")),
        name="inproj",
    )(x.reshape(b, lp, D_MODEL), g, w, wt, fb, pw)


def _aux_constants():
    src = jnp.arange(LANES)[:, None]
    lane = jnp.arange(LANES)[None, :]
    half = jnp.where(lane >= ATTN_HEAD_DIM, 0, 1)
    slot = lane % ATTN_HEAD_DIM
    head = 2 * (slot // AUX_SLOTS) + half
    idx = slot % AUX_SLOTS
    part = src // ATTN_HEADS
    part_src = (part < AUX_PARTS) & (src % ATTN_HEADS == head)
    one_src = src == ONE_LANE
    key = (one_src & (idx < AUX_PARTS)).astype(F32) - (part_src & (idx == part + AUX_PARTS)).astype(F32)
    qry = (part_src & (idx == part)).astype(F32) + (one_src & (idx >= AUX_PARTS) & (idx < 2 * AUX_PARTS)).astype(F32)
    return jnp.concatenate([key, qry], axis=1).astype(BF16)


def _attn_body(qt_ref, qn_ref, ka_hbm, vt_ref, mb_ref, o_ref, ka_ref, k_sem, m_scr, acc_scr, s_scr, *, tq, hps):
    bi = pl.program_id(0)
    hi = pl.program_id(1)
    iq = pl.program_id(2)

    def key_copy(blk, slot):
        rows = pl.ds(pl.multiple_of(blk * tq, tq), tq)
        return pltpu.make_async_copy(ka_hbm.at[bi, rows, pl.ds(pl.multiple_of(hi * hps * LANES, LANES), hps * LANES)],
                                     ka_ref.at[rows, :], k_sem.at[slot])

    @pl.when(iq == 0)
    def _():
        first_block = key_copy(0, 0)
        first_block.start()
        first_block.wait()

    @pl.when(iq > 0)
    def _():
        key_copy(iq, 1).wait()

    @pl.when(iq + 1 < pl.num_programs(2))
    def _():
        key_copy(iq + 1, 1).start()

    m_scr[...] = jnp.full(m_scr.shape, NEG_BIG, F32)
    acc_scr[...] = jnp.zeros(acc_scr.shape, F32)
    hd = ATTN_HEAD_DIM

    def scores(j, h):
        ks = pl.multiple_of(j * tq, tq)
        ka = ka_ref[pl.ds(ks, tq), h * LANES:(h + 1) * LANES]
        return _dot(ka, qt_ref[0, h * LANES:(h + 1) * LANES, :])

    def softmax_pv(j, h, st):
        ks = pl.multiple_of(j * tq, tq)
        vt = vt_ref[0, h * V_ROWS:(h + 1) * V_ROWS, pl.ds(ks, tq)]
        m_prev = m_scr[h]
        m_new = jnp.maximum(m_prev, jnp.max(st, axis=0, keepdims=True))
        alpha = jnp.exp2(m_prev - m_new)
        pt = jnp.exp2(st - m_new).astype(vt.dtype)
        acc_scr[h] = alpha * acc_scr[h] + _dot(vt, pt)
        m_scr[h] = m_new

    @pl.when(iq == 0)
    def _():
        for h in range(hps):
            s_scr[h] = scores(0, h)

    def loop_body(j, carry):
        for h in range(hps):
            s_next = scores(j + 1, h)
            softmax_pv(j, h, s_scr[h])
            s_scr[h] = s_next
        return carry

    lax.fori_loop(0, iq, loop_body, 0)
    for h in range(hps):
        s_next = _dot(ka_ref[0:tq, h * LANES:(h + 1) * LANES], qn_ref[0, h * LANES:(h + 1) * LANES, :])
        softmax_pv(iq, h, s_scr[h] + mb_ref[...])
        s_scr[h] = s_next

    for hp in range(hps // 2):
        outs = []
        for e in range(2):
            acc = acc_scr[2 * hp + e]
            outs.append(acc[0:hd, :] / acc[hd:hd + 1, :])
        o_ref[0, :, hp * LANES:(hp + 1) * LANES] = jnp.concatenate(outs, axis=0).T.astype(o_ref.dtype)


def _attention(qt, ka, vt, b, lp):
    tq, hps = ATTN_TQ, ATTN_HPS
    key_i = jnp.arange(tq)[:, None]
    query_i = jnp.arange(tq)[None, :]
    causal_bias = jnp.where(key_i <= query_i, 0.0, NEG_BIG).astype(F32)
    last_tile = lp // tq - 1
    return pl.pallas_call(
        functools.partial(_attn_body, tq=tq, hps=hps),
        out_shape=jax.ShapeDtypeStruct((b, lp, D_ATTN), BF16),
        grid=(b, ATTN_HEADS // hps, lp // tq),
        in_specs=[pl.BlockSpec((1, hps * LANES, tq), lambda bi, hi, qi: (bi, hi, qi)),
                  pl.BlockSpec((1, hps * LANES, tq), lambda bi, hi, qi: (bi, hi, jnp.minimum(qi + 1, last_tile))),
                  pl.BlockSpec(memory_space=pl.ANY),
                  pl.BlockSpec((1, hps * V_ROWS, lp), lambda bi, hi, qi: (bi, hi, 0)),
                  pl.BlockSpec((tq, tq), lambda bi, hi, qi: (0, 0))],
        out_specs=pl.BlockSpec((1, tq, hps * ATTN_HEAD_DIM), lambda bi, hi, qi: (bi, qi, hi)),
        scratch_shapes=[pltpu.VMEM((lp, hps * LANES), BF16),
                        pltpu.SemaphoreType.DMA((2,)),
                        pltpu.VMEM((hps, 1, tq), F32),
                        pltpu.VMEM((hps, V_ROWS, tq), F32),
                        pltpu.VMEM((hps, tq, tq), F32)],
        compiler_params=_cparams(("parallel", "parallel", "arbitrary")),
        name="fox_attention",
    )(qt, qt, ka, vt, causal_bias)


def _causal_conv(x, xp_scr, w_ref, b_ref, first_tile, rows, taps):
    @pl.when(first_tile)
    def _():
        xp_scr[0:SUBLANES, :] = jnp.zeros((SUBLANES, x.shape[1]), F32)

    xp_scr[SUBLANES:SUBLANES + rows, :] = x
    y = b_ref[...] + w_ref[taps - 1:taps, :] * x
    for kk in range(taps - 1):
        r0 = SUBLANES - (taps - 1) + kk
        y = y + w_ref[kk:kk + 1, :] * xp_scr[r0:r0 + rows, :]
    xp_scr[0:SUBLANES, :] = x[rows - SUBLANES:rows, :]
    return y


def _ssd_body(xbc_ref, z_ref, fdt_ref, cw_ref, cb_ref, dtb_ref, alog_ref, dfull_ref, nw_ref,
              exp_ref, o_ref, xp_scr, st_scr):
    rows = SSD_ROWS
    first_tile = pl.program_id(1) == 0

    @pl.when(first_tile)
    def _():
        st_scr[...] = jnp.zeros(st_scr.shape, F32)

    y = _causal_conv(xbc_ref[0].astype(F32), xp_scr, cw_ref, cb_ref, first_tile, rows, SSD_CONV)
    xc = y * _sigmoid(y)
    for ci in range(rows // SSD_CHUNK):
        rs = slice(ci * SSD_CHUNK, (ci + 1) * SSD_CHUNK)
        o_ref[0, rs, :] = _ssd_chunk(xc[rs], z_ref[0, rs, :].astype(F32), fdt_ref[0, rs, :], dtb_ref, alog_ref,
                                     dfull_ref, nw_ref, exp_ref, st_scr).astype(o_ref.dtype)


def _ssd_chunk(xc, z, dt_raw, dtb_ref, alog_ref, dfull_ref, nw_ref, exp_ref, st_scr):
    q = SSD_CHUNK
    gs = D_SSD // SSD_GROUPS
    heads_per_group = SSD_HEADS // SSD_GROUPS

    lane = lax.broadcasted_iota(jnp.int32, (q, LANES), 1)
    dt_lane = (lane >= DT_LANE0) & (lane < DT_LANE0 + SSD_HEADS)
    dt = jnp.where(dt_lane, _softplus(dt_raw + dtb_ref[...]), 0.0)
    a = -jnp.exp(alog_ref[...])
    da = dt * a
    row = lax.broadcasted_iota(jnp.int32, (q, q), 0)
    col = lax.broadcasted_iota(jnp.int32, (q, q), 1)
    lower = row >= col
    a_cum = _dot_01_lhs(lower.astype(BF16), da)
    a_cum_t = a_cum.T
    half = lax.broadcasted_iota(jnp.int32, (q, LANES), 1) < SSD_HEAD_DIM

    outs = []
    for g in range(SSD_GROUPS):
        sl = slice(g * gs, (g + 1) * gs)
        expand = exp_ref[:, sl]
        dt_full = _dot_01_rhs(dt, expand)
        a_cum_full = _dot_01_rhs(a_cum, expand)
        a_last_full = a_cum_full[q - 1:q, :]
        xs = xc[:, sl]
        xdt = xs * dt_full
        xdt_b = xdt.astype(BF16)
        xde_b = (xdt * jnp.exp(a_last_full - a_cum_full)).astype(BF16)

        bm = xc[:, D_SSD + g * SSD_STATE:D_SSD + (g + 1) * SSD_STATE]
        cm = xc[:, D_SSD + SSD_GROUPS * SSD_STATE + g * SSD_STATE:
                D_SSD + SSD_GROUPS * SSD_STATE + (g + 1) * SSD_STATE]
        bm_b = bm.astype(BF16)
        cm_b = cm.astype(BF16)
        cb = _dot_nt(cm_b, bm_b)
        y_pairs = []
        for pair in range(heads_per_group // 2):
            xp = xdt_b[:, pair * LANES:(pair + 1) * LANES]
            ys = []
            for e in range(2):
                hl = DT_LANE0 + g * heads_per_group + 2 * pair + e
                seg = a_cum[:, hl:hl + 1] - a_cum_t[hl:hl + 1, :]
                dec = jnp.exp(jnp.where(lower, seg, -jnp.inf))
                ys.append(_dot((cb * dec).astype(BF16), xp))
            y_pairs.append(jnp.where(half, ys[0], ys[1]))
        y_diag = jnp.concatenate(y_pairs, axis=1)
        prev = st_scr[g]
        y_off = _dot(cm_b, prev.astype(BF16)) * jnp.exp(a_cum_full)
        st_scr[g] = prev * jnp.exp(a_last_full) + _dot(bm.T.astype(BF16), xde_b)
        yg = y_diag + y_off + dfull_ref[:, sl] * xs
        zg = z[:, sl]
        yg = yg * (zg * _sigmoid(zg))
        yg = yg * lax.rsqrt(jnp.mean(yg * yg, axis=-1, keepdims=True) + NORM_EPS)
        outs.append(yg * nw_ref[:, sl])
    return jnp.concatenate(outs, axis=1)


def _ssd(xbc, z, fdt, cw, cb, dtb, alog, dfull, nw, expand, b, lp):
    q = SSD_ROWS
    return pl.pallas_call(
        _ssd_body,
        out_shape=jax.ShapeDtypeStruct((b, lp, D_SSD), BF16),
        grid=(b, lp // q),
        in_specs=[pl.BlockSpec((1, q, D_XBC), lambda bi, ci: (bi, ci, 0)),
                  pl.BlockSpec((1, q, D_SSD), lambda bi, ci: (bi, ci, 0)),
                  pl.BlockSpec((1, q, FDT_COLS), lambda bi, ci: (bi, ci, 0)),
                  _const_spec((SSD_CONV, D_XBC)),
                  _const_spec((1, D_XBC)),
                  _const_spec((1, LANES)),
                  _const_spec((1, LANES)),
                  _const_spec((1, D_SSD)),
                  _const_spec((1, D_SSD)),
                  _const_spec((LANES, D_SSD))],
        out_specs=pl.BlockSpec((1, q, D_SSD), lambda bi, ci: (bi, ci, 0)),
        scratch_shapes=[pltpu.VMEM((SUBLANES + q, D_XBC), F32),
                        pltpu.VMEM((SSD_GROUPS, SSD_STATE, D_SSD // SSD_GROUPS), F32)],
        compiler_params=_cparams(("parallel", "arbitrary")),
        name="ssd",
    )(xbc, z, fdt, cw, cb, dtb, alog, dfull, nw, expand)


def _lru_body(xr_ref, gate_ref, cw_ref, cb_ref, w2_ref, ba_ref, bx_ref, lam_ref, o_ref,
              xp_scr, h_scr):
    rows = SCAN_ROWS
    first_tile = pl.program_id(1) == 0

    @pl.when(first_tile)
    def _():
        h_scr[...] = jnp.zeros(h_scr.shape, F32)

    xc = _causal_conv(xr_ref[0].astype(F32), xp_scr, cw_ref, cb_ref, first_tile, rows, LRU_CONV)
    xc_b = xc.astype(BF16)
    pre = [_dot(xc_b[:, j * LANES:(j + 1) * LANES], w2_ref[j]) for j in range(D_LRU // LANES)]
    pre_a = jnp.concatenate([p[:, :LANES] for p in pre], axis=1)
    pre_x = jnp.concatenate([p[:, LANES:] for p in pre], axis=1)
    r = _sigmoid(pre_a + ba_ref[...])
    i = _sigmoid(pre_x + bx_ref[...])
    log_a = LRU_C * r * _log_sigmoid(lam_ref[...])
    a = jnp.exp(log_a)
    mult = jnp.sqrt(-jnp.tanh(log_a) * (a * a + 1.0))
    row0 = lax.broadcasted_iota(jnp.int32, (SUBLANES, D_LRU), 0) == 0
    mult = jnp.concatenate([jnp.where(first_tile & row0, 1.0, mult[:SUBLANES]), mult[SUBLANES:]], axis=0)
    u = mult * (i * xc)

    groups = rows // SUBLANES
    a3 = a.reshape(groups, SUBLANES, D_LRU)
    u3 = u.reshape(groups, SUBLANES, D_LRU)
    sub = lax.broadcasted_iota(jnp.int32, (groups, SUBLANES, D_LRU), 1)
    d = 1
    while d < SUBLANES:
        keep = sub >= d
        a_s = jnp.where(keep, pltpu.roll(a3, d, 1), 1.0)
        u_s = jnp.where(keep, pltpu.roll(u3, d, 1), 0.0)
        u3 = a3 * u_s + u3
        a3 = a3 * a_s
        d *= 2
    h_prev = h_scr[0:1, :]
    hs = []
    for r in range(groups):
        h_r = a3[r] * h_prev + u3[r]
        hs.append(h_r)
        h_prev = h_r[SUBLANES - 1:SUBLANES, :]
    h = jnp.concatenate(hs, axis=0)
    h_scr[0:1, :] = h[rows - 1:rows, :]
    o_ref[0] = (h * jax.nn.gelu(gate_ref[0].astype(F32))).astype(o_ref.dtype)


def _lru(xr, gate, cw, cb, w2, ba, bx, lam, b, lp):
    rows = SCAN_ROWS
    return pl.pallas_call(
        _lru_body,
        out_shape=jax.ShapeDtypeStruct((b, lp, D_LRU), BF16),
        grid=(b, lp // rows),
        in_specs=[pl.BlockSpec((1, rows, D_LRU), lambda bi, ti: (bi, ti, 0)),
                  pl.BlockSpec((1, rows, D_LRU), lambda bi, ti: (bi, ti, 0)),
                  _const_spec((LRU_CONV, D_LRU)),
                  _const_spec((1, D_LRU)),
                  _const_spec((D_LRU // LANES, LANES, 2 * LANES)),
                  _const_spec((1, D_LRU)),
                  _const_spec((1, D_LRU)),
                  _const_spec((1, D_LRU))],
        out_specs=pl.BlockSpec((1, rows, D_LRU), lambda bi, ti: (bi, ti, 0)),
        scratch_shapes=[pltpu.VMEM((SUBLANES + rows, D_LRU), F32),
                        pltpu.VMEM((SUBLANES, D_LRU), F32)],
        compiler_params=_cparams(("parallel", "arbitrary")),
        name="rglru",
    )(xr, gate, cw, cb, w2, ba, bx, lam)


def _merge_body(h_ref, ya_ref, yb_ref, yc_ref, m_ref, wa_ref, wb_ref, wc_ref, wo_ref, o_ref):
    gate = lambda i: _sigmoid(m_ref[:, i * D_MODEL:(i + 1) * D_MODEL].astype(F32))
    mixed = gate(0) * _dot(ya_ref[...], wa_ref[...])
    mixed = mixed + gate(1) * _dot(yb_ref[...], wb_ref[...])
    mixed = mixed + gate(2) * _dot(yc_ref[...], wc_ref[...])
    o_ref[...] = h_ref[...] + _dot(mixed.astype(BF16), wo_ref[...])


def _merge(h, ya, yb, yc, m, wa, wb, wc, wo):
    t = h.shape[0]
    tm = MERGE_ROWS
    row_spec = lambda width: pl.BlockSpec((tm, width), lambda i: (i, 0))
    return pl.pallas_call(
        _merge_body,
        out_shape=jax.ShapeDtypeStruct((t, D_MODEL), F32),
        grid=(t // tm,),
        in_specs=[row_spec(D_MODEL), row_spec(D_ATTN), row_spec(D_SSD), row_spec(D_LRU),
                  row_spec(N_BRANCH * D_MODEL),
                  _const_spec((D_ATTN, D_MODEL)), _const_spec((D_SSD, D_MODEL)),
                  _const_spec((D_LRU, D_MODEL)), _const_spec((D_MODEL, D_MODEL))],
        out_specs=row_spec(D_MODEL),
        compiler_params=_cparams(("parallel",)),
        name="merge_out",
    )(h, ya, yb, yc, m, wa, wb, wc, wo)


def _prep_w_in(w_in):
    sizes = (D_ATTN, D_ATTN, D_ATTN, ATTN_HEADS, D_SSD, D_XBC, SSD_HEADS, D_LRU, D_LRU, N_BRANCH * D_MODEL)
    offs = [0]
    for s in sizes:
        offs.append(offs[-1] + s)
    part = lambda i: w_in[:, offs[i]:offs[i + 1]]
    q, k, v, f, z, xbc, dt, xr, gate, merge = (part(i) for i in range(10))
    pad = jnp.zeros((D_MODEL, FDT_COLS - ATTN_HEADS - SSD_HEADS), w_in.dtype)
    w = jnp.concatenate([k, z, xbc, xr, gate, merge, f, dt, pad], axis=1).astype(BF16)
    wt = jnp.concatenate([q.T, v.T], axis=0).astype(BF16)
    return w, wt


def _pad_lanes(vec, lane0):
    out = jnp.zeros((1, LANES), F32)
    return out.at[0, lane0:lane0 + vec.shape[0]].set(vec.astype(F32))


def _lru_gate_weights(w_a, w_x):
    def blockdiag_pairs(w):
        w = w.reshape(LRU_BLOCKS // 2, 2, LRU_BLOCK_DIM, LRU_BLOCK_DIM)
        zero = jnp.zeros_like(w[:, 0])
        top = jnp.concatenate([w[:, 0], zero], axis=2)
        bot = jnp.concatenate([zero, w[:, 1]], axis=2)
        return jnp.concatenate([top, bot], axis=1)
    return jnp.concatenate([blockdiag_pairs(w_a), blockdiag_pairs(w_x)], axis=2).astype(BF16)


def _head_expand():
    rows = jnp.arange(LANES)[:, None]
    cols = jnp.arange(D_SSD)[None, :]
    return (rows == DT_LANE0 + cols // SSD_HEAD_DIM).astype(BF16)


def kernel(x, meta_tokens, ffn1_norm, ffn1_w_gate_up, ffn1_w_down, mix_norm, w_in, fox_forget_bias,
           ssd_conv_w, ssd_conv_b, ssd_dt_bias, ssd_a_log, ssd_d, ssd_norm,
           lru_conv_w, lru_conv_b, lru_w_a, lru_b_a, lru_w_x, lru_b_x, lru_lambda,
           w_branch_attn, w_branch_ssd, w_branch_lru, w_out,
           ffn2_norm, ffn2_w_gate_up, ffn2_w_down, final_norm):
    b, s, d = x.shape
    depth = w_in.shape[0]
    length = N_META + s
    lp = -(-length // SEQ_ALIGN) * SEQ_ALIGN
    t = b * lp
    assert d == D_MODEL and t % FFN_ROWS == 0 and t % MERGE_ROWS == 0 and s % FFN_ROWS == 0

    meta = jnp.broadcast_to(meta_tokens.astype(x.dtype)[None], (b, N_META, d))
    h = jnp.concatenate([meta, x, jnp.zeros((b, lp - length, d), x.dtype)], axis=1).reshape(t, d)

    row = lambda vec: vec.astype(F32).reshape(1, -1)
    expand = _head_expand()
    pw = _aux_constants()
    fg = row(final_norm)
    for l in range(depth):
        h = _ffn(h, row(ffn1_norm[l]), ffn1_w_gate_up[l, :, :D_FF].astype(BF16),
                 ffn1_w_gate_up[l, :, D_FF:].astype(BF16), ffn1_w_down[l].astype(BF16), fg, False)

        w_std, w_t = _prep_w_in(w_in[l])
        qt, ka, vt, z, xbc, xr, gate, merge, fdt = _inproj(
            h, row(mix_norm[l]), w_std, w_t, _pad_lanes(fox_forget_bias[l], 0), pw, b, lp)
        y_a = _attention(qt, ka, vt, b, lp).reshape(t, D_ATTN)
        y_b = _ssd(xbc, z, fdt, ssd_conv_w[l].astype(F32), row(ssd_conv_b[l]),
                   _pad_lanes(ssd_dt_bias[l], DT_LANE0), _pad_lanes(ssd_a_log[l], DT_LANE0),
                   row(jnp.repeat(ssd_d[l], SSD_HEAD_DIM)), row(ssd_norm[l]), expand, b, lp).reshape(t, D_SSD)
        y_c = _lru(xr, gate, lru_conv_w[l].astype(F32), row(lru_conv_b[l]),
                   _lru_gate_weights(lru_w_a[l], lru_w_x[l]), row(lru_b_a[l]), row(lru_b_x[l]),
                   row(lru_lambda[l]), b, lp).reshape(t, D_LRU)
        h = _merge(h, y_a, y_b, y_c, merge.reshape(t, N_BRANCH * D_MODEL), w_branch_attn[l].astype(BF16),
                   w_branch_ssd[l].astype(BF16), w_branch_lru[l].astype(BF16), w_out[l].astype(BF16))

        ffn2 = (row(ffn2_norm[l]), ffn2_w_gate_up[l, :, :D_FF].astype(BF16),
                ffn2_w_gate_up[l, :, D_FF:].astype(BF16), ffn2_w_down[l].astype(BF16), fg)
        if l < depth - 1:
            h = _ffn(h, *ffn2, False)
    return _ffn_final(h, *ffn2, b, lp, s)
```

```python
import functools

import jax
import jax.numpy as jnp
from jax import lax
from jax.experimental import pallas as pl
from jax.experimental.pallas import tpu as pltpu

F32 = jnp.float32
BF16 = jnp.bfloat16

D_MODEL = 1024
N_META = 16
SSD_CHUNK = 128
NORM_EPS = 1e-6
ATTN_HEADS = 16
ATTN_HEAD_DIM = 64
D_ATTN = ATTN_HEADS * ATTN_HEAD_DIM
SSD_HEAD_DIM = 64
D_SSD = D_MODEL
SSD_HEADS = D_SSD // SSD_HEAD_DIM
SSD_GROUPS = 2
SSD_STATE = 128
SSD_CONV = 4
D_XBC = D_SSD + 2 * SSD_GROUPS * SSD_STATE
D_LRU = D_MODEL
LRU_BLOCKS = 16
LRU_BLOCK_DIM = D_LRU // LRU_BLOCKS
LRU_CONV = 4
LRU_C = 8.0
D_FF = 2816
N_BRANCH = 3

LANES = 128
SUBLANES = 8
VMEM_LIMIT_BYTES = 56 * 1024 * 1024

SEQ_ALIGN = 256
FFN_ROWS = 1024
FFN_CHUNK = 256
PROJ_ROWS = 256
PROJ_CHUNK = 512
ATTN_TQ = 256
ATTN_HPS = 16
MERGE_ROWS = 1024
SCAN_ROWS = 256
SSD_ROWS = 256
FDT_COLS = LANES
DT_LANE0 = ATTN_HEADS
NEG_BIG = -1e30
LOG2E = 1.4426950408889634
Q_SCALE = ATTN_HEAD_DIM ** -0.5 * LOG2E
AUX_PARTS = 3
AUX_SLOTS = 8


def _cparams(sem):
    return pltpu.CompilerParams(dimension_semantics=sem, vmem_limit_bytes=VMEM_LIMIT_BYTES)


def _const_spec(shape):
    nd = len(shape)
    return pl.BlockSpec(shape, lambda *_: (0,) * nd, pipeline_mode=pl.Buffered(1))


def _rms(x, g):
    ms = jnp.mean(x * x, axis=-1, keepdims=True)
    return (x * lax.rsqrt(ms + NORM_EPS)) * g


def _dot(a, b):
    return jnp.dot(a, b, preferred_element_type=F32)


def _dot_nt(a, b):
    return lax.dot_general(a, b, (((1,), (1,)), ((), ())), preferred_element_type=F32)


def _split3(x):
    hi = x.astype(BF16)
    r1 = x - hi.astype(F32)
    mid = r1.astype(BF16)
    lo = (r1 - mid.astype(F32)).astype(BF16)
    return hi, mid, lo


def _dot_01_lhs(sel, x):
    hi, mid, lo = _split3(x)
    return _dot(sel, hi) + _dot(sel, mid) + _dot(sel, lo)


def _dot_01_rhs(x, sel):
    hi, mid, lo = _split3(x)
    return _dot(hi, sel) + _dot(mid, sel) + _dot(lo, sel)


def _log_sigmoid(x):
    return -(jnp.maximum(-x, 0.0) + jnp.log1p(jnp.exp(-jnp.abs(x))))


def _softplus(x):
    return jnp.maximum(x, 0.0) + jnp.log1p(jnp.exp(-jnp.abs(x)))


def _sigmoid(x):
    return 1.0 / (1.0 + jnp.exp(-x))


def _ffn_body(x_ref, g_ref, wg_ref, wu_ref, wd_ref, fg_ref, o_ref, a_scr, *, final_norm):
    x = x_ref[...]
    hn = _rms(x, g_ref[...]).astype(BF16)
    for c0 in range(0, D_FF, FFN_CHUNK):
        gate = _dot(hn, wg_ref[:, c0:c0 + FFN_CHUNK])
        up = _dot(hn, wu_ref[:, c0:c0 + FFN_CHUNK])
        a_scr[:, c0:c0 + FFN_CHUNK] = ((gate * _sigmoid(gate)) * up).astype(BF16)
    y = x + 0.5 * _dot(a_scr[...], wd_ref[...])
    if final_norm:
        y = _rms(y, fg_ref[...])
    o_ref[...] = y


def _ffn(x, g, wg, wu, wd, fg, final_norm):
    t = x.shape[0]
    tm = FFN_ROWS
    return pl.pallas_call(
        functools.partial(_ffn_body, final_norm=final_norm),
        out_shape=jax.ShapeDtypeStruct((t, D_MODEL), F32),
        grid=(t // tm,),
        in_specs=[pl.BlockSpec((tm, D_MODEL), lambda i: (i, 0))] + _ffn_weight_specs(),
        out_specs=pl.BlockSpec((tm, D_MODEL), lambda i: (i, 0)),
        scratch_shapes=[pltpu.VMEM((tm, D_FF), BF16)],
        compiler_params=_cparams(("parallel",)),
        name="ffn",
    )(x, g, wg, wu, wd, fg)


def _ffn_weight_specs():
    return [_const_spec((1, D_MODEL)), _const_spec((D_MODEL, D_FF)), _const_spec((D_MODEL, D_FF)),
            _const_spec((D_FF, D_MODEL)), _const_spec((1, D_MODEL))]


def _ffn_final_body(x_ref, g_ref, wg_ref, wu_ref, wd_ref, fg_ref, o_ref, a_scr):
    _ffn_body(x_ref.at[0], g_ref, wg_ref, wu_ref, wd_ref, fg_ref, o_ref.at[0], a_scr, final_norm=True)


def _ffn_final(x, g, wg, wu, wd, fg, b, lp, s_out):
    tm = FFN_ROWS
    return pl.pallas_call(
        _ffn_final_body,
        out_shape=jax.ShapeDtypeStruct((b, s_out, D_MODEL), F32),
        grid=(b, s_out // tm),
        in_specs=[pl.BlockSpec((pl.Element(1), pl.Element(tm), pl.Element(D_MODEL)),
                               lambda bi, i: (bi, pl.multiple_of(N_META + i * tm, SUBLANES), 0))]
        + _ffn_weight_specs(),
        out_specs=pl.BlockSpec((1, tm, D_MODEL), lambda bi, i: (bi, i, 0)),
        scratch_shapes=[pltpu.VMEM((tm, D_FF), BF16)],
        compiler_params=_cparams(("parallel", "parallel")),
        name="ffn_final",
    )(x.reshape(b, lp, D_MODEL), g, wg, wu, wd, fg)


_PLAIN_GROUPS = (("z", D_SSD), ("xbc", D_XBC), ("xr", D_LRU), ("gate", D_LRU), ("merge", N_BRANCH * D_MODEL))
OFF_K = 0
OFF_PLAIN = D_ATTN
OFF_FDT = OFF_PLAIN + sum(w for _, w in _PLAIN_GROUPS)
N_PROJ = OFF_FDT + FDT_COLS
D_AUG = ATTN_HEADS * LANES
V_ROWS = ATTN_HEAD_DIM + 16
ONE_LANE = LANES - 1


def _inproj_body(x_ref, g_ref, w_ref, wt_ref, fb_ref, pw_ref, qt_ref, ka_ref, vt_ref,
                 z_ref, xbc_ref, xr_ref, gate_ref, merge_ref, fdt_ref, carry_scr):
    tm = x_ref.shape[1]

    @pl.when(pl.program_id(1) == 0)
    def _():
        carry_scr[...] = jnp.zeros(carry_scr.shape, F32)

    hn = _rms(x_ref[0], g_ref[...]).astype(BF16)

    def mm(c0, width):
        return _dot(hn, w_ref[:, c0:c0 + width])

    fdt = mm(OFF_FDT, FDT_COLS)
    fdt_ref[0] = fdt
    off = OFF_PLAIN
    for (_, width), o_ref in zip(_PLAIN_GROUPS, (z_ref, xbc_ref, xr_ref, gate_ref, merge_ref)):
        for c0 in range(0, width, PROJ_CHUNK):
            o_ref[0, :, c0:c0 + PROJ_CHUNK] = mm(off + c0, PROJ_CHUNK).astype(o_ref.dtype)
        off += width

    lane = lax.broadcasted_iota(jnp.int32, (tm, LANES), 1)
    lf = jnp.where(lane < ATTN_HEADS, _log_sigmoid(fdt + fb_ref[...]), 0.0)
    row = lax.broadcasted_iota(jnp.int32, (tm, tm), 0)
    col = lax.broadcasted_iota(jnp.int32, (tm, tm), 1)
    c = _dot_01_lhs((row >= col).astype(BF16), lf) + carry_scr[0:1, :]
    carry_scr[0:1, :] = c[tm - 1:tm, :]
    hi, mid, lo = (part.astype(F32) for part in _split3(c * LOG2E))
    cparts = (hi + pltpu.roll(mid, ATTN_HEADS, 1) + pltpu.roll(lo, 2 * ATTN_HEADS, 1)
              + jnp.where(lane == ONE_LANE, 1.0, 0.0))

    w_aux = _dot(cparts.astype(BF16), pw_ref[...])
    k_aux = w_aux[:, :LANES]
    q_aux_t = w_aux[:, LANES:].T

    first = lane < ATTN_HEAD_DIM
    for c0 in range(0, D_ATTN, PROJ_CHUNK):
        kv = mm(OFF_K + c0, PROJ_CHUNK)
        for pr in range(PROJ_CHUNK // LANES):
            pair = c0 // LANES + pr
            k_data = kv[:, pr * LANES:(pr + 1) * LANES]
            ka_ref[0, :, 2 * pair * LANES:(2 * pair + 1) * LANES] = jnp.where(
                first, k_data, k_aux).astype(ka_ref.dtype)
            ka_ref[0, :, (2 * pair + 1) * LANES:(2 * pair + 2) * LANES] = jnp.where(
                first, k_aux, k_data).astype(ka_ref.dtype)

    hd = ATTN_HEAD_DIM
    q_t = _dot_nt(wt_ref[0:D_ATTN, :], hn) * Q_SCALE
    v_t = _dot_nt(wt_ref[D_ATTN:2 * D_ATTN, :], hn)
    ones = jnp.ones((V_ROWS - hd, tm), vt_ref.dtype)
    for h in range(ATTN_HEADS):
        even = h % 2 == 0
        data0 = h * LANES + (0 if even else hd)
        aux0 = h * LANES + (hd if even else 0)
        a0 = (hd if even else 0) + AUX_SLOTS * (h // 2)
        before, after = AUX_SLOTS * (h // 2), hd - AUX_SLOTS * (h // 2 + 1)
        pieces = ([jnp.zeros((before, tm), F32)] if before else []) + [q_aux_t[a0:a0 + AUX_SLOTS, :]]
        pieces += [jnp.zeros((after, tm), F32)] if after else []
        qt_ref[0, 0, data0:data0 + hd, :] = q_t[h * hd:(h + 1) * hd, :].astype(qt_ref.dtype)
        qt_ref[0, 0, aux0:aux0 + hd, :] = jnp.concatenate(pieces, axis=0).astype(qt_ref.dtype)
        vt_ref[0, 0, h * V_ROWS:h * V_ROWS + hd, :] = v_t[h * hd:(h + 1) * hd, :].astype(vt_ref.dtype)
        vt_ref[0, 0, h * V_ROWS + hd:(h + 1) * V_ROWS, :] = ones


def _inproj(x, g, w, wt, fb, pw, b, lp):
    tm = PROJ_ROWS
    row_spec = lambda width: pl.BlockSpec((1, tm, width), lambda bi, ti: (bi, ti, 0))
    col_spec = lambda height: pl.BlockSpec((1, 1, height, tm), lambda bi, ti: (bi, ti, 0, 0))
    plain = [w_ for _, w_ in _PLAIN_GROUPS]
    out_shape = ([jax.ShapeDtypeStruct((b, lp // tm, D_AUG, tm), BF16), jax.ShapeDtypeStruct((b, lp, D_AUG), BF16),
                  jax.ShapeDtypeStruct((b, lp // tm, ATTN_HEADS * V_ROWS, tm), BF16)]
                 + [jax.ShapeDtypeStruct((b, lp, w_), BF16) for w_ in plain]
                 + [jax.ShapeDtypeStruct((b, lp, FDT_COLS), F32)])
    return pl.pallas_call(
        _inproj_body,
        out_shape=out_shape,
        grid=(b, lp // tm),
        in_specs=[row_spec(D_MODEL),
                  _const_spec((1, D_MODEL)),
                  _const_spec((D_MODEL, N_PROJ)),
                  _const_spec((2 * D_ATTN, D_MODEL)),
                  _const_spec((1, LANES)),
                  _const_spec((LANES, 2 * LANES))],
        out_specs=([col_spec(D_AUG), row_spec(D_AUG), col_spec(ATTN_HEADS * V_ROWS)]
                   + [row_spec(w_) for w_ in plain] + [row_spec(FDT_COLS)]),
        scratch_shapes=[pltpu.VMEM((SUBLANES, LANES), F32)],
        compiler_params=_cparams(("parallel", "arbitrary")),
        name="inproj",
    )(x.reshape(b, lp, D_MODEL), g, w, wt, fb, pw)


def _aux_constants():
    src = jnp.arange(LANES)[:, None]
    lane = jnp.arange(LANES)[None, :]
    half = jnp.where(lane >= ATTN_HEAD_DIM, 0, 1)
    slot = lane % ATTN_HEAD_DIM
    head = 2 * (slot // AUX_SLOTS) + half
    idx = slot % AUX_SLOTS
    part = src // ATTN_HEADS
    part_src = (part < AUX_PARTS) & (src % ATTN_HEADS == head)
    one_src = src == ONE_LANE
    key = (one_src & (idx < AUX_PARTS)).astype(F32) - (part_src & (idx == part + AUX_PARTS)).astype(F32)
    qry = (part_src & (idx == part)).astype(F32) + (one_src & (idx >= AUX_PARTS) & (idx < 2 * AUX_PARTS)).astype(F32)
    return jnp.concatenate([key, qry], axis=1).astype(BF16)


def _attn_body(qt_ref, qn_ref, ka_hbm, vt_ref, mb_ref, o_ref, ka_ref, k_sem, m_scr, acc_scr, s_scr, *, tq, hps):
    bi = pl.program_id(0)
    hi = pl.program_id(1)
    iq = pl.program_id(2)

    def key_copy(blk, slot):
        rows = pl.ds(pl.multiple_of(blk * tq, tq), tq)
        return pltpu.make_async_copy(ka_hbm.at[bi, rows, pl.ds(pl.multiple_of(hi * hps * LANES, LANES), hps * LANES)],
                                     ka_ref.at[rows, :], k_sem.at[slot])

    @pl.when(iq == 0)
    def _():
        first_block = key_copy(0, 0)
        first_block.start()
        first_block.wait()

    @pl.when(iq > 0)
    def _():
        key_copy(iq, 1).wait()

    @pl.when(iq + 1 < pl.num_programs(2))
    def _():
        key_copy(iq + 1, 1).start()

    m_scr[...] = jnp.full(m_scr.shape, NEG_BIG, F32)
    acc_scr[...] = jnp.zeros(acc_scr.shape, F32)
    hd = ATTN_HEAD_DIM

    def scores(j, h):
        ks = pl.multiple_of(j * tq, tq)
        ka = ka_ref[pl.ds(ks, tq), h * LANES:(h + 1) * LANES]
        return _dot(ka, qt_ref[0, 0, h * LANES:(h + 1) * LANES, :])

    def softmax_pv(j, h, st):
        vt = vt_ref[0, j, h * V_ROWS:(h + 1) * V_ROWS, :]
        m_prev = m_scr[h]
        m_new = jnp.maximum(m_prev, jnp.max(st, axis=0, keepdims=True))
        alpha = jnp.exp2(m_prev - m_new)
        pt = jnp.exp2((st - m_new).astype(vt.dtype))
        acc_scr[h] = alpha * acc_scr[h] + _dot(vt, pt)
        m_scr[h] = m_new

    @pl.when(iq == 0)
    def _():
        for h in range(hps):
            s_scr[h] = scores(0, h)

    def loop_body(j, carry):
        for h in range(hps):
            s_next = scores(j + 1, h)
            softmax_pv(j, h, s_scr[h])
            s_scr[h] = s_next
        return carry

    lax.fori_loop(0, iq, loop_body, 0)
    for h in range(hps):
        s_next = _dot(ka_ref[0:tq, h * LANES:(h + 1) * LANES], qn_ref[0, 0, h * LANES:(h + 1) * LANES, :])
        softmax_pv(iq, h, s_scr[h] + mb_ref[...])
        s_scr[h] = s_next

    for hp in range(hps // 2):
        outs = []
        for e in range(2):
            acc = acc_scr[2 * hp + e]
            outs.append(acc[0:hd, :] / acc[hd:hd + 1, :])
        o_ref[0, :, hp * LANES:(hp + 1) * LANES] = jnp.concatenate(outs, axis=0).T.astype(o_ref.dtype)


def _attention(qt, ka, vt, b, lp):
    tq, hps = ATTN_TQ, ATTN_HPS
    key_i = jnp.arange(tq)[:, None]
    query_i = jnp.arange(tq)[None, :]
    causal_bias = jnp.where(key_i <= query_i, 0.0, NEG_BIG).astype(F32)
    last_tile = lp // tq - 1
    return pl.pallas_call(
        functools.partial(_attn_body, tq=tq, hps=hps),
        out_shape=jax.ShapeDtypeStruct((b, lp, D_ATTN), BF16),
        grid=(b, ATTN_HEADS // hps, lp // tq),
        in_specs=[pl.BlockSpec((1, 1, hps * LANES, tq), lambda bi, hi, qi: (bi, qi, hi, 0)),
                  pl.BlockSpec((1, 1, hps * LANES, tq), lambda bi, hi, qi: (bi, jnp.minimum(qi + 1, last_tile), hi, 0)),
                  pl.BlockSpec(memory_space=pl.ANY),
                  pl.BlockSpec((1, lp // tq, hps * V_ROWS, tq), lambda bi, hi, qi: (bi, 0, hi, 0)),
                  pl.BlockSpec((tq, tq), lambda bi, hi, qi: (0, 0))],
        out_specs=pl.BlockSpec((1, tq, hps * ATTN_HEAD_DIM), lambda bi, hi, qi: (bi, qi, hi)),
        scratch_shapes=[pltpu.VMEM((lp, hps * LANES), BF16),
                        pltpu.SemaphoreType.DMA((2,)),
                        pltpu.VMEM((hps, 1, tq), F32),
                        pltpu.VMEM((hps, V_ROWS, tq), F32),
                        pltpu.VMEM((hps, tq, tq), F32)],
        compiler_params=_cparams(("parallel", "parallel", "arbitrary")),
        name="fox_attention",
    )(qt, qt, ka, vt, causal_bias)


def _causal_conv(x, xp_scr, w_ref, b_ref, first_tile, rows, taps):
    @pl.when(first_tile)
    def _():
        xp_scr[0:SUBLANES, :] = jnp.zeros((SUBLANES, x.shape[1]), F32)

    xp_scr[SUBLANES:SUBLANES + rows, :] = x
    y = b_ref[...] + w_ref[taps - 1:taps, :] * x
    for kk in range(taps - 1):
        r0 = SUBLANES - (taps - 1) + kk
        y = y + w_ref[kk:kk + 1, :] * xp_scr[r0:r0 + rows, :]
    xp_scr[0:SUBLANES, :] = x[rows - SUBLANES:rows, :]
    return y


def _ssd_body(xbc_ref, z_ref, fdt_ref, cw_ref, cb_ref, dtb_ref, alog_ref, dfull_ref, nw_ref,
              exp_ref, o_ref, xp_scr, st_scr):
    rows = SSD_ROWS
    first_tile = pl.program_id(1) == 0

    @pl.when(first_tile)
    def _():
        st_scr[...] = jnp.zeros(st_scr.shape, F32)

    y = _causal_conv(xbc_ref[0].astype(F32), xp_scr, cw_ref, cb_ref, first_tile, rows, SSD_CONV)
    xc = y * _sigmoid(y)
    for ci in range(rows // SSD_CHUNK):
        rs = slice(ci * SSD_CHUNK, (ci + 1) * SSD_CHUNK)
        o_ref[0, rs, :] = _ssd_chunk(xc[rs], z_ref[0, rs, :].astype(F32), fdt_ref[0, rs, :], dtb_ref, alog_ref,
                                     dfull_ref, nw_ref, exp_ref, st_scr).astype(o_ref.dtype)


def _ssd_chunk(xc, z, dt_raw, dtb_ref, alog_ref, dfull_ref, nw_ref, exp_ref, st_scr):
    q = SSD_CHUNK
    gs = D_SSD // SSD_GROUPS
    heads_per_group = SSD_HEADS // SSD_GROUPS

    lane = lax.broadcasted_iota(jnp.int32, (q, LANES), 1)
    dt_lane = (lane >= DT_LANE0) & (lane < DT_LANE0 + SSD_HEADS)
    dt = jnp.where(dt_lane, _softplus(dt_raw + dtb_ref[...]), 0.0)
    a = -jnp.exp(alog_ref[...])
    da = dt * a
    row = lax.broadcasted_iota(jnp.int32, (q, q), 0)
    col = lax.broadcasted_iota(jnp.int32, (q, q), 1)
    lower = row >= col
    a_cum = _dot_01_lhs(lower.astype(BF16), da)
    a_cum_t = a_cum.T
    half = lax.broadcasted_iota(jnp.int32, (q, LANES), 1) < SSD_HEAD_DIM

    outs = []
    for g in range(SSD_GROUPS):
        sl = slice(g * gs, (g + 1) * gs)
        expand = exp_ref[:, sl]
        dt_full = _dot_01_rhs(dt, expand)
        a_cum_full = _dot_01_rhs(a_cum, expand)
        a_last_full = a_cum_full[q - 1:q, :]
        xs = xc[:, sl]
        xdt = xs * dt_full
        xdt_b = xdt.astype(BF16)
        xde_b = (xdt * jnp.exp(a_last_full - a_cum_full)).astype(BF16)

        bm = xc[:, D_SSD + g * SSD_STATE:D_SSD + (g + 1) * SSD_STATE]
        cm = xc[:, D_SSD + SSD_GROUPS * SSD_STATE + g * SSD_STATE:
                D_SSD + SSD_GROUPS * SSD_STATE + (g + 1) * SSD_STATE]
        bm_b = bm.astype(BF16)
        cm_b = cm.astype(BF16)
        cb = _dot_nt(cm_b, bm_b)
        y_pairs = []
        for pair in range(heads_per_group // 2):
            xp = xdt_b[:, pair * LANES:(pair + 1) * LANES]
            ys = []
            for e in range(2):
                hl = DT_LANE0 + g * heads_per_group + 2 * pair + e
                seg = a_cum[:, hl:hl + 1] - a_cum_t[hl:hl + 1, :]
                dec = jnp.exp(jnp.where(lower, seg, -jnp.inf))
                ys.append(_dot((cb * dec).astype(BF16), xp))
            y_pairs.append(jnp.where(half, ys[0], ys[1]))
        y_diag = jnp.concatenate(y_pairs, axis=1)
        prev = st_scr[g]
        y_off = _dot(cm_b, prev.astype(BF16)) * jnp.exp(a_cum_full)
        st_scr[g] = prev * jnp.exp(a_last_full) + _dot(bm.T.astype(BF16), xde_b)
        yg = y_diag + y_off + dfull_ref[:, sl] * xs
        zg = z[:, sl]
        yg = yg * (zg * _sigmoid(zg))
        yg = yg * lax.rsqrt(jnp.mean(yg * yg, axis=-1, keepdims=True) + NORM_EPS)
        outs.append(yg * nw_ref[:, sl])
    return jnp.concatenate(outs, axis=1)


def _ssd(xbc, z, fdt, cw, cb, dtb, alog, dfull, nw, expand, b, lp):
    q = SSD_ROWS
    return pl.pallas_call(
        _ssd_body,
        out_shape=jax.ShapeDtypeStruct((b, lp, D_SSD), BF16),
        grid=(b, lp // q),
        in_specs=[pl.BlockSpec((1, q, D_XBC), lambda bi, ci: (bi, ci, 0)),
                  pl.BlockSpec((1, q, D_SSD), lambda bi, ci: (bi, ci, 0)),
                  pl.BlockSpec((1, q, FDT_COLS), lambda bi, ci: (bi, ci, 0)),
                  _const_spec((SSD_CONV, D_XBC)),
                  _const_spec((1, D_XBC)),
                  _const_spec((1, LANES)),
                  _const_spec((1, LANES)),
                  _const_spec((1, D_SSD)),
                  _const_spec((1, D_SSD)),
                  _const_spec((LANES, D_SSD))],
        out_specs=pl.BlockSpec((1, q, D_SSD), lambda bi, ci: (bi, ci, 0)),
        scratch_shapes=[pltpu.VMEM((SUBLANES + q, D_XBC), F32),
                        pltpu.VMEM((SSD_GROUPS, SSD_STATE, D_SSD // SSD_GROUPS), F32)],
        compiler_params=_cparams(("parallel", "arbitrary")),
        name="ssd",
    )(xbc, z, fdt, cw, cb, dtb, alog, dfull, nw, expand)


def _lru_body(xr_ref, gate_ref, cw_ref, cb_ref, w2_ref, ba_ref, bx_ref, lam_ref, o_ref,
              xp_scr, h_scr):
    rows = SCAN_ROWS
    first_tile = pl.program_id(1) == 0

    @pl.when(first_tile)
    def _():
        h_scr[...] = jnp.zeros(h_scr.shape, F32)

    xc = _causal_conv(xr_ref[0].astype(F32), xp_scr, cw_ref, cb_ref, first_tile, rows, LRU_CONV)
    xc_b = xc.astype(BF16)
    pre = [_dot(xc_b[:, j * LANES:(j + 1) * LANES], w2_ref[j]) for j in range(D_LRU // LANES)]
    pre_a = jnp.concatenate([p[:, :LANES] for p in pre], axis=1)
    pre_x = jnp.concatenate([p[:, LANES:] for p in pre], axis=1)
    r = _sigmoid(pre_a + ba_ref[...])
    i = _sigmoid(pre_x + bx_ref[...])
    log_a = LRU_C * r * _log_sigmoid(lam_ref[...])
    a = jnp.exp(log_a)
    mult = jnp.sqrt(-jnp.tanh(log_a) * (a * a + 1.0))
    row0 = lax.broadcasted_iota(jnp.int32, (SUBLANES, D_LRU), 0) == 0
    mult = jnp.concatenate([jnp.where(first_tile & row0, 1.0, mult[:SUBLANES]), mult[SUBLANES:]], axis=0)
    u = mult * (i * xc)

    groups = rows // SUBLANES
    a3 = a.reshape(groups, SUBLANES, D_LRU)
    u3 = u.reshape(groups, SUBLANES, D_LRU)
    sub = lax.broadcasted_iota(jnp.int32, (groups, SUBLANES, D_LRU), 1)
    d = 1
    while d < SUBLANES:
        keep = sub >= d
        a_s = jnp.where(keep, pltpu.roll(a3, d, 1), 1.0)
        u_s = jnp.where(keep, pltpu.roll(u3, d, 1), 0.0)
        u3 = a3 * u_s + u3
        a3 = a3 * a_s
        d *= 2
    h_prev = h_scr[0:1, :]
    hs = []
    for r in range(groups):
        h_r = a3[r] * h_prev + u3[r]
        hs.append(h_r)
        h_prev = h_r[SUBLANES - 1:SUBLANES, :]
    h = jnp.concatenate(hs, axis=0)
    h_scr[0:1, :] = h[rows - 1:rows, :]
    o_ref[0] = (h * jax.nn.gelu(gate_ref[0].astype(F32))).astype(o_ref.dtype)


def _lru(xr, gate, cw, cb, w2, ba, bx, lam, b, lp):
    rows = SCAN_ROWS
    return pl.pallas_call(
        _lru_body,
        out_shape=jax.ShapeDtypeStruct((b, lp, D_LRU), BF16),
        grid=(b, lp // rows),
        in_specs=[pl.BlockSpec((1, rows, D_LRU), lambda bi, ti: (bi, ti, 0)),
                  pl.BlockSpec((1, rows, D_LRU), lambda bi, ti: (bi, ti, 0)),
                  _const_spec((LRU_CONV, D_LRU)),
                  _const_spec((1, D_LRU)),
                  _const_spec((D_LRU // LANES, LANES, 2 * LANES)),
                  _const_spec((1, D_LRU)),
                  _const_spec((1, D_LRU)),
                  _const_spec((1, D_LRU))],
        out_specs=pl.BlockSpec((1, rows, D_LRU), lambda bi, ti: (bi, ti, 0)),
        scratch_shapes=[pltpu.VMEM((SUBLANES + rows, D_LRU), F32),
                        pltpu.VMEM((SUBLANES, D_LRU), F32)],
        compiler_params=_cparams(("parallel", "arbitrary")),
        name="rglru",
    )(xr, gate, cw, cb, w2, ba, bx, lam)


def _merge_body(h_ref, ya_ref, yb_ref, yc_ref, m_ref, wa_ref, wb_ref, wc_ref, wo_ref, o_ref):
    gate = lambda i: _sigmoid(m_ref[:, i * D_MODEL:(i + 1) * D_MODEL].astype(F32))
    mixed = gate(0) * _dot(ya_ref[...], wa_ref[...])
    mixed = mixed + gate(1) * _dot(yb_ref[...], wb_ref[...])
    mixed = mixed + gate(2) * _dot(yc_ref[...], wc_ref[...])
    o_ref[...] = h_ref[...] + _dot(mixed.astype(BF16), wo_ref[...])


def _merge(h, ya, yb, yc, m, wa, wb, wc, wo):
    t = h.shape[0]
    tm = MERGE_ROWS
    row_spec = lambda width: pl.BlockSpec((tm, width), lambda i: (i, 0))
    return pl.pallas_call(
        _merge_body,
        out_shape=jax.ShapeDtypeStruct((t, D_MODEL), F32),
        grid=(t // tm,),
        in_specs=[row_spec(D_MODEL), row_spec(D_ATTN), row_spec(D_SSD), row_spec(D_LRU),
                  row_spec(N_BRANCH * D_MODEL),
                  _const_spec((D_ATTN, D_MODEL)), _const_spec((D_SSD, D_MODEL)),
                  _const_spec((D_LRU, D_MODEL)), _const_spec((D_MODEL, D_MODEL))],
        out_specs=row_spec(D_MODEL),
        compiler_params=_cparams(("parallel",)),
        name="merge_out",
    )(h, ya, yb, yc, m, wa, wb, wc, wo)


def _prep_w_in(w_in):
    sizes = (D_ATTN, D_ATTN, D_ATTN, ATTN_HEADS, D_SSD, D_XBC, SSD_HEADS, D_LRU, D_LRU, N_BRANCH * D_MODEL)
    offs = [0]
    for s in sizes:
        offs.append(offs[-1] + s)
    part = lambda i: w_in[:, offs[i]:offs[i + 1]]
    q, k, v, f, z, xbc, dt, xr, gate, merge = (part(i) for i in range(10))
    pad = jnp.zeros((D_MODEL, FDT_COLS - ATTN_HEADS - SSD_HEADS), w_in.dtype)
    w = jnp.concatenate([k, z, xbc, xr, gate, merge, f, dt, pad], axis=1).astype(BF16)
    wt = jnp.concatenate([q.T, v.T], axis=0).astype(BF16)
    return w, wt


def _pad_lanes(vec, lane0):
    out = jnp.zeros((1, LANES), F32)
    return out.at[0, lane0:lane0 + vec.shape[0]].set(vec.astype(F32))


def _lru_gate_weights(w_a, w_x):
    def blockdiag_pairs(w):
        w = w.reshape(LRU_BLOCKS // 2, 2, LRU_BLOCK_DIM, LRU_BLOCK_DIM)
        zero = jnp.zeros_like(w[:, 0])
        top = jnp.concatenate([w[:, 0], zero], axis=2)
        bot = jnp.concatenate([zero, w[:, 1]], axis=2)
        return jnp.concatenate([top, bot], axis=1)
    return jnp.concatenate([blockdiag_pairs(w_a), blockdiag_pairs(w_x)], axis=2).astype(BF16)


def _head_expand():
    rows = jnp.arange(LANES)[:, None]
    cols = jnp.arange(D_SSD)[None, :]
    return (rows == DT_LANE0 + cols // SSD_HEAD_DIM).astype(BF16)


def kernel(x, meta_tokens, ffn1_norm, ffn1_w_gate_up, ffn1_w_down, mix_norm, w_in, fox_forget_bias,
           ssd_conv_w, ssd_conv_b, ssd_dt_bias, ssd_a_log, ssd_d, ssd_norm,
           lru_conv_w, lru_conv_b, lru_w_a, lru_b_a, lru_w_x, lru_b_x, lru_lambda,
           w_branch_attn, w_branch_ssd, w_branch_lru, w_out,
           ffn2_norm, ffn2_w_gate_up, ffn2_w_down, final_norm):
    b, s, d = x.shape
    depth = w_in.shape[0]
    length = N_META + s
    lp = -(-length // SEQ_ALIGN) * SEQ_ALIGN
    t = b * lp
    assert PROJ_ROWS == ATTN_TQ
    assert d == D_MODEL and t % FFN_ROWS == 0 and t % MERGE_ROWS == 0 and s % FFN_ROWS == 0

    meta = jnp.broadcast_to(meta_tokens.astype(x.dtype)[None], (b, N_META, d))
    h = jnp.concatenate([meta, x, jnp.zeros((b, lp - length, d), x.dtype)], axis=1).reshape(t, d)

    row = lambda vec: vec.astype(F32).reshape(1, -1)
    expand = _head_expand()
    pw = _aux_constants()
    fg = row(final_norm)
    for l in range(depth):
        h = _ffn(h, row(ffn1_norm[l]), ffn1_w_gate_up[l, :, :D_FF].astype(BF16),
                 ffn1_w_gate_up[l, :, D_FF:].astype(BF16), ffn1_w_down[l].astype(BF16), fg, False)

        w_std, w_t = _prep_w_in(w_in[l])
        qt, ka, vt, z, xbc, xr, gate, merge, fdt = _inproj(
            h, row(mix_norm[l]), w_std, w_t, _pad_lanes(fox_forget_bias[l], 0), pw, b, lp)
        y_a = _attention(qt, ka, vt, b, lp).reshape(t, D_ATTN)
        y_b = _ssd(xbc, z, fdt, ssd_conv_w[l].astype(F32), row(ssd_conv_b[l]),
                   _pad_lanes(ssd_dt_bias[l], DT_LANE0), _pad_lanes(ssd_a_log[l], DT_LANE0),
                   row(jnp.repeat(ssd_d[l], SSD_HEAD_DIM)), row(ssd_norm[l]), expand, b, lp).reshape(t, D_SSD)
        y_c = _lru(xr, gate, lru_conv_w[l].astype(F32), row(lru_conv_b[l]),
                   _lru_gate_weights(lru_w_a[l], lru_w_x[l]), row(lru_b_a[l]), row(lru_b_x[l]),
                   row(lru_lambda[l]), b, lp).reshape(t, D_LRU)
        h = _merge(h, y_a, y_b, y_c, merge.reshape(t, N_BRANCH * D_MODEL), w_branch_attn[l].astype(BF16),
                   w_branch_ssd[l].astype(BF16), w_branch_lru[l].astype(BF16), w_out[l].astype(BF16))

        ffn2 = (row(ffn2_norm[l]), ffn2_w_gate_up[l, :, :D_FF].astype(BF16),
                ffn2_w_gate_up[l, :, D_FF:].astype(BF16), ffn2_w_down[l].astype(BF16), fg)
        if l < depth - 1:
            h = _ffn(h, *ffn2, False)
    return _ffn_final(h, *ffn2, b, lp, s)
```

```python
import functools

import jax
import jax.numpy as jnp
from jax import lax
from jax.experimental import pallas as pl
from jax.experimental.pallas import tpu as pltpu

F32 = jnp.float32
BF16 = jnp.bfloat16

D_MODEL = 1024
N_META = 16
SSD_CHUNK = 128
NORM_EPS = 1e-6
ATTN_HEADS = 16
ATTN_HEAD_DIM = 64
D_ATTN = ATTN_HEADS * ATTN_HEAD_DIM
SSD_HEAD_DIM = 64
D_SSD = D_MODEL
SSD_HEADS = D_SSD // SSD_HEAD_DIM
SSD_GROUPS = 2
SSD_STATE = 128
SSD_CONV = 4
D_XBC = D_SSD + 2 * SSD_GROUPS * SSD_STATE
D_LRU = D_MODEL
LRU_BLOCKS = 16
LRU_BLOCK_DIM = D_LRU // LRU_BLOCKS
LRU_CONV = 4
LRU_C = 8.0
D_FF = 2816
N_BRANCH = 3

LANES = 128
SUBLANES = 8
VMEM_LIMIT_BYTES = 56 * 1024 * 1024

SEQ_ALIGN = 256
FFN_ROWS = 1024
FFN_CHUNK = 256
PROJ_ROWS = 256
PROJ_CHUNK = 512
ATTN_TQ = 256
ATTN_HPS = 16
MERGE_ROWS = 1024
SCAN_ROWS = 256
SSD_ROWS = 256
FDT_COLS = LANES
DT_LANE0 = ATTN_HEADS
NEG_BIG = -1e30
LOG2E = 1.4426950408889634
Q_SCALE = ATTN_HEAD_DIM ** -0.5 * LOG2E
AUX_PARTS = 3
AUX_SLOTS = 8


def _cparams(sem):
    return pltpu.CompilerParams(dimension_semantics=sem, vmem_limit_bytes=VMEM_LIMIT_BYTES)


def _const_spec(shape):
    nd = len(shape)
    return pl.BlockSpec(shape, lambda *_: (0,) * nd, pipeline_mode=pl.Buffered(1))


def _rms(x, g):
    ms = jnp.mean(x * x, axis=-1, keepdims=True)
    return (x * lax.rsqrt(ms + NORM_EPS)) * g


def _dot(a, b):
    return jnp.dot(a, b, preferred_element_type=F32)


def _dot_nt(a, b):
    return lax.dot_general(a, b, (((1,), (1,)), ((), ())), preferred_element_type=F32)


def _split3(x):
    hi = x.astype(BF16)
    r1 = x - hi.astype(F32)
    mid = r1.astype(BF16)
    lo = (r1 - mid.astype(F32)).astype(BF16)
    return hi, mid, lo


def _dot_01_lhs(sel, x):
    hi, mid, lo = _split3(x)
    return _dot(sel, hi) + _dot(sel, mid) + _dot(sel, lo)


def _dot_01_rhs(x, sel):
    hi, mid, lo = _split3(x)
    return _dot(hi, sel) + _dot(mid, sel) + _dot(lo, sel)


def _log_sigmoid(x):
    return -(jnp.maximum(-x, 0.0) + jnp.log1p(jnp.exp(-jnp.abs(x))))


def _softplus(x):
    return jnp.maximum(x, 0.0) + jnp.log1p(jnp.exp(-jnp.abs(x)))


def _sigmoid(x):
    return 1.0 / (1.0 + jnp.exp(-x))


def _ffn_body(x_ref, g_ref, wgu_ref, wd_ref, fg_ref, o_ref, a_scr, *, final_norm):
    x = x_ref[...]
    hn = _rms(x, g_ref[...]).astype(BF16)
    for c0 in range(0, D_FF, FFN_CHUNK):
        gate = _dot(hn, wgu_ref[:, c0:c0 + FFN_CHUNK])
        up = _dot(hn, wgu_ref[:, D_FF + c0:D_FF + c0 + FFN_CHUNK])
        a_scr[:, c0:c0 + FFN_CHUNK] = ((gate * _sigmoid(gate)) * up).astype(BF16)
    y = x + 0.5 * _dot(a_scr[...], wd_ref[...])
    if final_norm:
        y = _rms(y, fg_ref[...])
    o_ref[...] = y


def _ffn(x, g, wgu, wd, fg, final_norm):
    t = x.shape[0]
    tm = FFN_ROWS
    return pl.pallas_call(
        functools.partial(_ffn_body, final_norm=final_norm),
        out_shape=jax.ShapeDtypeStruct((t, D_MODEL), F32),
        grid=(t // tm,),
        in_specs=[pl.BlockSpec((tm, D_MODEL), lambda i: (i, 0))] + _ffn_weight_specs(),
        out_specs=pl.BlockSpec((tm, D_MODEL), lambda i: (i, 0)),
        scratch_shapes=[pltpu.VMEM((tm, D_FF), BF16)],
        compiler_params=_cparams(("parallel",)),
        name="ffn",
    )(x, g, wgu, wd, fg)


def _ffn_weight_specs():
    return [_const_spec((1, D_MODEL)), _const_spec((D_MODEL, 2 * D_FF)), _const_spec((D_FF, D_MODEL)),
            _const_spec((1, D_MODEL))]


def _ffn_final_body(x_ref, g_ref, wgu_ref, wd_ref, fg_ref, o_ref, a_scr):
    _ffn_body(x_ref.at[0], g_ref, wgu_ref, wd_ref, fg_ref, o_ref.at[0], a_scr, final_norm=True)


def _ffn_final(x, g, wgu, wd, fg, b, lp, s_out):
    tm = FFN_ROWS
    return pl.pallas_call(
        _ffn_final_body,
        out_shape=jax.ShapeDtypeStruct((b, s_out, D_MODEL), F32),
        grid=(b, s_out // tm),
        in_specs=[pl.BlockSpec((pl.Element(1), pl.Element(tm), pl.Element(D_MODEL)),
                               lambda bi, i: (bi, pl.multiple_of(N_META + i * tm, SUBLANES), 0))]
        + _ffn_weight_specs(),
        out_specs=pl.BlockSpec((1, tm, D_MODEL), lambda bi, i: (bi, i, 0)),
        scratch_shapes=[pltpu.VMEM((tm, D_FF), BF16)],
        compiler_params=_cparams(("parallel", "parallel")),
        name="ffn_final",
    )(x.reshape(b, lp, D_MODEL), g, wgu, wd, fg)


_PLAIN_GROUPS = (("z", D_SSD), ("xbc", D_XBC), ("xr", D_LRU), ("gate", D_LRU), ("merge", N_BRANCH * D_MODEL))
OFF_K = 0
OFF_PLAIN = D_ATTN
OFF_FDT = OFF_PLAIN + sum(w for _, w in _PLAIN_GROUPS)
N_PROJ = OFF_FDT + FDT_COLS
D_AUG = ATTN_HEADS * LANES
V_ROWS = ATTN_HEAD_DIM + 16
ONE_LANE = LANES - 1


def _inproj_body(x_ref, g_ref, w_ref, wt_ref, fb_ref, pw_ref, qt_ref, ka_ref, vt_ref,
                 z_ref, xbc_ref, xr_ref, gate_ref, merge_ref, fdt_ref, carry_scr):
    tm = x_ref.shape[1]

    @pl.when(pl.program_id(1) == 0)
    def _():
        carry_scr[...] = jnp.zeros(carry_scr.shape, F32)

    hn = _rms(x_ref[0], g_ref[...]).astype(BF16)

    def mm(c0, width):
        return _dot(hn, w_ref[:, c0:c0 + width])

    fdt = mm(OFF_FDT, FDT_COLS)
    fdt_ref[0] = fdt
    off = OFF_PLAIN
    for (_, width), o_ref in zip(_PLAIN_GROUPS, (z_ref, xbc_ref, xr_ref, gate_ref, merge_ref)):
        for c0 in range(0, width, PROJ_CHUNK):
            o_ref[0, :, c0:c0 + PROJ_CHUNK] = mm(off + c0, PROJ_CHUNK).astype(o_ref.dtype)
        off += width

    lane = lax.broadcasted_iota(jnp.int32, (tm, LANES), 1)
    lf = jnp.where(lane < ATTN_HEADS, _log_sigmoid(fdt + fb_ref[...]), 0.0)
    row = lax.broadcasted_iota(jnp.int32, (tm, tm), 0)
    col = lax.broadcasted_iota(jnp.int32, (tm, tm), 1)
    c = _dot_01_lhs((row >= col).astype(BF16), lf) + carry_scr[0:1, :]
    carry_scr[0:1, :] = c[tm - 1:tm, :]
    hi, mid, lo = (part.astype(F32) for part in _split3(c * LOG2E))
    cparts = (hi + pltpu.roll(mid, ATTN_HEADS, 1) + pltpu.roll(lo, 2 * ATTN_HEADS, 1)
              + jnp.where(lane == ONE_LANE, 1.0, 0.0))

    w_aux = _dot(cparts.astype(BF16), pw_ref[...])
    k_aux = w_aux[:, :LANES]
    q_aux_t = w_aux[:, LANES:].T

    first = lane < ATTN_HEAD_DIM
    for c0 in range(0, D_ATTN, PROJ_CHUNK):
        kv = mm(OFF_K + c0, PROJ_CHUNK)
        for pr in range(PROJ_CHUNK // LANES):
            pair = c0 // LANES + pr
            k_data = kv[:, pr * LANES:(pr + 1) * LANES]
            ka_ref[0, :, 2 * pair * LANES:(2 * pair + 1) * LANES] = jnp.where(
                first, k_data, k_aux).astype(ka_ref.dtype)
            ka_ref[0, :, (2 * pair + 1) * LANES:(2 * pair + 2) * LANES] = jnp.where(
                first, k_aux, k_data).astype(ka_ref.dtype)

    hd = ATTN_HEAD_DIM
    q_t = _dot_nt(wt_ref[0:D_ATTN, :], hn) * Q_SCALE
    v_t = _dot_nt(wt_ref[D_ATTN:2 * D_ATTN, :], hn)
    ones = jnp.ones((V_ROWS - hd, tm), vt_ref.dtype)
    for h in range(ATTN_HEADS):
        even = h % 2 == 0
        data0 = h * LANES + (0 if even else hd)
        aux0 = h * LANES + (hd if even else 0)
        a0 = (hd if even else 0) + AUX_SLOTS * (h // 2)
        before, after = AUX_SLOTS * (h // 2), hd - AUX_SLOTS * (h // 2 + 1)
        pieces = ([jnp.zeros((before, tm), F32)] if before else []) + [q_aux_t[a0:a0 + AUX_SLOTS, :]]
        pieces += [jnp.zeros((after, tm), F32)] if after else []
        qt_ref[0, 0, data0:data0 + hd, :] = q_t[h * hd:(h + 1) * hd, :].astype(qt_ref.dtype)
        qt_ref[0, 0, aux0:aux0 + hd, :] = jnp.concatenate(pieces, axis=0).astype(qt_ref.dtype)
        vt_ref[0, 0, h * V_ROWS:h * V_ROWS + hd, :] = v_t[h * hd:(h + 1) * hd, :].astype(vt_ref.dtype)
        vt_ref[0, 0, h * V_ROWS + hd:(h + 1) * V_ROWS, :] = ones


def _inproj(x, g, w, wt, fb, pw, b, lp):
    tm = PROJ_ROWS
    row_spec = lambda width: pl.BlockSpec((1, tm, width), lambda bi, ti: (bi, ti, 0))
    col_spec = lambda height: pl.BlockSpec((1, 1, height, tm), lambda bi, ti: (bi, ti, 0, 0))
    plain = [w_ for _, w_ in _PLAIN_GROUPS]
    out_shape = ([jax.ShapeDtypeStruct((b, lp // tm, D_AUG, tm), BF16), jax.ShapeDtypeStruct((b, lp, D_AUG), BF16),
                  jax.ShapeDtypeStruct((b, lp // tm, ATTN_HEADS * V_ROWS, tm), BF16)]
                 + [jax.ShapeDtypeStruct((b, lp, w_), BF16) for w_ in plain]
                 + [jax.ShapeDtypeStruct((b, lp, FDT_COLS), F32)])
    return pl.pallas_call(
        _inproj_body,
        out_shape=out_shape,
        grid=(b, lp // tm),
        in_specs=[row_spec(D_MODEL),
                  _const_spec((1, D_MODEL)),
                  _const_spec((D_MODEL, N_PROJ)),
                  _const_spec((2 * D_ATTN, D_MODEL)),
                  _const_spec((1, LANES)),
                  _const_spec((LANES, 2 * LANES))],
        out_specs=([col_spec(D_AUG), row_spec(D_AUG), col_spec(ATTN_HEADS * V_ROWS)]
                   + [row_spec(w_) for w_ in plain] + [row_spec(FDT_COLS)]),
        scratch_shapes=[pltpu.VMEM((SUBLANES, LANES), F32)],
        compiler_params=_cparams(("parallel", "arbitrary")),
        name="inproj",
    )(x.reshape(b, lp, D_MODEL), g, w, wt, fb, pw)


def _aux_constants():
    src = jnp.arange(LANES)[:, None]
    lane = jnp.arange(LANES)[None, :]
    half = jnp.where(lane >= ATTN_HEAD_DIM, 0, 1)
    slot = lane % ATTN_HEAD_DIM
    head = 2 * (slot // AUX_SLOTS) + half
    idx = slot % AUX_SLOTS
    part = src // ATTN_HEADS
    part_src = (part < AUX_PARTS) & (src % ATTN_HEADS == head)
    one_src = src == ONE_LANE
    key = (one_src & (idx < AUX_PARTS)).astype(F32) - (part_src & (idx == part + AUX_PARTS)).astype(F32)
    qry = (part_src & (idx == part)).astype(F32) + (one_src & (idx >= AUX_PARTS) & (idx < 2 * AUX_PARTS)).astype(F32)
    return jnp.concatenate([key, qry], axis=1).astype(BF16)


def _attn_body(qt_ref, qn_ref, ka_hbm, vt_ref, mb_ref, o_ref, ka_ref, k_sem, m_scr, acc_scr, s_scr, *, tq, hps):
    bi = pl.program_id(0)
    hi = pl.program_id(1)
    iq = pl.program_id(2)

    def key_copy(blk, slot):
        rows = pl.ds(pl.multiple_of(blk * tq, tq), tq)
        return pltpu.make_async_copy(ka_hbm.at[bi, rows, pl.ds(pl.multiple_of(hi * hps * LANES, LANES), hps * LANES)],
                                     ka_ref.at[rows, :], k_sem.at[slot])

    @pl.when(iq == 0)
    def _():
        first_block = key_copy(0, 0)
        first_block.start()
        first_block.wait()

    @pl.when(iq > 0)
    def _():
        key_copy(iq, 1).wait()

    @pl.when(iq + 1 < pl.num_programs(2))
    def _():
        key_copy(iq + 1, 1).start()

    m_scr[...] = jnp.full(m_scr.shape, NEG_BIG, F32)
    acc_scr[...] = jnp.zeros(acc_scr.shape, F32)
    hd = ATTN_HEAD_DIM

    def scores(j, h):
        ks = pl.multiple_of(j * tq, tq)
        ka = ka_ref[pl.ds(ks, tq), h * LANES:(h + 1) * LANES]
        return _dot(ka, qt_ref[0, 0, h * LANES:(h + 1) * LANES, :])

    def softmax_pv(j, h, st):
        vt = vt_ref[0, j, h * V_ROWS:(h + 1) * V_ROWS, :]
        m_prev = m_scr[h]
        m_new = jnp.maximum(m_prev, jnp.max(st, axis=0, keepdims=True))
        alpha = jnp.exp2(m_prev - m_new)
        pt = jnp.exp2((st - m_new).astype(vt.dtype))
        acc_scr[h] = alpha * acc_scr[h] + _dot(vt, pt)
        m_scr[h] = m_new

    @pl.when(iq == 0)
    def _():
        for h in range(hps):
            s_scr[h] = scores(0, h)

    def loop_body(j, carry):
        for h in range(hps):
            s_next = scores(j + 1, h)
            softmax_pv(j, h, s_scr[h])
            s_scr[h] = s_next
        return carry

    lax.fori_loop(0, iq, loop_body, 0)
    for h in range(hps):
        s_next = _dot(ka_ref[0:tq, h * LANES:(h + 1) * LANES], qn_ref[0, 0, h * LANES:(h + 1) * LANES, :])
        softmax_pv(iq, h, s_scr[h] + mb_ref[...])
        s_scr[h] = s_next

    for hp in range(hps // 2):
        outs = []
        for e in range(2):
            acc = acc_scr[2 * hp + e]
            outs.append(acc[0:hd, :] / acc[hd:hd + 1, :])
        o_ref[0, :, hp * LANES:(hp + 1) * LANES] = jnp.concatenate(outs, axis=0).T.astype(o_ref.dtype)


def _attention(qt, ka, vt, b, lp):
    tq, hps = ATTN_TQ, ATTN_HPS
    key_i = jnp.arange(tq)[:, None]
    query_i = jnp.arange(tq)[None, :]
    causal_bias = jnp.where(key_i <= query_i, 0.0, NEG_BIG).astype(F32)
    last_tile = lp // tq - 1
    return pl.pallas_call(
        functools.partial(_attn_body, tq=tq, hps=hps),
        out_shape=jax.ShapeDtypeStruct((b, lp, D_ATTN), BF16),
        grid=(b, ATTN_HEADS // hps, lp // tq),
        in_specs=[pl.BlockSpec((1, 1, hps * LANES, tq), lambda bi, hi, qi: (bi, qi, hi, 0)),
                  pl.BlockSpec((1, 1, hps * LANES, tq), lambda bi, hi, qi: (bi, jnp.minimum(qi + 1, last_tile), hi, 0)),
                  pl.BlockSpec(memory_space=pl.ANY),
                  pl.BlockSpec((1, lp // tq, hps * V_ROWS, tq), lambda bi, hi, qi: (bi, 0, hi, 0)),
                  pl.BlockSpec((tq, tq), lambda bi, hi, qi: (0, 0))],
        out_specs=pl.BlockSpec((1, tq, hps * ATTN_HEAD_DIM), lambda bi, hi, qi: (bi, qi, hi)),
        scratch_shapes=[pltpu.VMEM((lp, hps * LANES), BF16),
                        pltpu.SemaphoreType.DMA((2,)),
                        pltpu.VMEM((hps, 1, tq), F32),
                        pltpu.VMEM((hps, V_ROWS, tq), F32),
                        pltpu.VMEM((hps, tq, tq), F32)],
        compiler_params=_cparams(("parallel", "parallel", "arbitrary")),
        name="fox_attention",
    )(qt, qt, ka, vt, causal_bias)


def _causal_conv(x, xp_scr, w_ref, b_ref, first_tile, rows, taps):
    @pl.when(first_tile)
    def _():
        xp_scr[0:SUBLANES, :] = jnp.zeros((SUBLANES, x.shape[1]), F32)

    xp_scr[SUBLANES:SUBLANES + rows, :] = x
    y = b_ref[...] + w_ref[taps - 1:taps, :] * x
    for kk in range(taps - 1):
        r0 = SUBLANES - (taps - 1) + kk
        y = y + w_ref[kk:kk + 1, :] * xp_scr[r0:r0 + rows, :]
    xp_scr[0:SUBLANES, :] = x[rows - SUBLANES:rows, :]
    return y


def _ssd_body(xbc_ref, z_ref, fdt_ref, cw_ref, cb_ref, dtb_ref, alog_ref, dfull_ref, nw_ref,
              exp_ref, o_ref, xp_scr, st_scr):
    rows = SSD_ROWS
    first_tile = pl.program_id(1) == 0

    @pl.when(first_tile)
    def _():
        st_scr[...] = jnp.zeros(st_scr.shape, F32)

    y = _causal_conv(xbc_ref[0].astype(F32), xp_scr, cw_ref, cb_ref, first_tile, rows, SSD_CONV)
    xc = y * _sigmoid(y)
    for ci in range(rows // SSD_CHUNK):
        rs = slice(ci * SSD_CHUNK, (ci + 1) * SSD_CHUNK)
        o_ref[0, rs, :] = _ssd_chunk(xc[rs], z_ref[0, rs, :].astype(F32), fdt_ref[0, rs, :], dtb_ref, alog_ref,
                                     dfull_ref, nw_ref, exp_ref, st_scr).astype(o_ref.dtype)


def _ssd_chunk(xc, z, dt_raw, dtb_ref, alog_ref, dfull_ref, nw_ref, exp_ref, st_scr):
    q = SSD_CHUNK
    gs = D_SSD // SSD_GROUPS
    heads_per_group = SSD_HEADS // SSD_GROUPS

    lane = lax.broadcasted_iota(jnp.int32, (q, LANES), 1)
    dt_lane = (lane >= DT_LANE0) & (lane < DT_LANE0 + SSD_HEADS)
    dt = jnp.where(dt_lane, _softplus(dt_raw + dtb_ref[...]), 0.0)
    a = -jnp.exp(alog_ref[...])
    da = dt * a
    row = lax.broadcasted_iota(jnp.int32, (q, q), 0)
    col = lax.broadcasted_iota(jnp.int32, (q, q), 1)
    lower = row >= col
    a_cum = _dot_01_lhs(lower.astype(BF16), da)
    a_cum_t = a_cum.T
    half = lax.broadcasted_iota(jnp.int32, (q, LANES), 1) < SSD_HEAD_DIM

    outs = []
    for g in range(SSD_GROUPS):
        sl = slice(g * gs, (g + 1) * gs)
        expand = exp_ref[:, sl]
        dt_full = _dot_01_rhs(dt, expand)
        a_cum_full = _dot_01_rhs(a_cum, expand)
        a_last_full = a_cum_full[q - 1:q, :]
        xs = xc[:, sl]
        xdt = xs * dt_full
        xdt_b = xdt.astype(BF16)
        xde_b = (xdt * jnp.exp(a_last_full - a_cum_full)).astype(BF16)

        bm = xc[:, D_SSD + g * SSD_STATE:D_SSD + (g + 1) * SSD_STATE]
        cm = xc[:, D_SSD + SSD_GROUPS * SSD_STATE + g * SSD_STATE:
                D_SSD + SSD_GROUPS * SSD_STATE + (g + 1) * SSD_STATE]
        bm_b = bm.astype(BF16)
        cm_b = cm.astype(BF16)
        cb = _dot_nt(cm_b, bm_b)
        y_pairs = []
        for pair in range(heads_per_group // 2):
            xp = xdt_b[:, pair * LANES:(pair + 1) * LANES]
            ys = []
            for e in range(2):
                hl = DT_LANE0 + g * heads_per_group + 2 * pair + e
                seg = a_cum[:, hl:hl + 1] - a_cum_t[hl:hl + 1, :]
                dec = jnp.exp(jnp.where(lower, seg, -jnp.inf))
                ys.append(_dot((cb * dec).astype(BF16), xp))
            y_pairs.append(jnp.where(half, ys[0], ys[1]))
        y_diag = jnp.concatenate(y_pairs, axis=1)
        prev = st_scr[g]
        y_off = _dot(cm_b, prev.astype(BF16)) * jnp.exp(a_cum_full)
        st_scr[g] = prev * jnp.exp(a_last_full) + _dot(bm.T.astype(BF16), xde_b)
        yg = y_diag + y_off + dfull_ref[:, sl] * xs
        zg = z[:, sl]
        yg = yg * (zg * _sigmoid(zg))
        yg = yg * lax.rsqrt(jnp.mean(yg * yg, axis=-1, keepdims=True) + NORM_EPS)
        outs.append(yg * nw_ref[:, sl])
    return jnp.concatenate(outs, axis=1)


def _ssd(xbc, z, fdt, cw, cb, dtb, alog, dfull, nw, expand, b, lp):
    q = SSD_ROWS
    return pl.pallas_call(
        _ssd_body,
        out_shape=jax.ShapeDtypeStruct((b, lp, D_SSD), BF16),
        grid=(b, lp // q),
        in_specs=[pl.BlockSpec((1, q, D_XBC), lambda bi, ci: (bi, ci, 0)),
                  pl.BlockSpec((1, q, D_SSD), lambda bi, ci: (bi, ci, 0)),
                  pl.BlockSpec((1, q, FDT_COLS), lambda bi, ci: (bi, ci, 0)),
                  _const_spec((SSD_CONV, D_XBC)),
                  _const_spec((1, D_XBC)),
                  _const_spec((1, LANES)),
                  _const_spec((1, LANES)),
                  _const_spec((1, D_SSD)),
                  _const_spec((1, D_SSD)),
                  _const_spec((LANES, D_SSD))],
        out_specs=pl.BlockSpec((1, q, D_SSD), lambda bi, ci: (bi, ci, 0)),
        scratch_shapes=[pltpu.VMEM((SUBLANES + q, D_XBC), F32),
                        pltpu.VMEM((SSD_GROUPS, SSD_STATE, D_SSD // SSD_GROUPS), F32)],
        compiler_params=_cparams(("parallel", "arbitrary")),
        name="ssd",
    )(xbc, z, fdt, cw, cb, dtb, alog, dfull, nw, expand)


def _lru_body(xr_ref, gate_ref, cw_ref, cb_ref, w2_ref, ba_ref, bx_ref, lam_ref, o_ref,
              xp_scr, h_scr):
    rows = SCAN_ROWS
    first_tile = pl.program_id(1) == 0

    @pl.when(first_tile)
    def _():
        h_scr[...] = jnp.zeros(h_scr.shape, F32)

    xc = _causal_conv(xr_ref[0].astype(F32), xp_scr, cw_ref, cb_ref, first_tile, rows, LRU_CONV)
    xc_b = xc.astype(BF16)
    pre = [_dot(xc_b[:, j * LANES:(j + 1) * LANES], w2_ref[j]) for j in range(D_LRU // LANES)]
    pre_a = jnp.concatenate([p[:, :LANES] for p in pre], axis=1)
    pre_x = jnp.concatenate([p[:, LANES:] for p in pre], axis=1)
    r = _sigmoid(pre_a + ba_ref[...])
    i = _sigmoid(pre_x + bx_ref[...])
    log_a = LRU_C * r * _log_sigmoid(lam_ref[...])
    a = jnp.exp(log_a)
    mult = jnp.sqrt(-jnp.tanh(log_a) * (a * a + 1.0))
    row0 = lax.broadcasted_iota(jnp.int32, (SUBLANES, D_LRU), 0) == 0
    mult = jnp.concatenate([jnp.where(first_tile & row0, 1.0, mult[:SUBLANES]), mult[SUBLANES:]], axis=0)
    u = mult * (i * xc)

    groups = rows // SUBLANES
    a3 = a.reshape(groups, SUBLANES, D_LRU)
    u3 = u.reshape(groups, SUBLANES, D_LRU)
    sub = lax.broadcasted_iota(jnp.int32, (groups, SUBLANES, D_LRU), 1)
    d = 1
    while d < SUBLANES:
        keep = sub >= d
        a_s = jnp.where(keep, pltpu.roll(a3, d, 1), 1.0)
        u_s = jnp.where(keep, pltpu.roll(u3, d, 1), 0.0)
        u3 = a3 * u_s + u3
        a3 = a3 * a_s
        d *= 2
    h_prev = h_scr[0:1, :]
    hs = []
    for r in range(groups):
        h_r = a3[r] * h_prev + u3[r]
        hs.append(h_r)
        h_prev = h_r[SUBLANES - 1:SUBLANES, :]
    h = jnp.concatenate(hs, axis=0)
    h_scr[0:1, :] = h[rows - 1:rows, :]
    o_ref[0] = (h * jax.nn.gelu(gate_ref[0].astype(F32))).astype(o_ref.dtype)


def _lru(xr, gate, cw, cb, w2, ba, bx, lam, b, lp):
    rows = SCAN_ROWS
    return pl.pallas_call(
        _lru_body,
        out_shape=jax.ShapeDtypeStruct((b, lp, D_LRU), BF16),
        grid=(b, lp // rows),
        in_specs=[pl.BlockSpec((1, rows, D_LRU), lambda bi, ti: (bi, ti, 0)),
                  pl.BlockSpec((1, rows, D_LRU), lambda bi, ti: (bi, ti, 0)),
                  _const_spec((LRU_CONV, D_LRU)),
                  _const_spec((1, D_LRU)),
                  _const_spec((D_LRU // LANES, LANES, 2 * LANES)),
                  _const_spec((1, D_LRU)),
                  _const_spec((1, D_LRU)),
                  _const_spec((1, D_LRU))],
        out_specs=pl.BlockSpec((1, rows, D_LRU), lambda bi, ti: (bi, ti, 0)),
        scratch_shapes=[pltpu.VMEM((SUBLANES + rows, D_LRU), F32),
                        pltpu.VMEM((SUBLANES, D_LRU), F32)],
        compiler_params=_cparams(("parallel", "arbitrary")),
        name="rglru",
    )(xr, gate, cw, cb, w2, ba, bx, lam)


def _merge_body(h_ref, ya_ref, yb_ref, yc_ref, m_ref, wa_ref, wb_ref, wc_ref, wo_ref, o_ref):
    gate = lambda i: _sigmoid(m_ref[:, i * D_MODEL:(i + 1) * D_MODEL].astype(F32))
    mixed = gate(0) * _dot(ya_ref[...], wa_ref[...])
    mixed = mixed + gate(1) * _dot(yb_ref[...], wb_ref[...])
    mixed = mixed + gate(2) * _dot(yc_ref[...], wc_ref[...])
    o_ref[...] = h_ref[...] + _dot(mixed.astype(BF16), wo_ref[...])


def _merge(h, ya, yb, yc, m, wa, wb, wc, wo):
    t = h.shape[0]
    tm = MERGE_ROWS
    row_spec = lambda width: pl.BlockSpec((tm, width), lambda i: (i, 0))
    return pl.pallas_call(
        _merge_body,
        out_shape=jax.ShapeDtypeStruct((t, D_MODEL), F32),
        grid=(t // tm,),
        in_specs=[row_spec(D_MODEL), row_spec(D_ATTN), row_spec(D_SSD), row_spec(D_LRU),
                  row_spec(N_BRANCH * D_MODEL),
                  _const_spec((D_ATTN, D_MODEL)), _const_spec((D_SSD, D_MODEL)),
                  _const_spec((D_LRU, D_MODEL)), _const_spec((D_MODEL, D_MODEL))],
        out_specs=row_spec(D_MODEL),
        compiler_params=_cparams(("parallel",)),
        name="merge_out",
    )(h, ya, yb, yc, m, wa, wb, wc, wo)


def _prep_w_in(w_in):
    sizes = (D_ATTN, D_ATTN, D_ATTN, ATTN_HEADS, D_SSD, D_XBC, SSD_HEADS, D_LRU, D_LRU, N_BRANCH * D_MODEL)
    offs = [0]
    for s in sizes:
        offs.append(offs[-1] + s)
    part = lambda i: w_in[:, offs[i]:offs[i + 1]]
    q, k, v, f, z, xbc, dt, xr, gate, merge = (part(i) for i in range(10))
    pad = jnp.zeros((D_MODEL, FDT_COLS - ATTN_HEADS - SSD_HEADS), w_in.dtype)
    w = jnp.concatenate([k, z, xbc, xr, gate, merge, f, dt, pad], axis=1).astype(BF16)
    wt = jnp.concatenate([q.T, v.T], axis=0).astype(BF16)
    return w, wt


def _pad_lanes(vec, lane0):
    out = jnp.zeros((1, LANES), F32)
    return out.at[0, lane0:lane0 + vec.shape[0]].set(vec.astype(F32))


def _lru_gate_weights(w_a, w_x):
    def blockdiag_pairs(w):
        w = w.reshape(LRU_BLOCKS // 2, 2, LRU_BLOCK_DIM, LRU_BLOCK_DIM)
        zero = jnp.zeros_like(w[:, 0])
        top = jnp.concatenate([w[:, 0], zero], axis=2)
        bot = jnp.concatenate([zero, w[:, 1]], axis=2)
        return jnp.concatenate([top, bot], axis=1)
    return jnp.concatenate([blockdiag_pairs(w_a), blockdiag_pairs(w_x)], axis=2).astype(BF16)


def _head_expand():
    rows = jnp.arange(LANES)[:, None]
    cols = jnp.arange(D_SSD)[None, :]
    return (rows == DT_LANE0 + cols // SSD_HEAD_DIM).astype(BF16)


def kernel(x, meta_tokens, ffn1_norm, ffn1_w_gate_up, ffn1_w_down, mix_norm, w_in, fox_forget_bias,
           ssd_conv_w, ssd_conv_b, ssd_dt_bias, ssd_a_log, ssd_d, ssd_norm,
           lru_conv_w, lru_conv_b, lru_w_a, lru_b_a, lru_w_x, lru_b_x, lru_lambda,
           w_branch_attn, w_branch_ssd, w_branch_lru, w_out,
           ffn2_norm, ffn2_w_gate_up, ffn2_w_down, final_norm):
    b, s, d = x.shape
    depth = w_in.shape[0]
    length = N_META + s
    lp = -(-length // SEQ_ALIGN) * SEQ_ALIGN
    t = b * lp
    assert PROJ_ROWS == ATTN_TQ
    assert d == D_MODEL and t % FFN_ROWS == 0 and t % MERGE_ROWS == 0 and s % FFN_ROWS == 0

    meta = jnp.broadcast_to(meta_tokens.astype(x.dtype)[None], (b, N_META, d))
    h = jnp.concatenate([meta, x, jnp.zeros((b, lp - length, d), x.dtype)], axis=1).reshape(t, d)

    row = lambda vec: vec.astype(F32).reshape(1, -1)
    expand = _head_expand()
    pw = _aux_constants()
    fg = row(final_norm)
    for l in range(depth):
        h = _ffn(h, row(ffn1_norm[l]), ffn1_w_gate_up[l].astype(BF16), ffn1_w_down[l].astype(BF16), fg, False)

        w_std, w_t = _prep_w_in(w_in[l])
        qt, ka, vt, z, xbc, xr, gate, merge, fdt = _inproj(
            h, row(mix_norm[l]), w_std, w_t, _pad_lanes(fox_forget_bias[l], 0), pw, b, lp)
        y_a = _attention(qt, ka, vt, b, lp).reshape(t, D_ATTN)
        y_b = _ssd(xbc, z, fdt, ssd_conv_w[l].astype(F32), row(ssd_conv_b[l]),
                   _pad_lanes(ssd_dt_bias[l], DT_LANE0), _pad_lanes(ssd_a_log[l], DT_LANE0),
                   row(jnp.repeat(ssd_d[l], SSD_HEAD_DIM)), row(ssd_norm[l]), expand, b, lp).reshape(t, D_SSD)
        y_c = _lru(xr, gate, lru_conv_w[l].astype(F32), row(lru_conv_b[l]),
                   _lru_gate_weights(lru_w_a[l], lru_w_x[l]), row(lru_b_a[l]), row(lru_b_x[l]),
                   row(lru_lambda[l]), b, lp).reshape(t, D_LRU)
        h = _merge(h, y_a, y_b, y_c, merge.reshape(t, N_BRANCH * D_MODEL), w_branch_attn[l].astype(BF16),
                   w_branch_ssd[l].astype(BF16), w_branch_lru[l].astype(BF16), w_out[l].astype(BF16))

        ffn2 = (row(ffn2_norm[l]), ffn2_w_gate_up[l].astype(BF16), ffn2_w_down[l].astype(BF16), fg)
        if l < depth - 1:
            h = _ffn(h, *ffn2, False)
    return _ffn_final(h, *ffn2, b, lp, s)
```

```python
import functools

import jax
import jax.numpy as jnp
from jax import lax
from jax.experimental import pallas as pl
from jax.experimental.pallas import tpu as pltpu

F32 = jnp.float32
BF16 = jnp.bfloat16

D_MODEL = 1024
N_META = 16
SSD_CHUNK = 128
NORM_EPS = 1e-6
ATTN_HEADS = 16
ATTN_HEAD_DIM = 64
D_ATTN = ATTN_HEADS * ATTN_HEAD_DIM
SSD_HEAD_DIM = 64
D_SSD = D_MODEL
SSD_HEADS = D_SSD // SSD_HEAD_DIM
SSD_GROUPS = 2
SSD_STATE = 128
SSD_CONV = 4
D_XBC = D_SSD + 2 * SSD_GROUPS * SSD_STATE
D_LRU = D_MODEL
LRU_BLOCKS = 16
LRU_BLOCK_DIM = D_LRU // LRU_BLOCKS
LRU_CONV = 4
LRU_C = 8.0
D_FF = 2816
N_BRANCH = 3

LANES = 128
SUBLANES = 8
VMEM_LIMIT_BYTES = 56 * 1024 * 1024

SEQ_ALIGN = 256
FFN_ROWS = 1024
FFN_CHUNK = 256
PROJ_ROWS = 256
PROJ_CHUNK = 512
ATTN_TQ = 256
ATTN_HPS = 16
MERGE_ROWS = 1024
SCAN_ROWS = 256
SSD_ROWS = 256
FDT_COLS = LANES
DT_LANE0 = ATTN_HEADS
NEG_BIG = -1e30
LOG2E = 1.4426950408889634
Q_SCALE = ATTN_HEAD_DIM ** -0.5 * LOG2E
AUX_PARTS = 3
AUX_SLOTS = 8


def _cparams(sem):
    return pltpu.CompilerParams(dimension_semantics=sem, vmem_limit_bytes=VMEM_LIMIT_BYTES)


def _const_spec(shape):
    nd = len(shape)
    return pl.BlockSpec(shape, lambda *_: (0,) * nd, pipeline_mode=pl.Buffered(1))


def _rms(x, g):
    ms = jnp.mean(x * x, axis=-1, keepdims=True)
    return (x * lax.rsqrt(ms + NORM_EPS)) * g


def _dot(a, b):
    return jnp.dot(a, b, preferred_element_type=F32)


def _dot_nt(a, b):
    return lax.dot_general(a, b, (((1,), (1,)), ((), ())), preferred_element_type=F32)


def _split3(x):
    hi = x.astype(BF16)
    r1 = x - hi.astype(F32)
    mid = r1.astype(BF16)
    lo = (r1 - mid.astype(F32)).astype(BF16)
    return hi, mid, lo


def _dot_01_lhs(sel, x):
    hi, mid, lo = _split3(x)
    return _dot(sel, hi) + _dot(sel, mid) + _dot(sel, lo)


def _dot_01_rhs(x, sel):
    hi, mid, lo = _split3(x)
    return _dot(hi, sel) + _dot(mid, sel) + _dot(lo, sel)


def _log_sigmoid(x):
    return -(jnp.maximum(-x, 0.0) + jnp.log1p(jnp.exp(-jnp.abs(x))))


def _softplus(x):
    return jnp.maximum(x, 0.0) + jnp.log1p(jnp.exp(-jnp.abs(x)))


def _sigmoid(x):
    return 1.0 / (1.0 + jnp.exp(-x))


def _ffn_body(x_ref, g_ref, wgu_ref, wd_ref, fg_ref, o_ref, a_scr, *, final_norm):
    x = x_ref[...]
    hn = _rms(x, g_ref[...]).astype(BF16)
    for c0 in range(0, D_FF, FFN_CHUNK):
        gate = _dot(hn, wgu_ref[:, c0:c0 + FFN_CHUNK])
        up = _dot(hn, wgu_ref[:, D_FF + c0:D_FF + c0 + FFN_CHUNK])
        a_scr[:, c0:c0 + FFN_CHUNK] = ((gate * _sigmoid(gate)) * up).astype(BF16)
    y = x + 0.5 * _dot(a_scr[...], wd_ref[...])
    if final_norm:
        y = _rms(y, fg_ref[...])
    o_ref[...] = y


def _ffn(x, g, wgu, wd, fg, final_norm):
    t = x.shape[0]
    tm = FFN_ROWS
    return pl.pallas_call(
        functools.partial(_ffn_body, final_norm=final_norm),
        out_shape=jax.ShapeDtypeStruct((t, D_MODEL), F32),
        grid=(t // tm,),
        in_specs=[pl.BlockSpec((tm, D_MODEL), lambda i: (i, 0))] + _ffn_weight_specs(),
        out_specs=pl.BlockSpec((tm, D_MODEL), lambda i: (i, 0)),
        scratch_shapes=[pltpu.VMEM((tm, D_FF), BF16)],
        compiler_params=_cparams(("parallel",)),
        name="ffn",
    )(x, g, wgu, wd, fg)


def _ffn_weight_specs():
    return [_const_spec((1, D_MODEL)), _const_spec((D_MODEL, 2 * D_FF)), _const_spec((D_FF, D_MODEL)),
            _const_spec((1, D_MODEL))]


def _ffn_final_body(x_ref, g_ref, wgu_ref, wd_ref, fg_ref, o_ref, a_scr):
    _ffn_body(x_ref.at[0], g_ref, wgu_ref, wd_ref, fg_ref, o_ref.at[0], a_scr, final_norm=True)


def _ffn_final(x, g, wgu, wd, fg, b, lp, s_out):
    tm = FFN_ROWS
    return pl.pallas_call(
        _ffn_final_body,
        out_shape=jax.ShapeDtypeStruct((b, s_out, D_MODEL), F32),
        grid=(b, s_out // tm),
        in_specs=[pl.BlockSpec((pl.Element(1), pl.Element(tm), pl.Element(D_MODEL)),
                               lambda bi, i: (bi, pl.multiple_of(N_META + i * tm, SUBLANES), 0))]
        + _ffn_weight_specs(),
        out_specs=pl.BlockSpec((1, tm, D_MODEL), lambda bi, i: (bi, i, 0)),
        scratch_shapes=[pltpu.VMEM((tm, D_FF), BF16)],
        compiler_params=_cparams(("parallel", "parallel")),
        name="ffn_final",
    )(x.reshape(b, lp, D_MODEL), g, wgu, wd, fg)


_PLAIN_GROUPS = (("z", D_SSD), ("xbc", D_XBC), ("xr", D_LRU), ("gate", D_LRU), ("merge", N_BRANCH * D_MODEL))
OFF_K = 0
OFF_PLAIN = D_ATTN
OFF_FDT = OFF_PLAIN + sum(w for _, w in _PLAIN_GROUPS)
N_PROJ = OFF_FDT + FDT_COLS
D_AUG = ATTN_HEADS * LANES
V_ROWS = ATTN_HEAD_DIM + 16
ONE_LANE = LANES - 1


def _inproj_body(x_ref, g_ref, w_ref, wt_ref, fb_ref, pw_ref, qt_ref, ka_ref, vt_ref,
                 z_ref, xbc_ref, xr_ref, gate_ref, merge_ref, fdt_ref, carry_scr):
    tm = x_ref.shape[1]

    @pl.when(pl.program_id(1) == 0)
    def _():
        carry_scr[...] = jnp.zeros(carry_scr.shape, F32)

    hn = _rms(x_ref[0], g_ref[...]).astype(BF16)

    def mm(c0, width):
        return _dot(hn, w_ref[:, c0:c0 + width])

    fdt = mm(OFF_FDT, FDT_COLS)
    fdt_ref[0] = fdt
    off = OFF_PLAIN
    for (_, width), o_ref in zip(_PLAIN_GROUPS, (z_ref, xbc_ref, xr_ref, gate_ref, merge_ref)):
        for c0 in range(0, width, PROJ_CHUNK):
            o_ref[0, :, c0:c0 + PROJ_CHUNK] = mm(off + c0, PROJ_CHUNK).astype(o_ref.dtype)
        off += width

    lane = lax.broadcasted_iota(jnp.int32, (tm, LANES), 1)
    lf = jnp.where(lane < ATTN_HEADS, _log_sigmoid(fdt + fb_ref[...]), 0.0)
    row = lax.broadcasted_iota(jnp.int32, (tm, tm), 0)
    col = lax.broadcasted_iota(jnp.int32, (tm, tm), 1)
    c = _dot_01_lhs((row >= col).astype(BF16), lf) + carry_scr[0:1, :]
    carry_scr[0:1, :] = c[tm - 1:tm, :]
    hi, mid, lo = (part.astype(F32) for part in _split3(c * LOG2E))
    cparts = (hi + pltpu.roll(mid, ATTN_HEADS, 1) + pltpu.roll(lo, 2 * ATTN_HEADS, 1)
              + jnp.where(lane == ONE_LANE, 1.0, 0.0))

    w_aux = _dot(cparts.astype(BF16), pw_ref[...])
    k_aux = w_aux[:, :LANES]
    q_aux_t = w_aux[:, LANES:].T

    first = lane < ATTN_HEAD_DIM
    for c0 in range(0, D_ATTN, PROJ_CHUNK):
        kv = mm(OFF_K + c0, PROJ_CHUNK)
        for pr in range(PROJ_CHUNK // LANES):
            pair = c0 // LANES + pr
            k_data = kv[:, pr * LANES:(pr + 1) * LANES]
            ka_ref[0, :, 2 * pair * LANES:(2 * pair + 1) * LANES] = jnp.where(
                first, k_data, k_aux).astype(ka_ref.dtype)
            ka_ref[0, :, (2 * pair + 1) * LANES:(2 * pair + 2) * LANES] = jnp.where(
                first, k_aux, k_data).astype(ka_ref.dtype)

    hd = ATTN_HEAD_DIM
    q_t = _dot_nt(wt_ref[0:D_ATTN, :], hn) * Q_SCALE
    v_t = _dot_nt(wt_ref[D_ATTN:2 * D_ATTN, :], hn)
    ones = jnp.ones((V_ROWS - hd, tm), vt_ref.dtype)
    for h in range(ATTN_HEADS):
        even = h % 2 == 0
        data0 = h * LANES + (0 if even else hd)
        aux0 = h * LANES + (hd if even else 0)
        a0 = (hd if even else 0) + AUX_SLOTS * (h // 2)
        before, after = AUX_SLOTS * (h // 2), hd - AUX_SLOTS * (h // 2 + 1)
        pieces = ([jnp.zeros((before, tm), F32)] if before else []) + [q_aux_t[a0:a0 + AUX_SLOTS, :]]
        pieces += [jnp.zeros((after, tm), F32)] if after else []
        qt_ref[0, 0, data0:data0 + hd, :] = q_t[h * hd:(h + 1) * hd, :].astype(qt_ref.dtype)
        qt_ref[0, 0, aux0:aux0 + hd, :] = jnp.concatenate(pieces, axis=0).astype(qt_ref.dtype)
        vt_ref[0, 0, h * V_ROWS:h * V_ROWS + hd, :] = v_t[h * hd:(h + 1) * hd, :].astype(vt_ref.dtype)
        vt_ref[0, 0, h * V_ROWS + hd:(h + 1) * V_ROWS, :] = ones


def _inproj(x, g, w, wt, fb, pw, b, lp):
    tm = PROJ_ROWS
    row_spec = lambda width: pl.BlockSpec((1, tm, width), lambda bi, ti: (bi, ti, 0))
    col_spec = lambda height: pl.BlockSpec((1, 1, height, tm), lambda bi, ti: (bi, ti, 0, 0))
    plain = [w_ for _, w_ in _PLAIN_GROUPS]
    out_shape = ([jax.ShapeDtypeStruct((b, lp // tm, D_AUG, tm), BF16), jax.ShapeDtypeStruct((b, lp, D_AUG), BF16),
                  jax.ShapeDtypeStruct((b, lp // tm, ATTN_HEADS * V_ROWS, tm), BF16)]
                 + [jax.ShapeDtypeStruct((b, lp, w_), BF16) for w_ in plain]
                 + [jax.ShapeDtypeStruct((b, lp, FDT_COLS), F32)])
    return pl.pallas_call(
        _inproj_body,
        out_shape=out_shape,
        grid=(b, lp // tm),
        in_specs=[row_spec(D_MODEL),
                  _const_spec((1, D_MODEL)),
                  _const_spec((D_MODEL, N_PROJ)),
                  _const_spec((2 * D_ATTN, D_MODEL)),
                  _const_spec((1, LANES)),
                  _const_spec((LANES, 2 * LANES))],
        out_specs=([col_spec(D_AUG), row_spec(D_AUG), col_spec(ATTN_HEADS * V_ROWS)]
                   + [row_spec(w_) for w_ in plain] + [row_spec(FDT_COLS)]),
        scratch_shapes=[pltpu.VMEM((SUBLANES, LANES), F32)],
        compiler_params=_cparams(("parallel", "arbitrary")),
        name="inproj",
    )(x.reshape(b, lp, D_MODEL), g, w, wt, fb, pw)


def _aux_constants():
    src = jnp.arange(LANES)[:, None]
    lane = jnp.arange(LANES)[None, :]
    half = jnp.where(lane >= ATTN_HEAD_DIM, 0, 1)
    slot = lane % ATTN_HEAD_DIM
    head = 2 * (slot // AUX_SLOTS) + half
    idx = slot % AUX_SLOTS
    part = src // ATTN_HEADS
    part_src = (part < AUX_PARTS) & (src % ATTN_HEADS == head)
    one_src = src == ONE_LANE
    key = (one_src & (idx < AUX_PARTS)).astype(F32) - (part_src & (idx == part + AUX_PARTS)).astype(F32)
    qry = (part_src & (idx == part)).astype(F32) + (one_src & (idx >= AUX_PARTS) & (idx < 2 * AUX_PARTS)).astype(F32)
    return jnp.concatenate([key, qry], axis=1).astype(BF16)


def _attn_body(qt_ref, qn_ref, ka_hbm, vt_ref, mb_ref, o_ref, ka_ref, k_sem, m_scr, acc_scr, s_scr, *, tq, hps):
    bi = pl.program_id(0)
    hi = pl.program_id(1)
    iq = pl.program_id(2)

    def key_copy(blk, slot):
        rows = pl.ds(pl.multiple_of(blk * tq, tq), tq)
        return pltpu.make_async_copy(ka_hbm.at[bi, rows, pl.ds(pl.multiple_of(hi * hps * LANES, LANES), hps * LANES)],
                                     ka_ref.at[rows, :], k_sem.at[slot])

    @pl.when(iq == 0)
    def _():
        first_block = key_copy(0, 0)
        first_block.start()
        first_block.wait()

    @pl.when(iq > 0)
    def _():
        key_copy(iq, 1).wait()

    @pl.when(iq + 1 < pl.num_programs(2))
    def _():
        key_copy(iq + 1, 1).start()

    m_scr[...] = jnp.full(m_scr.shape, NEG_BIG, F32)
    acc_scr[...] = jnp.zeros(acc_scr.shape, F32)
    hd = ATTN_HEAD_DIM

    def scores(j, h):
        ks = pl.multiple_of(j * tq, tq)
        ka = ka_ref[pl.ds(ks, tq), h * LANES:(h + 1) * LANES]
        return _dot(ka, qt_ref[0, 0, h * LANES:(h + 1) * LANES, :])

    def softmax_pv(j, h, st):
        vt = vt_ref[0, j, h * V_ROWS:(h + 1) * V_ROWS, :]
        m_prev = m_scr[h]
        m_new = jnp.maximum(m_prev, jnp.max(st, axis=0, keepdims=True))
        alpha = jnp.exp2(m_prev - m_new)
        pt = jnp.exp2((st - m_new).astype(vt.dtype))
        acc_scr[h] = alpha * acc_scr[h] + _dot(vt, pt)
        m_scr[h] = m_new

    @pl.when(iq == 0)
    def _():
        for h in range(hps):
            s_scr[h] = scores(0, h)

    def loop_body(j, carry):
        for h in range(hps):
            s_next = scores(j + 1, h)
            softmax_pv(j, h, s_scr[h])
            s_scr[h] = s_next
        return carry

    lax.fori_loop(0, iq, loop_body, 0)
    for h in range(hps):
        s_next = _dot(ka_ref[0:tq, h * LANES:(h + 1) * LANES], qn_ref[0, 0, h * LANES:(h + 1) * LANES, :])
        softmax_pv(iq, h, s_scr[h] + mb_ref[...])
        s_scr[h] = s_next

    for hp in range(hps // 2):
        outs = []
        for e in range(2):
            acc = acc_scr[2 * hp + e]
            outs.append(acc[0:hd, :] / acc[hd:hd + 1, :])
        o_ref[0, :, hp * LANES:(hp + 1) * LANES] = jnp.concatenate(outs, axis=0).T.astype(o_ref.dtype)


def _attention(qt, ka, vt, b, lp):
    tq, hps = ATTN_TQ, ATTN_HPS
    key_i = jnp.arange(tq)[:, None]
    query_i = jnp.arange(tq)[None, :]
    causal_bias = jnp.where(key_i <= query_i, 0.0, NEG_BIG).astype(F32)
    last_tile = lp // tq - 1
    return pl.pallas_call(
        functools.partial(_attn_body, tq=tq, hps=hps),
        out_shape=jax.ShapeDtypeStruct((b, lp, D_ATTN), BF16),
        grid=(b, ATTN_HEADS // hps, lp // tq),
        in_specs=[pl.BlockSpec((1, 1, hps * LANES, tq), lambda bi, hi, qi: (bi, qi, hi, 0)),
                  pl.BlockSpec((1, 1, hps * LANES, tq), lambda bi, hi, qi: (bi, jnp.minimum(qi + 1, last_tile), hi, 0)),
                  pl.BlockSpec(memory_space=pl.ANY),
                  pl.BlockSpec((1, lp // tq, hps * V_ROWS, tq), lambda bi, hi, qi: (bi, 0, hi, 0)),
                  pl.BlockSpec((tq, tq), lambda bi, hi, qi: (0, 0))],
        out_specs=pl.BlockSpec((1, tq, hps * ATTN_HEAD_DIM), lambda bi, hi, qi: (bi, qi, hi)),
        scratch_shapes=[pltpu.VMEM((lp, hps * LANES), BF16),
                        pltpu.SemaphoreType.DMA((2,)),
                        pltpu.VMEM((hps, 1, tq), F32),
                        pltpu.VMEM((hps, V_ROWS, tq), F32),
                        pltpu.VMEM((hps, tq, tq), F32)],
        compiler_params=_cparams(("parallel", "parallel", "arbitrary")),
        name="fox_attention",
    )(qt, qt, ka, vt, causal_bias)


def _causal_conv(x, xp_scr, w_ref, b_ref, first_tile, rows, taps):
    @pl.when(first_tile)
    def _():
        xp_scr[0:SUBLANES, :] = jnp.zeros((SUBLANES, x.shape[1]), F32)

    xp_scr[SUBLANES:SUBLANES + rows, :] = x
    y = b_ref[...] + w_ref[taps - 1:taps, :] * x
    for kk in range(taps - 1):
        r0 = SUBLANES - (taps - 1) + kk
        y = y + w_ref[kk:kk + 1, :] * xp_scr[r0:r0 + rows, :]
    xp_scr[0:SUBLANES, :] = x[rows - SUBLANES:rows, :]
    return y


def _ssd_body(xbc_ref, z_ref, fdt_ref, cw_ref, cb_ref, dtb_ref, alog_ref, dfull_ref, nw_ref,
              exp_ref, o_ref, xp_scr, st_scr):
    rows = SSD_ROWS
    first_tile = pl.program_id(1) == 0

    @pl.when(first_tile)
    def _():
        st_scr[...] = jnp.zeros(st_scr.shape, F32)

    y = _causal_conv(xbc_ref[0].astype(F32), xp_scr, cw_ref, cb_ref, first_tile, rows, SSD_CONV)
    xc = y * _sigmoid(y)
    for ci in range(rows // SSD_CHUNK):
        rs = slice(ci * SSD_CHUNK, (ci + 1) * SSD_CHUNK)
        o_ref[0, rs, :] = _ssd_chunk(xc[rs], z_ref[0, rs, :].astype(F32), fdt_ref[0, rs, :], dtb_ref, alog_ref,
                                     dfull_ref, nw_ref, exp_ref, st_scr).astype(o_ref.dtype)


def _ssd_chunk(xc, z, dt_raw, dtb_ref, alog_ref, dfull_ref, nw_ref, exp_ref, st_scr):
    q = SSD_CHUNK
    gs = D_SSD // SSD_GROUPS
    heads_per_group = SSD_HEADS // SSD_GROUPS

    lane = lax.broadcasted_iota(jnp.int32, (q, LANES), 1)
    dt_lane = (lane >= DT_LANE0) & (lane < DT_LANE0 + SSD_HEADS)
    dt = jnp.where(dt_lane, _softplus(dt_raw + dtb_ref[...]), 0.0)
    a = -jnp.exp(alog_ref[...])
    da = dt * a
    row = lax.broadcasted_iota(jnp.int32, (q, q), 0)
    col = lax.broadcasted_iota(jnp.int32, (q, q), 1)
    lower = row >= col
    a_cum = _dot_01_lhs(lower.astype(BF16), da)
    a_cum_t = a_cum.T
    half = lax.broadcasted_iota(jnp.int32, (q, LANES), 1) < SSD_HEAD_DIM

    outs = []
    for g in range(SSD_GROUPS):
        sl = slice(g * gs, (g + 1) * gs)
        expand = exp_ref[:, sl]
        dt_full = _dot_01_rhs(dt, expand)
        a_cum_full = _dot_01_rhs(a_cum, expand)
        a_last_full = a_cum_full[q - 1:q, :]
        xs = xc[:, sl]
        xdt = xs * dt_full
        xdt_b = xdt.astype(BF16)
        xde_b = (xdt * jnp.exp(a_last_full - a_cum_full)).astype(BF16)

        bm = xc[:, D_SSD + g * SSD_STATE:D_SSD + (g + 1) * SSD_STATE]
        cm = xc[:, D_SSD + SSD_GROUPS * SSD_STATE + g * SSD_STATE:
                D_SSD + SSD_GROUPS * SSD_STATE + (g + 1) * SSD_STATE]
        bm_b = bm.astype(BF16)
        cm_b = cm.astype(BF16)
        cb = _dot_nt(cm_b, bm_b)
        y_pairs = []
        for pair in range(heads_per_group // 2):
            xp = xdt_b[:, pair * LANES:(pair + 1) * LANES]
            ys = []
            for e in range(2):
                hl = DT_LANE0 + g * heads_per_group + 2 * pair + e
                seg = a_cum[:, hl:hl + 1] - a_cum_t[hl:hl + 1, :]
                dec = jnp.exp(jnp.where(lower, seg, -jnp.inf))
                ys.append(_dot((cb * dec).astype(BF16), xp))
            y_pairs.append(jnp.where(half, ys[0], ys[1]))
        y_diag = jnp.concatenate(y_pairs, axis=1)
        prev = st_scr[g]
        y_off = _dot(cm_b, prev.astype(BF16)) * jnp.exp(a_cum_full)
        st_scr[g] = prev * jnp.exp(a_last_full) + _dot(bm.T.astype(BF16), xde_b)
        yg = y_diag + y_off + dfull_ref[:, sl] * xs
        zg = z[:, sl]
        yg = yg * (zg * _sigmoid(zg))
        yg = yg * lax.rsqrt(jnp.mean(yg * yg, axis=-1, keepdims=True) + NORM_EPS)
        outs.append(yg * nw_ref[:, sl])
    return jnp.concatenate(outs, axis=1)


def _lru_body(xr_ref, gate_ref, cw_ref, cb_ref, w2_ref, ba_ref, bx_ref, lam_ref, o_ref,
              xp_scr, h_scr):
    rows = SCAN_ROWS
    first_tile = pl.program_id(1) == 0

    @pl.when(first_tile)
    def _():
        h_scr[...] = jnp.zeros(h_scr.shape, F32)

    xc = _causal_conv(xr_ref[0].astype(F32), xp_scr, cw_ref, cb_ref, first_tile, rows, LRU_CONV)
    xc_b = xc.astype(BF16)
    pre = [_dot(xc_b[:, j * LANES:(j + 1) * LANES], w2_ref[j]) for j in range(D_LRU // LANES)]
    pre_a = jnp.concatenate([p[:, :LANES] for p in pre], axis=1)
    pre_x = jnp.concatenate([p[:, LANES:] for p in pre], axis=1)
    r = _sigmoid(pre_a + ba_ref[...])
    i = _sigmoid(pre_x + bx_ref[...])
    log_a = LRU_C * r * _log_sigmoid(lam_ref[...])
    a = jnp.exp(log_a)
    mult = jnp.sqrt(-jnp.tanh(log_a) * (a * a + 1.0))
    row0 = lax.broadcasted_iota(jnp.int32, (SUBLANES, D_LRU), 0) == 0
    mult = jnp.concatenate([jnp.where(first_tile & row0, 1.0, mult[:SUBLANES]), mult[SUBLANES:]], axis=0)
    u = mult * (i * xc)

    groups = rows // SUBLANES
    a3 = a.reshape(groups, SUBLANES, D_LRU)
    u3 = u.reshape(groups, SUBLANES, D_LRU)
    sub = lax.broadcasted_iota(jnp.int32, (groups, SUBLANES, D_LRU), 1)
    d = 1
    while d < SUBLANES:
        keep = sub >= d
        a_s = jnp.where(keep, pltpu.roll(a3, d, 1), 1.0)
        u_s = jnp.where(keep, pltpu.roll(u3, d, 1), 0.0)
        u3 = a3 * u_s + u3
        a3 = a3 * a_s
        d *= 2
    h_prev = h_scr[0:1, :]
    hs = []
    for r in range(groups):
        h_r = a3[r] * h_prev + u3[r]
        hs.append(h_r)
        h_prev = h_r[SUBLANES - 1:SUBLANES, :]
    h = jnp.concatenate(hs, axis=0)
    h_scr[0:1, :] = h[rows - 1:rows, :]
    o_ref[0] = (h * jax.nn.gelu(gate_ref[0].astype(F32))).astype(o_ref.dtype)


def _recurrent_body(xbc_ref, z_ref, fdt_ref, s_cw_ref, s_cb_ref, dtb_ref, alog_ref, dfull_ref, nw_ref, exp_ref,
                    xr_ref, gate_ref, l_cw_ref, l_cb_ref, w2_ref, ba_ref, bx_ref, lam_ref,
                    yb_ref, yc_ref, s_xp_scr, st_scr, l_xp_scr, h_scr):
    _lru_body(xr_ref, gate_ref, l_cw_ref, l_cb_ref, w2_ref, ba_ref, bx_ref, lam_ref, yc_ref, l_xp_scr, h_scr)
    _ssd_body(xbc_ref, z_ref, fdt_ref, s_cw_ref, s_cb_ref, dtb_ref, alog_ref, dfull_ref, nw_ref, exp_ref,
              yb_ref, s_xp_scr, st_scr)


def _recurrent(xbc, z, fdt, ssd_p, xr, gate, lru_p, b, lp):
    rows = SSD_ROWS
    tile = lambda width: pl.BlockSpec((1, rows, width), lambda bi, ti: (bi, ti, 0))
    params_s, params_l = tuple(ssd_p), tuple(lru_p)
    return pl.pallas_call(
        _recurrent_body,
        out_shape=[jax.ShapeDtypeStruct((b, lp, D_SSD), BF16), jax.ShapeDtypeStruct((b, lp, D_LRU), BF16)],
        grid=(b, lp // rows),
        in_specs=([tile(D_XBC), tile(D_SSD), tile(FDT_COLS)] + [_const_spec(p.shape) for p in params_s]
                  + [tile(D_LRU), tile(D_LRU)] + [_const_spec(p.shape) for p in params_l]),
        out_specs=[tile(D_SSD), tile(D_LRU)],
        scratch_shapes=[pltpu.VMEM((SUBLANES + rows, D_XBC), F32),
                        pltpu.VMEM((SSD_GROUPS, SSD_STATE, D_SSD // SSD_GROUPS), F32),
                        pltpu.VMEM((SUBLANES + rows, D_LRU), F32),
                        pltpu.VMEM((SUBLANES, D_LRU), F32)],
        compiler_params=_cparams(("parallel", "arbitrary")),
        name="recurrent",
    )(xbc, z, fdt, *params_s, xr, gate, *params_l)


def _merge_body(h_ref, ya_ref, yb_ref, yc_ref, m_ref, wa_ref, wb_ref, wc_ref, wo_ref, o_ref):
    gate = lambda i: _sigmoid(m_ref[:, i * D_MODEL:(i + 1) * D_MODEL].astype(F32))
    mixed = gate(0) * _dot(ya_ref[...], wa_ref[...])
    mixed = mixed + gate(1) * _dot(yb_ref[...], wb_ref[...])
    mixed = mixed + gate(2) * _dot(yc_ref[...], wc_ref[...])
    o_ref[...] = h_ref[...] + _dot(mixed.astype(BF16), wo_ref[...])


def _merge(h, ya, yb, yc, m, wa, wb, wc, wo):
    t = h.shape[0]
    tm = MERGE_ROWS
    row_spec = lambda width: pl.BlockSpec((tm, width), lambda i: (i, 0))
    return pl.pallas_call(
        _merge_body,
        out_shape=jax.ShapeDtypeStruct((t, D_MODEL), F32),
        grid=(t // tm,),
        in_specs=[row_spec(D_MODEL), row_spec(D_ATTN), row_spec(D_SSD), row_spec(D_LRU),
                  row_spec(N_BRANCH * D_MODEL),
                  _const_spec((D_ATTN, D_MODEL)), _const_spec((D_SSD, D_MODEL)),
                  _const_spec((D_LRU, D_MODEL)), _const_spec((D_MODEL, D_MODEL))],
        out_specs=row_spec(D_MODEL),
        compiler_params=_cparams(("parallel",)),
        name="merge_out",
    )(h, ya, yb, yc, m, wa, wb, wc, wo)


def _prep_w_in(w_in):
    sizes = (D_ATTN, D_ATTN, D_ATTN, ATTN_HEADS, D_SSD, D_XBC, SSD_HEADS, D_LRU, D_LRU, N_BRANCH * D_MODEL)
    offs = [0]
    for s in sizes:
        offs.append(offs[-1] + s)
    part = lambda i: w_in[:, offs[i]:offs[i + 1]]
    q, k, v, f, z, xbc, dt, xr, gate, merge = (part(i) for i in range(10))
    pad = jnp.zeros((D_MODEL, FDT_COLS - ATTN_HEADS - SSD_HEADS), w_in.dtype)
    w = jnp.concatenate([k, z, xbc, xr, gate, merge, f, dt, pad], axis=1).astype(BF16)
    wt = jnp.concatenate([q.T, v.T], axis=0).astype(BF16)
    return w, wt


def _pad_lanes(vec, lane0):
    out = jnp.zeros((1, LANES), F32)
    return out.at[0, lane0:lane0 + vec.shape[0]].set(vec.astype(F32))


def _lru_gate_weights(w_a, w_x):
    def blockdiag_pairs(w):
        w = w.reshape(LRU_BLOCKS // 2, 2, LRU_BLOCK_DIM, LRU_BLOCK_DIM)
        zero = jnp.zeros_like(w[:, 0])
        top = jnp.concatenate([w[:, 0], zero], axis=2)
        bot = jnp.concatenate([zero, w[:, 1]], axis=2)
        return jnp.concatenate([top, bot], axis=1)
    return jnp.concatenate([blockdiag_pairs(w_a), blockdiag_pairs(w_x)], axis=2).astype(BF16)


def _head_expand():
    rows = jnp.arange(LANES)[:, None]
    cols = jnp.arange(D_SSD)[None, :]
    return (rows == DT_LANE0 + cols // SSD_HEAD_DIM).astype(BF16)


def kernel(x, meta_tokens, ffn1_norm, ffn1_w_gate_up, ffn1_w_down, mix_norm, w_in, fox_forget_bias,
           ssd_conv_w, ssd_conv_b, ssd_dt_bias, ssd_a_log, ssd_d, ssd_norm,
           lru_conv_w, lru_conv_b, lru_w_a, lru_b_a, lru_w_x, lru_b_x, lru_lambda,
           w_branch_attn, w_branch_ssd, w_branch_lru, w_out,
           ffn2_norm, ffn2_w_gate_up, ffn2_w_down, final_norm):
    b, s, d = x.shape
    depth = w_in.shape[0]
    length = N_META + s
    lp = -(-length // SEQ_ALIGN) * SEQ_ALIGN
    t = b * lp
    assert PROJ_ROWS == ATTN_TQ
    assert d == D_MODEL and t % FFN_ROWS == 0 and t % MERGE_ROWS == 0 and s % FFN_ROWS == 0

    meta = jnp.broadcast_to(meta_tokens.astype(x.dtype)[None], (b, N_META, d))
    h = jnp.concatenate([meta, x, jnp.zeros((b, lp - length, d), x.dtype)], axis=1).reshape(t, d)

    row = lambda vec: vec.astype(F32).reshape(1, -1)
    expand = _head_expand()
    pw = _aux_constants()
    fg = row(final_norm)
    for l in range(depth):
        h = _ffn(h, row(ffn1_norm[l]), ffn1_w_gate_up[l].astype(BF16), ffn1_w_down[l].astype(BF16), fg, False)

        w_std, w_t = _prep_w_in(w_in[l])
        qt, ka, vt, z, xbc, xr, gate, merge, fdt = _inproj(
            h, row(mix_norm[l]), w_std, w_t, _pad_lanes(fox_forget_bias[l], 0), pw, b, lp)
        y_a = _attention(qt, ka, vt, b, lp).reshape(t, D_ATTN)
        ssd_p = (ssd_conv_w[l].astype(F32), row(ssd_conv_b[l]), _pad_lanes(ssd_dt_bias[l], DT_LANE0),
                 _pad_lanes(ssd_a_log[l], DT_LANE0), row(jnp.repeat(ssd_d[l], SSD_HEAD_DIM)), row(ssd_norm[l]), expand)
        lru_p = (lru_conv_w[l].astype(F32), row(lru_conv_b[l]), _lru_gate_weights(lru_w_a[l], lru_w_x[l]),
                 row(lru_b_a[l]), row(lru_b_x[l]), row(lru_lambda[l]))
        y_b, y_c = _recurrent(xbc, z, fdt, ssd_p, xr, gate, lru_p, b, lp)
        y_b = y_b.reshape(t, D_SSD)
        y_c = y_c.reshape(t, D_LRU)
        h = _merge(h, y_a, y_b, y_c, merge.reshape(t, N_BRANCH * D_MODEL), w_branch_attn[l].astype(BF16),
                   w_branch_ssd[l].astype(BF16), w_branch_lru[l].astype(BF16), w_out[l].astype(BF16))

        ffn2 = (row(ffn2_norm[l]), ffn2_w_gate_up[l].astype(BF16), ffn2_w_down[l].astype(BF16), fg)
        if l < depth - 1:
            h = _ffn(h, *ffn2, False)
    return _ffn_final(h, *ffn2, b, lp, s)
```

```python
import functools

import jax
import jax.numpy as jnp
from jax import lax
from jax.experimental import pallas as pl
from jax.experimental.pallas import tpu as pltpu

F32 = jnp.float32
BF16 = jnp.bfloat16

D_MODEL = 1024
N_META = 16
SSD_CHUNK = 128
NORM_EPS = 1e-6
ATTN_HEADS = 16
ATTN_HEAD_DIM = 64
D_ATTN = ATTN_HEADS * ATTN_HEAD_DIM
SSD_HEAD_DIM = 64
D_SSD = D_MODEL
SSD_HEADS = D_SSD // SSD_HEAD_DIM
SSD_GROUPS = 2
SSD_STATE = 128
SSD_CONV = 4
D_XBC = D_SSD + 2 * SSD_GROUPS * SSD_STATE
D_LRU = D_MODEL
LRU_BLOCKS = 16
LRU_BLOCK_DIM = D_LRU // LRU_BLOCKS
LRU_CONV = 4
LRU_C = 8.0
D_FF = 2816
N_BRANCH = 3

LANES = 128
SUBLANES = 8
VMEM_LIMIT_BYTES = 56 * 1024 * 1024

SEQ_ALIGN = 256
FFN_ROWS = 1024
FFN_CHUNK = 256
PROJ_ROWS = 256
PROJ_CHUNK = 512
ATTN_TQ = 256
ATTN_HPS = 16
MERGE_ROWS = 1024
SCAN_ROWS = 256
SSD_ROWS = 256
FDT_COLS = LANES
DT_LANE0 = ATTN_HEADS
NEG_BIG = -1e30
LOG2E = 1.4426950408889634
Q_SCALE = ATTN_HEAD_DIM ** -0.5 * LOG2E
AUX_PARTS = 3
AUX_SLOTS = 8


def _cparams(sem):
    return pltpu.CompilerParams(dimension_semantics=sem, vmem_limit_bytes=VMEM_LIMIT_BYTES)


def _const_spec(shape):
    nd = len(shape)
    return pl.BlockSpec(shape, lambda *_: (0,) * nd, pipeline_mode=pl.Buffered(1))


def _rms(x, g):
    ms = jnp.mean(x * x, axis=-1, keepdims=True)
    return (x * lax.rsqrt(ms + NORM_EPS)) * g


def _dot(a, b):
    return jnp.dot(a, b, preferred_element_type=F32)


def _dot_nt(a, b):
    return lax.dot_general(a, b, (((1,), (1,)), ((), ())), preferred_element_type=F32)


def _split3(x):
    hi = x.astype(BF16)
    r1 = x - hi.astype(F32)
    mid = r1.astype(BF16)
    lo = (r1 - mid.astype(F32)).astype(BF16)
    return hi, mid, lo


def _dot_01_lhs(sel, x):
    hi, mid, lo = _split3(x)
    return _dot(sel, hi) + _dot(sel, mid) + _dot(sel, lo)


def _dot_01_rhs(x, sel):
    hi, mid, lo = _split3(x)
    return _dot(hi, sel) + _dot(mid, sel) + _dot(lo, sel)


def _log_sigmoid(x):
    return -(jnp.maximum(-x, 0.0) + jnp.log1p(jnp.exp(-jnp.abs(x))))


def _softplus(x):
    return jnp.maximum(x, 0.0) + jnp.log1p(jnp.exp(-jnp.abs(x)))


def _sigmoid(x):
    return 1.0 / (1.0 + jnp.exp(-x))


def _ffn_body(x_ref, g_ref, wgu_ref, wd_ref, fg_ref, o_ref, a_scr, *, final_norm):
    x = x_ref[...]
    hn = _rms(x, g_ref[...]).astype(BF16)
    for c0 in range(0, D_FF, FFN_CHUNK):
        gate = _dot(hn, wgu_ref[:, c0:c0 + FFN_CHUNK])
        up = _dot(hn, wgu_ref[:, D_FF + c0:D_FF + c0 + FFN_CHUNK])
        a_scr[:, c0:c0 + FFN_CHUNK] = ((gate * _sigmoid(gate)) * up).astype(BF16)
    y = x + 0.5 * _dot(a_scr[...], wd_ref[...])
    if final_norm:
        y = _rms(y, fg_ref[...])
    o_ref[...] = y


def _ffn(x, g, wgu, wd, fg, final_norm):
    t = x.shape[0]
    tm = FFN_ROWS
    return pl.pallas_call(
        functools.partial(_ffn_body, final_norm=final_norm),
        out_shape=jax.ShapeDtypeStruct((t, D_MODEL), F32),
        grid=(t // tm,),
        in_specs=[pl.BlockSpec((tm, D_MODEL), lambda i: (i, 0))] + _ffn_weight_specs(),
        out_specs=pl.BlockSpec((tm, D_MODEL), lambda i: (i, 0)),
        scratch_shapes=[pltpu.VMEM((tm, D_FF), BF16)],
        compiler_params=_cparams(("parallel",)),
        name="ffn",
    )(x, g, wgu, wd, fg)


def _ffn_weight_specs():
    return [_const_spec((1, D_MODEL)), _const_spec((D_MODEL, 2 * D_FF)), _const_spec((D_FF, D_MODEL)),
            _const_spec((1, D_MODEL))]


def _ffn_final_body(x_ref, g_ref, wgu_ref, wd_ref, fg_ref, o_ref, a_scr):
    _ffn_body(x_ref.at[0], g_ref, wgu_ref, wd_ref, fg_ref, o_ref.at[0], a_scr, final_norm=True)


def _ffn_final(x, g, wgu, wd, fg, b, lp, s_out):
    tm = FFN_ROWS
    return pl.pallas_call(
        _ffn_final_body,
        out_shape=jax.ShapeDtypeStruct((b, s_out, D_MODEL), F32),
        grid=(b, s_out // tm),
        in_specs=[pl.BlockSpec((pl.Element(1), pl.Element(tm), pl.Element(D_MODEL)),
                               lambda bi, i: (bi, pl.multiple_of(N_META + i * tm, SUBLANES), 0))]
        + _ffn_weight_specs(),
        out_specs=pl.BlockSpec((1, tm, D_MODEL), lambda bi, i: (bi, i, 0)),
        scratch_shapes=[pltpu.VMEM((tm, D_FF), BF16)],
        compiler_params=_cparams(("parallel", "parallel")),
        name="ffn_final",
    )(x.reshape(b, lp, D_MODEL), g, wgu, wd, fg)


_PLAIN_GROUPS = (("z", D_SSD), ("xbc", D_XBC), ("xr", D_LRU), ("gate", D_LRU), ("merge", N_BRANCH * D_MODEL))
OFF_K = 0
OFF_PLAIN = D_ATTN
OFF_FDT = OFF_PLAIN + sum(w for _, w in _PLAIN_GROUPS)
N_PROJ = OFF_FDT + FDT_COLS
D_AUG = ATTN_HEADS * LANES
V_ROWS = ATTN_HEAD_DIM + 16
ONE_LANE = LANES - 1


def _inproj_body(x_ref, g_ref, w_ref, wt_ref, fb_ref, pw_ref, qt_ref, ka_ref, vt_ref,
                 z_ref, xbc_ref, xr_ref, gate_ref, merge_ref, fdt_ref, carry_scr):
    tm = x_ref.shape[1]

    @pl.when(pl.program_id(1) == 0)
    def _():
        carry_scr[...] = jnp.zeros(carry_scr.shape, F32)

    hn = _rms(x_ref[0], g_ref[...]).astype(BF16)

    def mm(c0, width):
        return _dot(hn, w_ref[:, c0:c0 + width])

    fdt = mm(OFF_FDT, FDT_COLS)
    fdt_ref[0] = fdt
    off = OFF_PLAIN
    for (_, width), o_ref in zip(_PLAIN_GROUPS, (z_ref, xbc_ref, xr_ref, gate_ref, merge_ref)):
        for c0 in range(0, width, PROJ_CHUNK):
            o_ref[0, :, c0:c0 + PROJ_CHUNK] = mm(off + c0, PROJ_CHUNK).astype(o_ref.dtype)
        off += width

    lane = lax.broadcasted_iota(jnp.int32, (tm, LANES), 1)
    lf = jnp.where(lane < ATTN_HEADS, _log_sigmoid(fdt + fb_ref[...]), 0.0)
    row = lax.broadcasted_iota(jnp.int32, (tm, tm), 0)
    col = lax.broadcasted_iota(jnp.int32, (tm, tm), 1)
    c = _dot_01_lhs((row >= col).astype(BF16), lf) + carry_scr[0:1, :]
    carry_scr[0:1, :] = c[tm - 1:tm, :]
    hi, mid, lo = (part.astype(F32) for part in _split3(c * LOG2E))
    cparts = (hi + pltpu.roll(mid, ATTN_HEADS, 1) + pltpu.roll(lo, 2 * ATTN_HEADS, 1)
              + jnp.where(lane == ONE_LANE, 1.0, 0.0))

    w_aux = _dot(cparts.astype(BF16), pw_ref[...])
    k_aux = w_aux[:, :LANES]
    q_aux_t = w_aux[:, LANES:].T

    first = lane < ATTN_HEAD_DIM
    for c0 in range(0, D_ATTN, PROJ_CHUNK):
        kv = mm(OFF_K + c0, PROJ_CHUNK)
        for pr in range(PROJ_CHUNK // LANES):
            pair = c0 // LANES + pr
            k_data = kv[:, pr * LANES:(pr + 1) * LANES]
            ka_ref[0, :, 2 * pair * LANES:(2 * pair + 1) * LANES] = jnp.where(
                first, k_data, k_aux).astype(ka_ref.dtype)
            ka_ref[0, :, (2 * pair + 1) * LANES:(2 * pair + 2) * LANES] = jnp.where(
                first, k_aux, k_data).astype(ka_ref.dtype)

    hd = ATTN_HEAD_DIM
    q_t = _dot_nt(wt_ref[0:D_ATTN, :], hn) * Q_SCALE
    v_t = _dot_nt(wt_ref[D_ATTN:2 * D_ATTN, :], hn)
    ones = jnp.ones((V_ROWS - hd, tm), vt_ref.dtype)
    for h in range(ATTN_HEADS):
        even = h % 2 == 0
        data0 = h * LANES + (0 if even else hd)
        aux0 = h * LANES + (hd if even else 0)
        a0 = (hd if even else 0) + AUX_SLOTS * (h // 2)
        before, after = AUX_SLOTS * (h // 2), hd - AUX_SLOTS * (h // 2 + 1)
        pieces = ([jnp.zeros((before, tm), F32)] if before else []) + [q_aux_t[a0:a0 + AUX_SLOTS, :]]
        pieces += [jnp.zeros((after, tm), F32)] if after else []
        qt_ref[0, 0, data0:data0 + hd, :] = q_t[h * hd:(h + 1) * hd, :].astype(qt_ref.dtype)
        qt_ref[0, 0, aux0:aux0 + hd, :] = jnp.concatenate(pieces, axis=0).astype(qt_ref.dtype)
        vt_ref[0, 0, h * V_ROWS:h * V_ROWS + hd, :] = v_t[h * hd:(h + 1) * hd, :].astype(vt_ref.dtype)
        vt_ref[0, 0, h * V_ROWS + hd:(h + 1) * V_ROWS, :] = ones


def _inproj(x, g, w, wt, fb, pw, b, lp):
    tm = PROJ_ROWS
    row_spec = lambda width: pl.BlockSpec((1, tm, width), lambda bi, ti: (bi, ti, 0))
    col_spec = lambda height: pl.BlockSpec((1, 1, height, tm), lambda bi, ti: (bi, ti, 0, 0))
    plain = [w_ for _, w_ in _PLAIN_GROUPS]
    out_shape = ([jax.ShapeDtypeStruct((b, lp // tm, D_AUG, tm), BF16), jax.ShapeDtypeStruct((b, lp, D_AUG), BF16),
                  jax.ShapeDtypeStruct((b, lp // tm, ATTN_HEADS * V_ROWS, tm), BF16)]
                 + [jax.ShapeDtypeStruct((b, lp, w_), BF16) for w_ in plain]
                 + [jax.ShapeDtypeStruct((b, lp, FDT_COLS), F32)])
    return pl.pallas_call(
        _inproj_body,
        out_shape=out_shape,
        grid=(b, lp // tm),
        in_specs=[row_spec(D_MODEL),
                  _const_spec((1, D_MODEL)),
                  _const_spec((D_MODEL, N_PROJ)),
                  _const_spec((2 * D_ATTN, D_MODEL)),
                  _const_spec((1, LANES)),
                  _const_spec((LANES, 2 * LANES))],
        out_specs=([col_spec(D_AUG), row_spec(D_AUG), col_spec(ATTN_HEADS * V_ROWS)]
                   + [row_spec(w_) for w_ in plain] + [row_spec(FDT_COLS)]),
        scratch_shapes=[pltpu.VMEM((SUBLANES, LANES), F32)],
        compiler_params=_cparams(("parallel", "arbitrary")),
        name="inproj",
    )(x.reshape(b, lp, D_MODEL), g, w, wt, fb, pw)


def _aux_constants():
    src = jnp.arange(LANES)[:, None]
    lane = jnp.arange(LANES)[None, :]
    half = jnp.where(lane >= ATTN_HEAD_DIM, 0, 1)
    slot = lane % ATTN_HEAD_DIM
    head = 2 * (slot // AUX_SLOTS) + half
    idx = slot % AUX_SLOTS
    part = src // ATTN_HEADS
    part_src = (part < AUX_PARTS) & (src % ATTN_HEADS == head)
    one_src = src == ONE_LANE
    key = (one_src & (idx < AUX_PARTS)).astype(F32) - (part_src & (idx == part + AUX_PARTS)).astype(F32)
    qry = (part_src & (idx == part)).astype(F32) + (one_src & (idx >= AUX_PARTS) & (idx < 2 * AUX_PARTS)).astype(F32)
    return jnp.concatenate([key, qry], axis=1).astype(BF16)


def _attn_body(qt_ref, qn_ref, ka_hbm, vt_ref, mb_ref, o_ref, ka_ref, k_sem, m_scr, acc_scr, s_scr, *, tq, hps):
    bi = pl.program_id(0)
    hi = pl.program_id(1)
    iq = pl.program_id(2)

    def key_copy(blk, slot):
        rows = pl.ds(pl.multiple_of(blk * tq, tq), tq)
        return pltpu.make_async_copy(ka_hbm.at[bi, rows, pl.ds(pl.multiple_of(hi * hps * LANES, LANES), hps * LANES)],
                                     ka_ref.at[rows, :], k_sem.at[slot])

    @pl.when(iq == 0)
    def _():
        first_block = key_copy(0, 0)
        first_block.start()
        first_block.wait()

    @pl.when(iq > 0)
    def _():
        key_copy(iq, 1).wait()

    @pl.when(iq + 1 < pl.num_programs(2))
    def _():
        key_copy(iq + 1, 1).start()

    m_scr[...] = jnp.full(m_scr.shape, NEG_BIG, F32)
    acc_scr[...] = jnp.zeros(acc_scr.shape, F32)
    hd = ATTN_HEAD_DIM

    def scores(j, h):
        ks = pl.multiple_of(j * tq, tq)
        ka = ka_ref[pl.ds(ks, tq), h * LANES:(h + 1) * LANES]
        return _dot(ka, qt_ref[0, 0, h * LANES:(h + 1) * LANES, :])

    def softmax_pv(j, h, st):
        vt = vt_ref[0, j, h * V_ROWS:(h + 1) * V_ROWS, :]
        m_prev = m_scr[h]
        m_new = jnp.maximum(m_prev, jnp.max(st, axis=0, keepdims=True))
        alpha = jnp.exp2(m_prev - m_new)
        pt = jnp.exp2((st - m_new).astype(vt.dtype))
        acc_scr[h] = alpha * acc_scr[h] + _dot(vt, pt)
        m_scr[h] = m_new

    @pl.when(iq == 0)
    def _():
        for h in range(hps):
            s_scr[h] = scores(0, h)

    def loop_body(j, carry):
        for h in range(hps):
            s_next = scores(j + 1, h)
            softmax_pv(j, h, s_scr[h])
            s_scr[h] = s_next
        return carry

    lax.fori_loop(0, iq, loop_body, 0)

    def causal(h):
        half = tq // 2
        tri = mb_ref[...]
        top = jnp.concatenate([s_scr[h, 0:half, 0:half] + tri, s_scr[h, 0:half, half:tq]], axis=1)
        bottom = jnp.concatenate([jnp.full((half, half), NEG_BIG, F32), s_scr[h, half:tq, half:tq] + tri], axis=1)
        return jnp.concatenate([top, bottom], axis=0)

    for h in range(hps):
        s_next = _dot(ka_ref[0:tq, h * LANES:(h + 1) * LANES], qn_ref[0, 0, h * LANES:(h + 1) * LANES, :])
        softmax_pv(iq, h, causal(h))
        s_scr[h] = s_next

    for hp in range(hps // 2):
        outs = []
        for e in range(2):
            acc = acc_scr[2 * hp + e]
            outs.append(acc[0:hd, :] / acc[hd:hd + 1, :])
        o_ref[0, :, hp * LANES:(hp + 1) * LANES] = jnp.concatenate(outs, axis=0).T.astype(o_ref.dtype)


def _attention(qt, ka, vt, b, lp):
    tq, hps = ATTN_TQ, ATTN_HPS
    key_i = jnp.arange(tq // 2)[:, None]
    query_i = jnp.arange(tq // 2)[None, :]
    causal_bias = jnp.where(key_i <= query_i, 0.0, NEG_BIG).astype(F32)
    last_tile = lp // tq - 1
    return pl.pallas_call(
        functools.partial(_attn_body, tq=tq, hps=hps),
        out_shape=jax.ShapeDtypeStruct((b, lp, D_ATTN), BF16),
        grid=(b, ATTN_HEADS // hps, lp // tq),
        in_specs=[pl.BlockSpec((1, 1, hps * LANES, tq), lambda bi, hi, qi: (bi, qi, hi, 0)),
                  pl.BlockSpec((1, 1, hps * LANES, tq), lambda bi, hi, qi: (bi, jnp.minimum(qi + 1, last_tile), hi, 0)),
                  pl.BlockSpec(memory_space=pl.ANY),
                  pl.BlockSpec((1, lp // tq, hps * V_ROWS, tq), lambda bi, hi, qi: (bi, 0, hi, 0)),
                  pl.BlockSpec((tq // 2, tq // 2), lambda bi, hi, qi: (0, 0))],
        out_specs=pl.BlockSpec((1, tq, hps * ATTN_HEAD_DIM), lambda bi, hi, qi: (bi, qi, hi)),
        scratch_shapes=[pltpu.VMEM((lp, hps * LANES), BF16),
                        pltpu.SemaphoreType.DMA((2,)),
                        pltpu.VMEM((hps, 1, tq), F32),
                        pltpu.VMEM((hps, V_ROWS, tq), F32),
                        pltpu.VMEM((hps, tq, tq), F32)],
        compiler_params=_cparams(("parallel", "parallel", "arbitrary")),
        name="fox_attention",
    )(qt, qt, ka, vt, causal_bias)


def _causal_conv(x, xp_scr, w_ref, b_ref, first_tile, rows, taps):
    @pl.when(first_tile)
    def _():
        xp_scr[0:SUBLANES, :] = jnp.zeros((SUBLANES, x.shape[1]), F32)

    xp_scr[SUBLANES:SUBLANES + rows, :] = x
    y = b_ref[...] + w_ref[taps - 1:taps, :] * x
    for kk in range(taps - 1):
        r0 = SUBLANES - (taps - 1) + kk
        y = y + w_ref[kk:kk + 1, :] * xp_scr[r0:r0 + rows, :]
    xp_scr[0:SUBLANES, :] = x[rows - SUBLANES:rows, :]
    return y


def _ssd_body(xbc_ref, z_ref, fdt_ref, cw_ref, cb_ref, dtb_ref, alog_ref, dfull_ref, nw_ref,
              exp_ref, o_ref, xp_scr, st_scr):
    rows = SSD_ROWS
    first_tile = pl.program_id(1) == 0

    @pl.when(first_tile)
    def _():
        st_scr[...] = jnp.zeros(st_scr.shape, F32)

    y = _causal_conv(xbc_ref[0].astype(F32), xp_scr, cw_ref, cb_ref, first_tile, rows, SSD_CONV)
    xc = y * _sigmoid(y)
    for ci in range(rows // SSD_CHUNK):
        rs = slice(ci * SSD_CHUNK, (ci + 1) * SSD_CHUNK)
        o_ref[0, rs, :] = _ssd_chunk(xc[rs], z_ref[0, rs, :].astype(F32), fdt_ref[0, rs, :], dtb_ref, alog_ref,
                                     dfull_ref, nw_ref, exp_ref, st_scr).astype(o_ref.dtype)


def _ssd_chunk(xc, z, dt_raw, dtb_ref, alog_ref, dfull_ref, nw_ref, exp_ref, st_scr):
    q = SSD_CHUNK
    gs = D_SSD // SSD_GROUPS
    heads_per_group = SSD_HEADS // SSD_GROUPS

    lane = lax.broadcasted_iota(jnp.int32, (q, LANES), 1)
    dt_lane = (lane >= DT_LANE0) & (lane < DT_LANE0 + SSD_HEADS)
    dt = jnp.where(dt_lane, _softplus(dt_raw + dtb_ref[...]), 0.0)
    a = -jnp.exp(alog_ref[...])
    da = dt * a
    row = lax.broadcasted_iota(jnp.int32, (q, q), 0)
    col = lax.broadcasted_iota(jnp.int32, (q, q), 1)
    lower = row >= col
    a_cum = _dot_01_lhs(lower.astype(BF16), da)
    a_cum_t = a_cum.T
    half = lax.broadcasted_iota(jnp.int32, (q, LANES), 1) < SSD_HEAD_DIM

    outs = []
    for g in range(SSD_GROUPS):
        sl = slice(g * gs, (g + 1) * gs)
        expand = exp_ref[:, sl]
        dt_full = _dot_01_rhs(dt, expand)
        a_cum_full = _dot_01_rhs(a_cum, expand)
        a_last_full = a_cum_full[q - 1:q, :]
        xs = xc[:, sl]
        xdt = xs * dt_full
        xdt_b = xdt.astype(BF16)
        xde_b = (xdt * jnp.exp(a_last_full - a_cum_full)).astype(BF16)

        bm = xc[:, D_SSD + g * SSD_STATE:D_SSD + (g + 1) * SSD_STATE]
        cm = xc[:, D_SSD + SSD_GROUPS * SSD_STATE + g * SSD_STATE:
                D_SSD + SSD_GROUPS * SSD_STATE + (g + 1) * SSD_STATE]
        bm_b = bm.astype(BF16)
        cm_b = cm.astype(BF16)
        cb = _dot_nt(cm_b, bm_b)
        y_pairs = []
        for pair in range(heads_per_group // 2):
            xp = xdt_b[:, pair * LANES:(pair + 1) * LANES]
            ys = []
            for e in range(2):
                hl = DT_LANE0 + g * heads_per_group + 2 * pair + e
                seg = a_cum[:, hl:hl + 1] - a_cum_t[hl:hl + 1, :]
                dec = jnp.exp(jnp.where(lower, seg, -jnp.inf))
                ys.append(_dot((cb * dec).astype(BF16), xp))
            y_pairs.append(jnp.where(half, ys[0], ys[1]))
        y_diag = jnp.concatenate(y_pairs, axis=1)
        prev = st_scr[g]
        y_off = _dot(cm_b, prev.astype(BF16)) * jnp.exp(a_cum_full)
        st_scr[g] = prev * jnp.exp(a_last_full) + _dot(bm.T.astype(BF16), xde_b)
        yg = y_diag + y_off + dfull_ref[:, sl] * xs
        zg = z[:, sl]
        yg = yg * (zg * _sigmoid(zg))
        yg = yg * lax.rsqrt(jnp.mean(yg * yg, axis=-1, keepdims=True) + NORM_EPS)
        outs.append(yg * nw_ref[:, sl])
    return jnp.concatenate(outs, axis=1)


def _lru_body(xr_ref, gate_ref, cw_ref, cb_ref, w2_ref, ba_ref, bx_ref, lam_ref, o_ref,
              xp_scr, h_scr):
    rows = SCAN_ROWS
    first_tile = pl.program_id(1) == 0

    @pl.when(first_tile)
    def _():
        h_scr[...] = jnp.zeros(h_scr.shape, F32)

    xc = _causal_conv(xr_ref[0].astype(F32), xp_scr, cw_ref, cb_ref, first_tile, rows, LRU_CONV)
    xc_b = xc.astype(BF16)
    pre = [_dot(xc_b[:, j * LANES:(j + 1) * LANES], w2_ref[j]) for j in range(D_LRU // LANES)]
    pre_a = jnp.concatenate([p[:, :LANES] for p in pre], axis=1)
    pre_x = jnp.concatenate([p[:, LANES:] for p in pre], axis=1)
    r = _sigmoid(pre_a + ba_ref[...])
    i = _sigmoid(pre_x + bx_ref[...])
    log_a = LRU_C * r * _log_sigmoid(lam_ref[...])
    a = jnp.exp(log_a)
    mult = jnp.sqrt(-jnp.tanh(log_a) * (a * a + 1.0))
    row0 = lax.broadcasted_iota(jnp.int32, (SUBLANES, D_LRU), 0) == 0
    mult = jnp.concatenate([jnp.where(first_tile & row0, 1.0, mult[:SUBLANES]), mult[SUBLANES:]], axis=0)
    u = mult * (i * xc)

    groups = rows // SUBLANES
    a3 = a.reshape(groups, SUBLANES, D_LRU)
    u3 = u.reshape(groups, SUBLANES, D_LRU)
    sub = lax.broadcasted_iota(jnp.int32, (groups, SUBLANES, D_LRU), 1)
    d = 1
    while d < SUBLANES:
        keep = sub >= d
        a_s = jnp.where(keep, pltpu.roll(a3, d, 1), 1.0)
        u_s = jnp.where(keep, pltpu.roll(u3, d, 1), 0.0)
        u3 = a3 * u_s + u3
        a3 = a3 * a_s
        d *= 2
    h_prev = h_scr[0:1, :]
    hs = []
    for r in range(groups):
        h_r = a3[r] * h_prev + u3[r]
        hs.append(h_r)
        h_prev = h_r[SUBLANES - 1:SUBLANES, :]
    h = jnp.concatenate(hs, axis=0)
    h_scr[0:1, :] = h[rows - 1:rows, :]
    o_ref[0] = (h * jax.nn.gelu(gate_ref[0].astype(F32))).astype(o_ref.dtype)


def _recurrent_body(xbc_ref, z_ref, fdt_ref, s_cw_ref, s_cb_ref, dtb_ref, alog_ref, dfull_ref, nw_ref, exp_ref,
                    xr_ref, gate_ref, l_cw_ref, l_cb_ref, w2_ref, ba_ref, bx_ref, lam_ref,
                    yb_ref, yc_ref, s_xp_scr, st_scr, l_xp_scr, h_scr):
    _lru_body(xr_ref, gate_ref, l_cw_ref, l_cb_ref, w2_ref, ba_ref, bx_ref, lam_ref, yc_ref, l_xp_scr, h_scr)
    _ssd_body(xbc_ref, z_ref, fdt_ref, s_cw_ref, s_cb_ref, dtb_ref, alog_ref, dfull_ref, nw_ref, exp_ref,
              yb_ref, s_xp_scr, st_scr)


def _recurrent(xbc, z, fdt, ssd_p, xr, gate, lru_p, b, lp):
    rows = SSD_ROWS
    tile = lambda width: pl.BlockSpec((1, rows, width), lambda bi, ti: (bi, ti, 0))
    params_s, params_l = tuple(ssd_p), tuple(lru_p)
    return pl.pallas_call(
        _recurrent_body,
        out_shape=[jax.ShapeDtypeStruct((b, lp, D_SSD), BF16), jax.ShapeDtypeStruct((b, lp, D_LRU), BF16)],
        grid=(b, lp // rows),
        in_specs=([tile(D_XBC), tile(D_SSD), tile(FDT_COLS)] + [_const_spec(p.shape) for p in params_s]
                  + [tile(D_LRU), tile(D_LRU)] + [_const_spec(p.shape) for p in params_l]),
        out_specs=[tile(D_SSD), tile(D_LRU)],
        scratch_shapes=[pltpu.VMEM((SUBLANES + rows, D_XBC), F32),
                        pltpu.VMEM((SSD_GROUPS, SSD_STATE, D_SSD // SSD_GROUPS), F32),
                        pltpu.VMEM((SUBLANES + rows, D_LRU), F32),
                        pltpu.VMEM((SUBLANES, D_LRU), F32)],
        compiler_params=_cparams(("parallel", "arbitrary")),
        name="recurrent",
    )(xbc, z, fdt, *params_s, xr, gate, *params_l)


def _merge_body(h_ref, ya_ref, yb_ref, yc_ref, m_ref, wa_ref, wb_ref, wc_ref, wo_ref, o_ref):
    gate = lambda i: _sigmoid(m_ref[:, i * D_MODEL:(i + 1) * D_MODEL].astype(F32))
    mixed = gate(0) * _dot(ya_ref[...], wa_ref[...])
    mixed = mixed + gate(1) * _dot(yb_ref[...], wb_ref[...])
    mixed = mixed + gate(2) * _dot(yc_ref[...], wc_ref[...])
    o_ref[...] = h_ref[...] + _dot(mixed.astype(BF16), wo_ref[...])


def _merge(h, ya, yb, yc, m, wa, wb, wc, wo):
    t = h.shape[0]
    tm = MERGE_ROWS
    row_spec = lambda width: pl.BlockSpec((tm, width), lambda i: (i, 0))
    return pl.pallas_call(
        _merge_body,
        out_shape=jax.ShapeDtypeStruct((t, D_MODEL), F32),
        grid=(t // tm,),
        in_specs=[row_spec(D_MODEL), row_spec(D_ATTN), row_spec(D_SSD), row_spec(D_LRU),
                  row_spec(N_BRANCH * D_MODEL),
                  _const_spec((D_ATTN, D_MODEL)), _const_spec((D_SSD, D_MODEL)),
                  _const_spec((D_LRU, D_MODEL)), _const_spec((D_MODEL, D_MODEL))],
        out_specs=row_spec(D_MODEL),
        compiler_params=_cparams(("parallel",)),
        name="merge_out",
    )(h, ya, yb, yc, m, wa, wb, wc, wo)


def _prep_w_in(w_in):
    sizes = (D_ATTN, D_ATTN, D_ATTN, ATTN_HEADS, D_SSD, D_XBC, SSD_HEADS, D_LRU, D_LRU, N_BRANCH * D_MODEL)
    offs = [0]
    for s in sizes:
        offs.append(offs[-1] + s)
    part = lambda i: w_in[:, offs[i]:offs[i + 1]]
    q, k, v, f, z, xbc, dt, xr, gate, merge = (part(i) for i in range(10))
    pad = jnp.zeros((D_MODEL, FDT_COLS - ATTN_HEADS - SSD_HEADS), w_in.dtype)
    w = jnp.concatenate([k, z, xbc, xr, gate, merge, f, dt, pad], axis=1).astype(BF16)
    wt = jnp.concatenate([q.T, v.T], axis=0).astype(BF16)
    return w, wt


def _pad_lanes(vec, lane0):
    out = jnp.zeros((1, LANES), F32)
    return out.at[0, lane0:lane0 + vec.shape[0]].set(vec.astype(F32))


def _lru_gate_weights(w_a, w_x):
    def blockdiag_pairs(w):
        w = w.reshape(LRU_BLOCKS // 2, 2, LRU_BLOCK_DIM, LRU_BLOCK_DIM)
        zero = jnp.zeros_like(w[:, 0])
        top = jnp.concatenate([w[:, 0], zero], axis=2)
        bot = jnp.concatenate([zero, w[:, 1]], axis=2)
        return jnp.concatenate([top, bot], axis=1)
    return jnp.concatenate([blockdiag_pairs(w_a), blockdiag_pairs(w_x)], axis=2).astype(BF16)


def _head_expand():
    rows = jnp.arange(LANES)[:, None]
    cols = jnp.arange(D_SSD)[None, :]
    return (rows == DT_LANE0 + cols // SSD_HEAD_DIM).astype(BF16)


def kernel(x, meta_tokens, ffn1_norm, ffn1_w_gate_up, ffn1_w_down, mix_norm, w_in, fox_forget_bias,
           ssd_conv_w, ssd_conv_b, ssd_dt_bias, ssd_a_log, ssd_d, ssd_norm,
           lru_conv_w, lru_conv_b, lru_w_a, lru_b_a, lru_w_x, lru_b_x, lru_lambda,
           w_branch_attn, w_branch_ssd, w_branch_lru, w_out,
           ffn2_norm, ffn2_w_gate_up, ffn2_w_down, final_norm):
    b, s, d = x.shape
    depth = w_in.shape[0]
    length = N_META + s
    lp = -(-length // SEQ_ALIGN) * SEQ_ALIGN
    t = b * lp
    assert PROJ_ROWS == ATTN_TQ
    assert d == D_MODEL and t % FFN_ROWS == 0 and t % MERGE_ROWS == 0 and s % FFN_ROWS == 0

    meta = jnp.broadcast_to(meta_tokens.astype(x.dtype)[None], (b, N_META, d))
    h = jnp.concatenate([meta, x, jnp.zeros((b, lp - length, d), x.dtype)], axis=1).reshape(t, d)

    row = lambda vec: vec.astype(F32).reshape(1, -1)
    expand = _head_expand()
    pw = _aux_constants()
    fg = row(final_norm)
    for l in range(depth):
        h = _ffn(h, row(ffn1_norm[l]), ffn1_w_gate_up[l].astype(BF16), ffn1_w_down[l].astype(BF16), fg, False)

        w_std, w_t = _prep_w_in(w_in[l])
        qt, ka, vt, z, xbc, xr, gate, merge, fdt = _inproj(
            h, row(mix_norm[l]), w_std, w_t, _pad_lanes(fox_forget_bias[l], 0), pw, b, lp)
        y_a = _attention(qt, ka, vt, b, lp).reshape(t, D_ATTN)
        ssd_p = (ssd_conv_w[l].astype(F32), row(ssd_conv_b[l]), _pad_lanes(ssd_dt_bias[l], DT_LANE0),
                 _pad_lanes(ssd_a_log[l], DT_LANE0), row(jnp.repeat(ssd_d[l], SSD_HEAD_DIM)), row(ssd_norm[l]), expand)
        lru_p = (lru_conv_w[l].astype(F32), row(lru_conv_b[l]), _lru_gate_weights(lru_w_a[l], lru_w_x[l]),
                 row(lru_b_a[l]), row(lru_b_x[l]), row(lru_lambda[l]))
        y_b, y_c = _recurrent(xbc, z, fdt, ssd_p, xr, gate, lru_p, b, lp)
        y_b = y_b.reshape(t, D_SSD)
        y_c = y_c.reshape(t, D_LRU)
        h = _merge(h, y_a, y_b, y_c, merge.reshape(t, N_BRANCH * D_MODEL), w_branch_attn[l].astype(BF16),
                   w_branch_ssd[l].astype(BF16), w_branch_lru[l].astype(BF16), w_out[l].astype(BF16))

        ffn2 = (row(ffn2_norm[l]), ffn2_w_gate_up[l].astype(BF16), ffn2_w_down[l].astype(BF16), fg)
        if l < depth - 1:
            h = _ffn(h, *ffn2, False)
    return _ffn_final(h, *ffn2, b, lp, s)
```

```python
import functools

import jax
import jax.numpy as jnp
from jax import lax
from jax.experimental import pallas as pl
from jax.experimental.pallas import tpu as pltpu

F32 = jnp.float32
BF16 = jnp.bfloat16

D_MODEL = 1024
N_META = 16
SSD_CHUNK = 128
NORM_EPS = 1e-6
ATTN_HEADS = 16
ATTN_HEAD_DIM = 64
D_ATTN = ATTN_HEADS * ATTN_HEAD_DIM
SSD_HEAD_DIM = 64
D_SSD = D_MODEL
SSD_HEADS = D_SSD // SSD_HEAD_DIM
SSD_GROUPS = 2
SSD_STATE = 128
SSD_CONV = 4
D_XBC = D_SSD + 2 * SSD_GROUPS * SSD_STATE
D_LRU = D_MODEL
LRU_BLOCKS = 16
LRU_BLOCK_DIM = D_LRU // LRU_BLOCKS
LRU_CONV = 4
LRU_C = 8.0
D_FF = 2816
N_BRANCH = 3

LANES = 128
SUBLANES = 8
VMEM_LIMIT_BYTES = 56 * 1024 * 1024

SEQ_ALIGN = 256
FFN_ROWS = 1024
FFN_CHUNK = 256
PROJ_ROWS = 256
PROJ_CHUNK = 512
ATTN_TQ = 256
ATTN_HPS = 16
MERGE_ROWS = 1024
FUSED_ROWS = 512
SCAN_ROWS = 256
SSD_ROWS = 256
FDT_COLS = LANES
DT_LANE0 = ATTN_HEADS
NEG_BIG = -1e30
LOG2E = 1.4426950408889634
Q_SCALE = ATTN_HEAD_DIM ** -0.5 * LOG2E
AUX_PARTS = 3
AUX_SLOTS = 8


def _cparams(sem):
    return pltpu.CompilerParams(dimension_semantics=sem, vmem_limit_bytes=VMEM_LIMIT_BYTES)


def _const_spec(shape):
    nd = len(shape)
    return pl.BlockSpec(shape, lambda *_: (0,) * nd, pipeline_mode=pl.Buffered(1))


def _rms(x, g):
    ms = jnp.mean(x * x, axis=-1, keepdims=True)
    return (x * lax.rsqrt(ms + NORM_EPS)) * g


def _dot(a, b):
    return jnp.dot(a, b, preferred_element_type=F32)


def _dot_nt(a, b):
    return lax.dot_general(a, b, (((1,), (1,)), ((), ())), preferred_element_type=F32)


def _split3(x):
    hi = x.astype(BF16)
    r1 = x - hi.astype(F32)
    mid = r1.astype(BF16)
    lo = (r1 - mid.astype(F32)).astype(BF16)
    return hi, mid, lo


def _dot_01_lhs(sel, x):
    hi, mid, lo = _split3(x)
    return _dot(sel, hi) + _dot(sel, mid) + _dot(sel, lo)


def _dot_01_rhs(x, sel):
    hi, mid, lo = _split3(x)
    return _dot(hi, sel) + _dot(mid, sel) + _dot(lo, sel)


def _log_sigmoid(x):
    return -(jnp.maximum(-x, 0.0) + jnp.log1p(jnp.exp(-jnp.abs(x))))


def _softplus(x):
    return jnp.maximum(x, 0.0) + jnp.log1p(jnp.exp(-jnp.abs(x)))


def _sigmoid(x):
    return 1.0 / (1.0 + jnp.exp(-x))


def _ffn_apply(x, g_ref, wgu_ref, wd_ref, a_scr):
    hn = _rms(x, g_ref[...]).astype(BF16)
    for c0 in range(0, D_FF, FFN_CHUNK):
        gate = _dot(hn, wgu_ref[:, c0:c0 + FFN_CHUNK])
        up = _dot(hn, wgu_ref[:, D_FF + c0:D_FF + c0 + FFN_CHUNK])
        a_scr[:, c0:c0 + FFN_CHUNK] = ((gate * _sigmoid(gate)) * up).astype(BF16)
    return x + 0.5 * _dot(a_scr[...], wd_ref[...])


def _ffn_body(x_ref, g_ref, wgu_ref, wd_ref, fg_ref, o_ref, a_scr, *, final_norm):
    y = _ffn_apply(x_ref[...], g_ref, wgu_ref, wd_ref, a_scr)
    if final_norm:
        y = _rms(y, fg_ref[...])
    o_ref[...] = y


def _ffn(x, g, wgu, wd, fg, final_norm):
    t = x.shape[0]
    tm = FFN_ROWS
    return pl.pallas_call(
        functools.partial(_ffn_body, final_norm=final_norm),
        out_shape=jax.ShapeDtypeStruct((t, D_MODEL), F32),
        grid=(t // tm,),
        in_specs=[pl.BlockSpec((tm, D_MODEL), lambda i: (i, 0))] + _ffn_weight_specs(),
        out_specs=pl.BlockSpec((tm, D_MODEL), lambda i: (i, 0)),
        scratch_shapes=[pltpu.VMEM((tm, D_FF), BF16)],
        compiler_params=_cparams(("parallel",)),
        name="ffn",
    )(x, g, wgu, wd, fg)


def _ffn_weight_specs():
    return [_const_spec((1, D_MODEL)), _const_spec((D_MODEL, 2 * D_FF)), _const_spec((D_FF, D_MODEL)),
            _const_spec((1, D_MODEL))]


def _ffn_final_body(x_ref, g_ref, wgu_ref, wd_ref, fg_ref, o_ref, a_scr):
    _ffn_body(x_ref.at[0], g_ref, wgu_ref, wd_ref, fg_ref, o_ref.at[0], a_scr, final_norm=True)


def _ffn_final(x, g, wgu, wd, fg, b, lp, s_out):
    tm = FFN_ROWS
    return pl.pallas_call(
        _ffn_final_body,
        out_shape=jax.ShapeDtypeStruct((b, s_out, D_MODEL), F32),
        grid=(b, s_out // tm),
        in_specs=[pl.BlockSpec((pl.Element(1), pl.Element(tm), pl.Element(D_MODEL)),
                               lambda bi, i: (bi, pl.multiple_of(N_META + i * tm, SUBLANES), 0))]
        + _ffn_weight_specs(),
        out_specs=pl.BlockSpec((1, tm, D_MODEL), lambda bi, i: (bi, i, 0)),
        scratch_shapes=[pltpu.VMEM((tm, D_FF), BF16)],
        compiler_params=_cparams(("parallel", "parallel")),
        name="ffn_final",
    )(x.reshape(b, lp, D_MODEL), g, wgu, wd, fg)


_PLAIN_GROUPS = (("z", D_SSD), ("xbc", D_XBC), ("xr", D_LRU), ("gate", D_LRU), ("merge", N_BRANCH * D_MODEL))
OFF_K = 0
OFF_PLAIN = D_ATTN
OFF_FDT = OFF_PLAIN + sum(w for _, w in _PLAIN_GROUPS)
N_PROJ = OFF_FDT + FDT_COLS
D_AUG = ATTN_HEADS * LANES
V_ROWS = ATTN_HEAD_DIM + 16
ONE_LANE = LANES - 1


def _inproj_body(x_ref, g_ref, w_ref, wt_ref, fb_ref, pw_ref, qt_ref, ka_ref, vt_ref,
                 z_ref, xbc_ref, xr_ref, gate_ref, merge_ref, fdt_ref, carry_scr):
    tm = x_ref.shape[1]

    @pl.when(pl.program_id(1) == 0)
    def _():
        carry_scr[...] = jnp.zeros(carry_scr.shape, F32)

    hn = _rms(x_ref[0], g_ref[...]).astype(BF16)

    def mm(c0, width):
        return _dot(hn, w_ref[:, c0:c0 + width])

    fdt = mm(OFF_FDT, FDT_COLS)
    fdt_ref[0] = fdt
    off = OFF_PLAIN
    for (_, width), o_ref in zip(_PLAIN_GROUPS, (z_ref, xbc_ref, xr_ref, gate_ref, merge_ref)):
        for c0 in range(0, width, PROJ_CHUNK):
            o_ref[0, :, c0:c0 + PROJ_CHUNK] = mm(off + c0, PROJ_CHUNK).astype(o_ref.dtype)
        off += width

    lane = lax.broadcasted_iota(jnp.int32, (tm, LANES), 1)
    lf = jnp.where(lane < ATTN_HEADS, _log_sigmoid(fdt + fb_ref[...]), 0.0)
    row = lax.broadcasted_iota(jnp.int32, (tm, tm), 0)
    col = lax.broadcasted_iota(jnp.int32, (tm, tm), 1)
    c = _dot_01_lhs((row >= col).astype(BF16), lf) + carry_scr[0:1, :]
    carry_scr[0:1, :] = c[tm - 1:tm, :]
    hi, mid, lo = (part.astype(F32) for part in _split3(c * LOG2E))
    cparts = (hi + pltpu.roll(mid, ATTN_HEADS, 1) + pltpu.roll(lo, 2 * ATTN_HEADS, 1)
              + jnp.where(lane == ONE_LANE, 1.0, 0.0))

    w_aux = _dot(cparts.astype(BF16), pw_ref[...])
    k_aux = w_aux[:, :LANES]
    q_aux_t = w_aux[:, LANES:].T

    first = lane < ATTN_HEAD_DIM
    for c0 in range(0, D_ATTN, PROJ_CHUNK):
        kv = mm(OFF_K + c0, PROJ_CHUNK)
        for pr in range(PROJ_CHUNK // LANES):
            pair = c0 // LANES + pr
            k_data = kv[:, pr * LANES:(pr + 1) * LANES]
            ka_ref[0, :, 2 * pair * LANES:(2 * pair + 1) * LANES] = jnp.where(
                first, k_data, k_aux).astype(ka_ref.dtype)
            ka_ref[0, :, (2 * pair + 1) * LANES:(2 * pair + 2) * LANES] = jnp.where(
                first, k_aux, k_data).astype(ka_ref.dtype)

    hd = ATTN_HEAD_DIM
    q_t = _dot_nt(wt_ref[0:D_ATTN, :], hn) * Q_SCALE
    v_t = _dot_nt(wt_ref[D_ATTN:2 * D_ATTN, :], hn)
    ones = jnp.ones((V_ROWS - hd, tm), vt_ref.dtype)
    for h in range(ATTN_HEADS):
        even = h % 2 == 0
        data0 = h * LANES + (0 if even else hd)
        aux0 = h * LANES + (hd if even else 0)
        a0 = (hd if even else 0) + AUX_SLOTS * (h // 2)
        before, after = AUX_SLOTS * (h // 2), hd - AUX_SLOTS * (h // 2 + 1)
        pieces = ([jnp.zeros((before, tm), F32)] if before else []) + [q_aux_t[a0:a0 + AUX_SLOTS, :]]
        pieces += [jnp.zeros((after, tm), F32)] if after else []
        qt_ref[0, 0, data0:data0 + hd, :] = q_t[h * hd:(h + 1) * hd, :].astype(qt_ref.dtype)
        qt_ref[0, 0, aux0:aux0 + hd, :] = jnp.concatenate(pieces, axis=0).astype(qt_ref.dtype)
        vt_ref[0, 0, h * V_ROWS:h * V_ROWS + hd, :] = v_t[h * hd:(h + 1) * hd, :].astype(vt_ref.dtype)
        vt_ref[0, 0, h * V_ROWS + hd:(h + 1) * V_ROWS, :] = ones


def _inproj(x, g, w, wt, fb, pw, b, lp):
    tm = PROJ_ROWS
    row_spec = lambda width: pl.BlockSpec((1, tm, width), lambda bi, ti: (bi, ti, 0))
    col_spec = lambda height: pl.BlockSpec((1, 1, height, tm), lambda bi, ti: (bi, ti, 0, 0))
    plain = [w_ for _, w_ in _PLAIN_GROUPS]
    out_shape = ([jax.ShapeDtypeStruct((b, lp // tm, D_AUG, tm), BF16), jax.ShapeDtypeStruct((b, lp, D_AUG), BF16),
                  jax.ShapeDtypeStruct((b, lp // tm, ATTN_HEADS * V_ROWS, tm), BF16)]
                 + [jax.ShapeDtypeStruct((b, lp, w_), BF16) for w_ in plain]
                 + [jax.ShapeDtypeStruct((b, lp, FDT_COLS), F32)])
    return pl.pallas_call(
        _inproj_body,
        out_shape=out_shape,
        grid=(b, lp // tm),
        in_specs=[row_spec(D_MODEL),
                  _const_spec((1, D_MODEL)),
                  _const_spec((D_MODEL, N_PROJ)),
                  _const_spec((2 * D_ATTN, D_MODEL)),
                  _const_spec((1, LANES)),
                  _const_spec((LANES, 2 * LANES))],
        out_specs=([col_spec(D_AUG), row_spec(D_AUG), col_spec(ATTN_HEADS * V_ROWS)]
                   + [row_spec(w_) for w_ in plain] + [row_spec(FDT_COLS)]),
        scratch_shapes=[pltpu.VMEM((SUBLANES, LANES), F32)],
        compiler_params=_cparams(("parallel", "arbitrary")),
        name="inproj",
    )(x.reshape(b, lp, D_MODEL), g, w, wt, fb, pw)


def _aux_constants():
    src = jnp.arange(LANES)[:, None]
    lane = jnp.arange(LANES)[None, :]
    half = jnp.where(lane >= ATTN_HEAD_DIM, 0, 1)
    slot = lane % ATTN_HEAD_DIM
    head = 2 * (slot // AUX_SLOTS) + half
    idx = slot % AUX_SLOTS
    part = src // ATTN_HEADS
    part_src = (part < AUX_PARTS) & (src % ATTN_HEADS == head)
    one_src = src == ONE_LANE
    key = (one_src & (idx < AUX_PARTS)).astype(F32) - (part_src & (idx == part + AUX_PARTS)).astype(F32)
    qry = (part_src & (idx == part)).astype(F32) + (one_src & (idx >= AUX_PARTS) & (idx < 2 * AUX_PARTS)).astype(F32)
    return jnp.concatenate([key, qry], axis=1).astype(BF16)


def _attn_body(qt_ref, qn_ref, ka_hbm, vt_ref, mb_ref, o_ref, ka_ref, k_sem, m_scr, acc_scr, s_scr, *, tq, hps):
    bi = pl.program_id(0)
    hi = pl.program_id(1)
    iq = pl.program_id(2)

    def key_copy(blk, slot):
        rows = pl.ds(pl.multiple_of(blk * tq, tq), tq)
        return pltpu.make_async_copy(ka_hbm.at[bi, rows, pl.ds(pl.multiple_of(hi * hps * LANES, LANES), hps * LANES)],
                                     ka_ref.at[rows, :], k_sem.at[slot])

    @pl.when(iq == 0)
    def _():
        first_block = key_copy(0, 0)
        first_block.start()
        first_block.wait()

    @pl.when(iq > 0)
    def _():
        key_copy(iq, 1).wait()

    @pl.when(iq + 1 < pl.num_programs(2))
    def _():
        key_copy(iq + 1, 1).start()

    m_scr[...] = jnp.full(m_scr.shape, NEG_BIG, F32)
    acc_scr[...] = jnp.zeros(acc_scr.shape, F32)
    hd = ATTN_HEAD_DIM

    def scores(j, h):
        ks = pl.multiple_of(j * tq, tq)
        ka = ka_ref[pl.ds(ks, tq), h * LANES:(h + 1) * LANES]
        return _dot(ka, qt_ref[0, 0, h * LANES:(h + 1) * LANES, :])

    def softmax_pv(j, h, st):
        vt = vt_ref[0, j, h * V_ROWS:(h + 1) * V_ROWS, :]
        m_prev = m_scr[h]
        m_new = jnp.maximum(m_prev, jnp.max(st, axis=0, keepdims=True))
        alpha = jnp.exp2(m_prev - m_new)
        pt = jnp.exp2((st - m_new).astype(vt.dtype))
        acc_scr[h] = alpha * acc_scr[h] + _dot(vt, pt)
        m_scr[h] = m_new

    @pl.when(iq == 0)
    def _():
        for h in range(hps):
            s_scr[h] = scores(0, h)

    def loop_body(j, carry):
        for h in range(hps):
            s_next = scores(j + 1, h)
            softmax_pv(j, h, s_scr[h])
            s_scr[h] = s_next
        return carry

    lax.fori_loop(0, iq, loop_body, 0)

    def causal(h):
        half = tq // 2
        tri = mb_ref[...]
        top = jnp.concatenate([s_scr[h, 0:half, 0:half] + tri, s_scr[h, 0:half, half:tq]], axis=1)
        bottom = jnp.concatenate([jnp.full((half, half), NEG_BIG, F32), s_scr[h, half:tq, half:tq] + tri], axis=1)
        return jnp.concatenate([top, bottom], axis=0)

    for h in range(hps):
        s_next = _dot(ka_ref[0:tq, h * LANES:(h + 1) * LANES], qn_ref[0, 0, h * LANES:(h + 1) * LANES, :])
        softmax_pv(iq, h, causal(h))
        s_scr[h] = s_next

    for hp in range(hps // 2):
        outs = []
        for e in range(2):
            acc = acc_scr[2 * hp + e]
            outs.append(acc[0:hd, :] / acc[hd:hd + 1, :])
        o_ref[0, :, hp * LANES:(hp + 1) * LANES] = jnp.concatenate(outs, axis=0).T.astype(o_ref.dtype)


def _attention(qt, ka, vt, b, lp):
    tq, hps = ATTN_TQ, ATTN_HPS
    key_i = jnp.arange(tq // 2)[:, None]
    query_i = jnp.arange(tq // 2)[None, :]
    causal_bias = jnp.where(key_i <= query_i, 0.0, NEG_BIG).astype(F32)
    last_tile = lp // tq - 1
    return pl.pallas_call(
        functools.partial(_attn_body, tq=tq, hps=hps),
        out_shape=jax.ShapeDtypeStruct((b, lp, D_ATTN), BF16),
        grid=(b, ATTN_HEADS // hps, lp // tq),
        in_specs=[pl.BlockSpec((1, 1, hps * LANES, tq), lambda bi, hi, qi: (bi, qi, hi, 0)),
                  pl.BlockSpec((1, 1, hps * LANES, tq), lambda bi, hi, qi: (bi, jnp.minimum(qi + 1, last_tile), hi, 0)),
                  pl.BlockSpec(memory_space=pl.ANY),
                  pl.BlockSpec((1, lp // tq, hps * V_ROWS, tq), lambda bi, hi, qi: (bi, 0, hi, 0)),
                  pl.BlockSpec((tq // 2, tq // 2), lambda bi, hi, qi: (0, 0))],
        out_specs=pl.BlockSpec((1, tq, hps * ATTN_HEAD_DIM), lambda bi, hi, qi: (bi, qi, hi)),
        scratch_shapes=[pltpu.VMEM((lp, hps * LANES), BF16),
                        pltpu.SemaphoreType.DMA((2,)),
                        pltpu.VMEM((hps, 1, tq), F32),
                        pltpu.VMEM((hps, V_ROWS, tq), F32),
                        pltpu.VMEM((hps, tq, tq), F32)],
        compiler_params=_cparams(("parallel", "parallel", "arbitrary")),
        name="fox_attention",
    )(qt, qt, ka, vt, causal_bias)


def _causal_conv(x, xp_scr, w_ref, b_ref, first_tile, rows, taps):
    @pl.when(first_tile)
    def _():
        xp_scr[0:SUBLANES, :] = jnp.zeros((SUBLANES, x.shape[1]), F32)

    xp_scr[SUBLANES:SUBLANES + rows, :] = x
    y = b_ref[...] + w_ref[taps - 1:taps, :] * x
    for kk in range(taps - 1):
        r0 = SUBLANES - (taps - 1) + kk
        y = y + w_ref[kk:kk + 1, :] * xp_scr[r0:r0 + rows, :]
    xp_scr[0:SUBLANES, :] = x[rows - SUBLANES:rows, :]
    return y


def _ssd_body(xbc_ref, z_ref, fdt_ref, cw_ref, cb_ref, dtb_ref, alog_ref, dfull_ref, nw_ref,
              exp_ref, o_ref, xp_scr, st_scr):
    rows = SSD_ROWS
    first_tile = pl.program_id(1) == 0

    @pl.when(first_tile)
    def _():
        st_scr[...] = jnp.zeros(st_scr.shape, F32)

    y = _causal_conv(xbc_ref[0].astype(F32), xp_scr, cw_ref, cb_ref, first_tile, rows, SSD_CONV)
    xc = y * _sigmoid(y)
    for ci in range(rows // SSD_CHUNK):
        rs = slice(ci * SSD_CHUNK, (ci + 1) * SSD_CHUNK)
        o_ref[0, rs, :] = _ssd_chunk(xc[rs], z_ref[0, rs, :].astype(F32), fdt_ref[0, rs, :], dtb_ref, alog_ref,
                                     dfull_ref, nw_ref, exp_ref, st_scr).astype(o_ref.dtype)


def _ssd_chunk(xc, z, dt_raw, dtb_ref, alog_ref, dfull_ref, nw_ref, exp_ref, st_scr):
    q = SSD_CHUNK
    gs = D_SSD // SSD_GROUPS
    heads_per_group = SSD_HEADS // SSD_GROUPS

    lane = lax.broadcasted_iota(jnp.int32, (q, LANES), 1)
    dt_lane = (lane >= DT_LANE0) & (lane < DT_LANE0 + SSD_HEADS)
    dt = jnp.where(dt_lane, _softplus(dt_raw + dtb_ref[...]), 0.0)
    a = -jnp.exp(alog_ref[...])
    da = dt * a
    row = lax.broadcasted_iota(jnp.int32, (q, q), 0)
    col = lax.broadcasted_iota(jnp.int32, (q, q), 1)
    lower = row >= col
    a_cum = _dot_01_lhs(lower.astype(BF16), da)
    a_cum_t = a_cum.T
    half = lax.broadcasted_iota(jnp.int32, (q, LANES), 1) < SSD_HEAD_DIM

    outs = []
    for g in range(SSD_GROUPS):
        sl = slice(g * gs, (g + 1) * gs)
        expand = exp_ref[:, sl]
        dt_full = _dot_01_rhs(dt, expand)
        a_cum_full = _dot_01_rhs(a_cum, expand)
        a_last_full = a_cum_full[q - 1:q, :]
        xs = xc[:, sl]
        xdt = xs * dt_full
        xdt_b = xdt.astype(BF16)
        xde_b = (xdt * jnp.exp(a_last_full - a_cum_full)).astype(BF16)

        bm = xc[:, D_SSD + g * SSD_STATE:D_SSD + (g + 1) * SSD_STATE]
        cm = xc[:, D_SSD + SSD_GROUPS * SSD_STATE + g * SSD_STATE:
                D_SSD + SSD_GROUPS * SSD_STATE + (g + 1) * SSD_STATE]
        bm_b = bm.astype(BF16)
        cm_b = cm.astype(BF16)
        cb = _dot_nt(cm_b, bm_b)
        y_pairs = []
        for pair in range(heads_per_group // 2):
            xp = xdt_b[:, pair * LANES:(pair + 1) * LANES]
            ys = []
            for e in range(2):
                hl = DT_LANE0 + g * heads_per_group + 2 * pair + e
                seg = a_cum[:, hl:hl + 1] - a_cum_t[hl:hl + 1, :]
                dec = jnp.exp(jnp.where(lower, seg, -jnp.inf))
                ys.append(_dot((cb * dec).astype(BF16), xp))
            y_pairs.append(jnp.where(half, ys[0], ys[1]))
        y_diag = jnp.concatenate(y_pairs, axis=1)
        prev = st_scr[g]
        y_off = _dot(cm_b, prev.astype(BF16)) * jnp.exp(a_cum_full)
        st_scr[g] = prev * jnp.exp(a_last_full) + _dot(bm.T.astype(BF16), xde_b)
        yg = y_diag + y_off + dfull_ref[:, sl] * xs
        zg = z[:, sl]
        yg = yg * (zg * _sigmoid(zg))
        yg = yg * lax.rsqrt(jnp.mean(yg * yg, axis=-1, keepdims=True) + NORM_EPS)
        outs.append(yg * nw_ref[:, sl])
    return jnp.concatenate(outs, axis=1)


def _lru_body(xr_ref, gate_ref, cw_ref, cb_ref, w2_ref, ba_ref, bx_ref, lam_ref, o_ref,
              xp_scr, h_scr):
    rows = SCAN_ROWS
    first_tile = pl.program_id(1) == 0

    @pl.when(first_tile)
    def _():
        h_scr[...] = jnp.zeros(h_scr.shape, F32)

    xc = _causal_conv(xr_ref[0].astype(F32), xp_scr, cw_ref, cb_ref, first_tile, rows, LRU_CONV)
    xc_b = xc.astype(BF16)
    pre = [_dot(xc_b[:, j * LANES:(j + 1) * LANES], w2_ref[j]) for j in range(D_LRU // LANES)]
    pre_a = jnp.concatenate([p[:, :LANES] for p in pre], axis=1)
    pre_x = jnp.concatenate([p[:, LANES:] for p in pre], axis=1)
    r = _sigmoid(pre_a + ba_ref[...])
    i = _sigmoid(pre_x + bx_ref[...])
    log_a = LRU_C * r * _log_sigmoid(lam_ref[...])
    a = jnp.exp(log_a)
    mult = jnp.sqrt(-jnp.tanh(log_a) * (a * a + 1.0))
    row0 = lax.broadcasted_iota(jnp.int32, (SUBLANES, D_LRU), 0) == 0
    mult = jnp.concatenate([jnp.where(first_tile & row0, 1.0, mult[:SUBLANES]), mult[SUBLANES:]], axis=0)
    u = mult * (i * xc)

    groups = rows // SUBLANES
    a3 = a.reshape(groups, SUBLANES, D_LRU)
    u3 = u.reshape(groups, SUBLANES, D_LRU)
    sub = lax.broadcasted_iota(jnp.int32, (groups, SUBLANES, D_LRU), 1)
    d = 1
    while d < SUBLANES:
        keep = sub >= d
        a_s = jnp.where(keep, pltpu.roll(a3, d, 1), 1.0)
        u_s = jnp.where(keep, pltpu.roll(u3, d, 1), 0.0)
        u3 = a3 * u_s + u3
        a3 = a3 * a_s
        d *= 2
    h_prev = h_scr[0:1, :]
    hs = []
    for r in range(groups):
        h_r = a3[r] * h_prev + u3[r]
        hs.append(h_r)
        h_prev = h_r[SUBLANES - 1:SUBLANES, :]
    h = jnp.concatenate(hs, axis=0)
    h_scr[0:1, :] = h[rows - 1:rows, :]
    o_ref[0] = (h * jax.nn.gelu(gate_ref[0].astype(F32))).astype(o_ref.dtype)


def _recurrent_body(xbc_ref, z_ref, fdt_ref, s_cw_ref, s_cb_ref, dtb_ref, alog_ref, dfull_ref, nw_ref, exp_ref,
                    xr_ref, gate_ref, l_cw_ref, l_cb_ref, w2_ref, ba_ref, bx_ref, lam_ref,
                    yb_ref, yc_ref, s_xp_scr, st_scr, l_xp_scr, h_scr):
    _lru_body(xr_ref, gate_ref, l_cw_ref, l_cb_ref, w2_ref, ba_ref, bx_ref, lam_ref, yc_ref, l_xp_scr, h_scr)
    _ssd_body(xbc_ref, z_ref, fdt_ref, s_cw_ref, s_cb_ref, dtb_ref, alog_ref, dfull_ref, nw_ref, exp_ref,
              yb_ref, s_xp_scr, st_scr)


def _recurrent(xbc, z, fdt, ssd_p, xr, gate, lru_p, b, lp):
    rows = SSD_ROWS
    tile = lambda width: pl.BlockSpec((1, rows, width), lambda bi, ti: (bi, ti, 0))
    params_s, params_l = tuple(ssd_p), tuple(lru_p)
    return pl.pallas_call(
        _recurrent_body,
        out_shape=[jax.ShapeDtypeStruct((b, lp, D_SSD), BF16), jax.ShapeDtypeStruct((b, lp, D_LRU), BF16)],
        grid=(b, lp // rows),
        in_specs=([tile(D_XBC), tile(D_SSD), tile(FDT_COLS)] + [_const_spec(p.shape) for p in params_s]
                  + [tile(D_LRU), tile(D_LRU)] + [_const_spec(p.shape) for p in params_l]),
        out_specs=[tile(D_SSD), tile(D_LRU)],
        scratch_shapes=[pltpu.VMEM((SUBLANES + rows, D_XBC), F32),
                        pltpu.VMEM((SSD_GROUPS, SSD_STATE, D_SSD // SSD_GROUPS), F32),
                        pltpu.VMEM((SUBLANES + rows, D_LRU), F32),
                        pltpu.VMEM((SUBLANES, D_LRU), F32)],
        compiler_params=_cparams(("parallel", "arbitrary")),
        name="recurrent",
    )(xbc, z, fdt, *params_s, xr, gate, *params_l)


def _merge_apply(h_ref, ya_ref, yb_ref, yc_ref, m_ref, wa_ref, wb_ref, wc_ref, wo_ref):
    gate = lambda i: _sigmoid(m_ref[:, i * D_MODEL:(i + 1) * D_MODEL].astype(F32))
    mixed = gate(0) * _dot(ya_ref[...], wa_ref[...])
    mixed = mixed + gate(1) * _dot(yb_ref[...], wb_ref[...])
    mixed = mixed + gate(2) * _dot(yc_ref[...], wc_ref[...])
    return h_ref[...] + _dot(mixed.astype(BF16), wo_ref[...])


def _merge_body(h_ref, ya_ref, yb_ref, yc_ref, m_ref, wa_ref, wb_ref, wc_ref, wo_ref, o_ref):
    o_ref[...] = _merge_apply(h_ref, ya_ref, yb_ref, yc_ref, m_ref, wa_ref, wb_ref, wc_ref, wo_ref)


def _merge_ffn_body(h_ref, ya_ref, yb_ref, yc_ref, m_ref, wa_ref, wb_ref, wc_ref, wo_ref,
                    g_ref, wgu_ref, wd_ref, o_ref, a_scr):
    h = _merge_apply(h_ref, ya_ref, yb_ref, yc_ref, m_ref, wa_ref, wb_ref, wc_ref, wo_ref)
    o_ref[...] = _ffn_apply(h, g_ref, wgu_ref, wd_ref, a_scr)


def _merge_ffn(h, ya, yb, yc, m, wa, wb, wc, wo, g, wgu, wd):
    t = h.shape[0]
    tm = FUSED_ROWS
    row_spec = lambda width: pl.BlockSpec((tm, width), lambda i: (i, 0))
    return pl.pallas_call(
        _merge_ffn_body,
        out_shape=jax.ShapeDtypeStruct((t, D_MODEL), F32),
        grid=(t // tm,),
        in_specs=[row_spec(D_MODEL), row_spec(D_ATTN), row_spec(D_SSD), row_spec(D_LRU),
                  row_spec(N_BRANCH * D_MODEL),
                  _const_spec((D_ATTN, D_MODEL)), _const_spec((D_SSD, D_MODEL)),
                  _const_spec((D_LRU, D_MODEL)), _const_spec((D_MODEL, D_MODEL)),
                  _const_spec((1, D_MODEL)), _const_spec((D_MODEL, 2 * D_FF)), _const_spec((D_FF, D_MODEL))],
        out_specs=row_spec(D_MODEL),
        scratch_shapes=[pltpu.VMEM((tm, D_FF), BF16)],
        compiler_params=_cparams(("parallel",)),
        name="merge_ffn",
    )(h, ya, yb, yc, m, wa, wb, wc, wo, g, wgu, wd)


def _merge(h, ya, yb, yc, m, wa, wb, wc, wo):
    t = h.shape[0]
    tm = MERGE_ROWS
    row_spec = lambda width: pl.BlockSpec((tm, width), lambda i: (i, 0))
    return pl.pallas_call(
        _merge_body,
        out_shape=jax.ShapeDtypeStruct((t, D_MODEL), F32),
        grid=(t // tm,),
        in_specs=[row_spec(D_MODEL), row_spec(D_ATTN), row_spec(D_SSD), row_spec(D_LRU),
                  row_spec(N_BRANCH * D_MODEL),
                  _const_spec((D_ATTN, D_MODEL)), _const_spec((D_SSD, D_MODEL)),
                  _const_spec((D_LRU, D_MODEL)), _const_spec((D_MODEL, D_MODEL))],
        out_specs=row_spec(D_MODEL),
        compiler_params=_cparams(("parallel",)),
        name="merge_out",
    )(h, ya, yb, yc, m, wa, wb, wc, wo)


def _prep_w_in(w_in):
    sizes = (D_ATTN, D_ATTN, D_ATTN, ATTN_HEADS, D_SSD, D_XBC, SSD_HEADS, D_LRU, D_LRU, N_BRANCH * D_MODEL)
    offs = [0]
    for s in sizes:
        offs.append(offs[-1] + s)
    part = lambda i: w_in[:, offs[i]:offs[i + 1]]
    q, k, v, f, z, xbc, dt, xr, gate, merge = (part(i) for i in range(10))
    pad = jnp.zeros((D_MODEL, FDT_COLS - ATTN_HEADS - SSD_HEADS), w_in.dtype)
    w = jnp.concatenate([k, z, xbc, xr, gate, merge, f, dt, pad], axis=1).astype(BF16)
    wt = jnp.concatenate([q.T, v.T], axis=0).astype(BF16)
    return w, wt


def _pad_lanes(vec, lane0):
    out = jnp.zeros((1, LANES), F32)
    return out.at[0, lane0:lane0 + vec.shape[0]].set(vec.astype(F32))


def _lru_gate_weights(w_a, w_x):
    def blockdiag_pairs(w):
        w = w.reshape(LRU_BLOCKS // 2, 2, LRU_BLOCK_DIM, LRU_BLOCK_DIM)
        zero = jnp.zeros_like(w[:, 0])
        top = jnp.concatenate([w[:, 0], zero], axis=2)
        bot = jnp.concatenate([zero, w[:, 1]], axis=2)
        return jnp.concatenate([top, bot], axis=1)
    return jnp.concatenate([blockdiag_pairs(w_a), blockdiag_pairs(w_x)], axis=2).astype(BF16)


def _head_expand():
    rows = jnp.arange(LANES)[:, None]
    cols = jnp.arange(D_SSD)[None, :]
    return (rows == DT_LANE0 + cols // SSD_HEAD_DIM).astype(BF16)


def kernel(x, meta_tokens, ffn1_norm, ffn1_w_gate_up, ffn1_w_down, mix_norm, w_in, fox_forget_bias,
           ssd_conv_w, ssd_conv_b, ssd_dt_bias, ssd_a_log, ssd_d, ssd_norm,
           lru_conv_w, lru_conv_b, lru_w_a, lru_b_a, lru_w_x, lru_b_x, lru_lambda,
           w_branch_attn, w_branch_ssd, w_branch_lru, w_out,
           ffn2_norm, ffn2_w_gate_up, ffn2_w_down, final_norm):
    b, s, d = x.shape
    depth = w_in.shape[0]
    length = N_META + s
    lp = -(-length // SEQ_ALIGN) * SEQ_ALIGN
    t = b * lp
    assert PROJ_ROWS == ATTN_TQ
    assert d == D_MODEL and t % FFN_ROWS == 0 and t % MERGE_ROWS == 0 and t % FUSED_ROWS == 0 and s % FFN_ROWS == 0

    meta = jnp.broadcast_to(meta_tokens.astype(x.dtype)[None], (b, N_META, d))
    h = jnp.concatenate([meta, x, jnp.zeros((b, lp - length, d), x.dtype)], axis=1).reshape(t, d)

    row = lambda vec: vec.astype(F32).reshape(1, -1)
    expand = _head_expand()
    pw = _aux_constants()
    fg = row(final_norm)
    for l in range(depth):
        h = _ffn(h, row(ffn1_norm[l]), ffn1_w_gate_up[l].astype(BF16), ffn1_w_down[l].astype(BF16), fg, False)

        w_std, w_t = _prep_w_in(w_in[l])
        qt, ka, vt, z, xbc, xr, gate, merge, fdt = _inproj(
            h, row(mix_norm[l]), w_std, w_t, _pad_lanes(fox_forget_bias[l], 0), pw, b, lp)
        y_a = _attention(qt, ka, vt, b, lp).reshape(t, D_ATTN)
        ssd_p = (ssd_conv_w[l].astype(F32), row(ssd_conv_b[l]), _pad_lanes(ssd_dt_bias[l], DT_LANE0),
                 _pad_lanes(ssd_a_log[l], DT_LANE0), row(jnp.repeat(ssd_d[l], SSD_HEAD_DIM)), row(ssd_norm[l]), expand)
        lru_p = (lru_conv_w[l].astype(F32), row(lru_conv_b[l]), _lru_gate_weights(lru_w_a[l], lru_w_x[l]),
                 row(lru_b_a[l]), row(lru_b_x[l]), row(lru_lambda[l]))
        y_b, y_c = _recurrent(xbc, z, fdt, ssd_p, xr, gate, lru_p, b, lp)
        y_b = y_b.reshape(t, D_SSD)
        y_c = y_c.reshape(t, D_LRU)
        branch = (y_a, y_b, y_c, merge.reshape(t, N_BRANCH * D_MODEL), w_branch_attn[l].astype(BF16),
                  w_branch_ssd[l].astype(BF16), w_branch_lru[l].astype(BF16), w_out[l].astype(BF16))
        ffn2 = (row(ffn2_norm[l]), ffn2_w_gate_up[l].astype(BF16), ffn2_w_down[l].astype(BF16))
        if l < depth - 1:
            h = _merge_ffn(h, *branch, *ffn2)
        else:
            h = _merge(h, *branch)
    return _ffn_final(h, *ffn2, fg, b, lp, s)
```
